```python
import math
import jax, jax.numpy as jnp
from jax import lax
import numpy as np

D_MODEL = 1024
BATCH = 4
SEQ = 4096
DEPTH = 2

D_CONV = D_MODEL // 2
CONV_WIDTH = 31
D_SSM = D_MODEL // 2
SSM_GROUP = 16
N_SSM_GROUPS = D_SSM // SSM_GROUP
SSM_STATE = 64
N_DIR = 2
DT_MIN = 1e-3
DT_MAX = 1e-1
D_IN = 2 * D_CONV + D_SSM + 2 * D_MODEL
D_FF = ((8 * D_MODEL // 3 + 127) // 128) * 128
N_EXPERTS = 8
TOP_K = 2
N_DENSE = (DEPTH + 1) // 2
N_MOE = DEPTH // 2
RMS_EPS = 1e-6
LN_EPS = 1e-5

kernel_name = "hybrid_conv_s5_moe_encoder"


def rms_norm(x, g):
    xf = x.astype(jnp.float32)
    y = xf * lax.rsqrt(jnp.mean(xf * xf, axis=-1, keepdims=True) + RMS_EPS)
    return (y * g.astype(jnp.float32)).astype(x.dtype)


def layer_norm(x, g, b):
    xf = x.astype(jnp.float32)
    mu = jnp.mean(xf, axis=-1, keepdims=True)
    var = jnp.mean(jnp.square(xf - mu), axis=-1, keepdims=True)
    y = (xf - mu) * lax.rsqrt(var + LN_EPS)
    return (y * g.astype(jnp.float32) + b.astype(jnp.float32)).astype(x.dtype)


def conformer_conv(v, gate, dw_w, dw_b, ln_g, ln_b, w_proj):
    h = v * jax.nn.sigmoid(gate)
    half = CONV_WIDTH // 2
    h = lax.conv_general_dilated(
        h, dw_w[:, None, :], window_strides=(1,), padding=[(half, half)],
        dimension_numbers=('NWC', 'WIO', 'NWC'),
        feature_group_count=D_CONV) + dw_b
    h = jax.nn.silu(layer_norm(h, ln_g, ln_b))
    return h @ w_proj


def zoh_discretise(a_re, a_im, log_dt, b_re, b_im):
    dt = jnp.exp(log_dt)[:, None]
    mag = jnp.exp(a_re * dt)
    ab_re = mag * jnp.cos(a_im * dt)
    ab_im = mag * jnp.sin(a_im * dt)
    n_re = ab_re - 1.0
    n_im = ab_im
    den = a_re * a_re + a_im * a_im
    q_re = ((n_re * a_re + n_im * a_im) / den)[..., None]
    q_im = ((n_im * a_re - n_re * a_im) / den)[..., None]
    bb_re = q_re * b_re - q_im * b_im
    bb_im = q_re * b_im + q_im * b_re
    return ab_re, ab_im, bb_re, bb_im


def linear_recurrence_combine(left, right):
    a1r, a1i, b1r, b1i = left
    a2r, a2i, b2r, b2i = right
    ar = a2r * a1r - a2i * a1i
    ai = a2r * a1i + a2i * a1r
    br = a2r * b1r - a2i * b1i + b2r
    bi = a2r * b1i + a2i * b1r + b2i
    return ar, ai, br, bi


def s5_bidirectional(u, a_re, a_im, log_dt, b_re, b_im, c_re, c_im, d_skip):
    bsz, seq_len, _ = u.shape
    uf = u.astype(jnp.float32)
    ug = uf.reshape(bsz, seq_len, N_SSM_GROUPS, SSM_GROUP)
    y = d_skip.astype(jnp.float32) * uf
    b_re32 = b_re.astype(jnp.float32)
    b_im32 = b_im.astype(jnp.float32)
    for direction in range(N_DIR):
        ab_re, ab_im, bb_re, bb_im = zoh_discretise(
            a_re[direction].astype(jnp.float32), a_im[direction].astype(jnp.float32),
            log_dt[direction].astype(jnp.float32), b_re32, b_im32)
        bu_re = jnp.einsum('blgc,gpc->blgp', ug, bb_re)
        bu_im = jnp.einsum('blgc,gpc->blgp', ug, bb_im)
        a_seq_re = jnp.broadcast_to(ab_re, (1, seq_len, N_SSM_GROUPS, SSM_STATE))
        a_seq_im = jnp.broadcast_to(ab_im, (1, seq_len, N_SSM_GROUPS, SSM_STATE))
        _, _, s_re, s_im = lax.associative_scan(
            linear_recurrence_combine, (a_seq_re, a_seq_im, bu_re, bu_im),
            reverse=(direction == 1), axis=1)
        cr = c_re[direction].astype(jnp.float32)
        ci = c_im[direction].astype(jnp.float32)
        y_dir = (jnp.einsum('blgp,gcp->blgc', s_re, cr)
                 - jnp.einsum('blgp,gcp->blgc', s_im, ci))
        y = y + y_dir.reshape(bsz, seq_len, D_SSM)
    return y.astype(u.dtype)


def swiglu(h, w_gate, w_up, w_down):
    return (jax.nn.silu(h @ w_gate) * (h @ w_up)) @ w_down


def top2_moe(h, router_w, router_b, w_gate, w_up, w_down):
    bsz, seq_len, d = h.shape
    t = h.reshape(bsz * seq_len, d)
    logits = (t @ router_w + router_b).astype(jnp.float32)
    top_val, top_idx = lax.top_k(logits, TOP_K)
    top_w = jax.nn.softmax(top_val, axis=-1)
    gates = jnp.sum(jax.nn.one_hot(top_idx, N_EXPERTS, dtype=jnp.float32)
                    * top_w[..., None], axis=1)
    out = jnp.zeros((bsz * seq_len, d), jnp.float32)
    for e in range(N_EXPERTS):
        out = out + gates[:, e:e + 1] * swiglu(t, w_gate[e], w_up[e], w_down[e]).astype(jnp.float32)
    return out.astype(h.dtype).reshape(bsz, seq_len, d)


def setup_inputs(seed: int = 0) -> dict:
    key = jax.random.key(seed)
    ks = iter(jax.random.split(key, 40))
    f32 = jnp.float32

    def nrm(shape, scale):
        return jax.random.normal(next(ks), shape, f32) * scale

    x = jax.random.normal(next(ks), (BATCH, SEQ, D_MODEL), f32)
    norm_mix_g = 1.0 + nrm((DEPTH, D_MODEL), 0.02)
    w_in = nrm((DEPTH, D_MODEL, D_IN), D_MODEL ** -0.5)
    conv_dw_w = nrm((DEPTH, CONV_WIDTH, D_CONV), CONV_WIDTH ** -0.5)
    conv_dw_b = nrm((DEPTH, D_CONV), 0.02)
    conv_ln_g = 1.0 + nrm((DEPTH, D_CONV), 0.02)
    conv_ln_b = nrm((DEPTH, D_CONV), 0.02)
    w_conv_proj = nrm((DEPTH, D_CONV, D_MODEL), D_CONV ** -0.5)
    n_idx = jnp.arange(SSM_STATE, dtype=f32)
    ssm_a_re = -0.5 + nrm((DEPTH, N_DIR, N_SSM_GROUPS, SSM_STATE), 0.01)
    ssm_a_im = math.pi * n_idx + nrm((DEPTH, N_DIR, N_SSM_GROUPS, SSM_STATE), 0.01)
    ssm_log_dt = jax.random.uniform(next(ks), (DEPTH, N_DIR, N_SSM_GROUPS), f32,
                                    math.log(DT_MIN), math.log(DT_MAX))
    b_scale = (2.0 * SSM_GROUP) ** -0.5
    ssm_b_re = nrm((DEPTH, N_SSM_GROUPS, SSM_STATE, SSM_GROUP), b_scale)
    ssm_b_im = nrm((DEPTH, N_SSM_GROUPS, SSM_STATE, SSM_GROUP), b_scale)
    c_scale = (2.0 * SSM_STATE) ** -0.5
    ssm_c_re = nrm((DEPTH, N_DIR, N_SSM_GROUPS, SSM_GROUP, SSM_STATE), c_scale)
    ssm_c_im = nrm((DEPTH, N_DIR, N_SSM_GROUPS, SSM_GROUP, SSM_STATE), c_scale)
    ssm_d = nrm((DEPTH, D_SSM), 1.0)
    ssm_w_glu = nrm((DEPTH, D_SSM, D_SSM), D_SSM ** -0.5)
    ssm_b_glu = nrm((DEPTH, D_SSM), 0.02)
    w_ssm_proj = nrm((DEPTH, D_SSM, D_MODEL), D_SSM ** -0.5)
    w_out = nrm((DEPTH, D_MODEL, D_MODEL), D_MODEL ** -0.5)
    norm_ffn_g = 1.0 + nrm((DEPTH, D_MODEL), 0.02)
    ffn_w_gate = nrm((N_DENSE, D_MODEL, D_FF), D_MODEL ** -0.5)
    ffn_w_up = nrm((N_DENSE, D_MODEL, D_FF), D_MODEL ** -0.5)
    ffn_w_down = nrm((N_DENSE, D_FF, D_MODEL), D_FF ** -0.5)
    router_w = nrm((N_MOE, D_MODEL, N_EXPERTS), D_MODEL ** -0.5)
    router_b = nrm((N_MOE, N_EXPERTS), 0.01)
    moe_w_gate = nrm((N_MOE, N_EXPERTS, D_MODEL, D_FF), D_MODEL ** -0.5)
    moe_w_up = nrm((N_MOE, N_EXPERTS, D_MODEL, D_FF), D_MODEL ** -0.5)
    moe_w_down = nrm((N_MOE, N_EXPERTS, D_FF, D_MODEL), D_FF ** -0.5)
    final_norm_g = 1.0 + nrm((D_MODEL,), 0.02)
    return {
        "x": x, "norm_mix_g": norm_mix_g, "w_in": w_in,
        "conv_dw_w": conv_dw_w, "conv_dw_b": conv_dw_b,
        "conv_ln_g": conv_ln_g, "conv_ln_b": conv_ln_b, "w_conv_proj": w_conv_proj,
        "ssm_a_re": ssm_a_re, "ssm_a_im": ssm_a_im, "ssm_log_dt": ssm_log_dt,
        "ssm_b_re": ssm_b_re, "ssm_b_im": ssm_b_im,
        "ssm_c_re": ssm_c_re, "ssm_c_im": ssm_c_im, "ssm_d": ssm_d,
        "ssm_w_glu": ssm_w_glu, "ssm_b_glu": ssm_b_glu, "w_ssm_proj": w_ssm_proj,
        "w_out": w_out, "norm_ffn_g": norm_ffn_g,
        "ffn_w_gate": ffn_w_gate, "ffn_w_up": ffn_w_up, "ffn_w_down": ffn_w_down,
        "router_w": router_w, "router_b": router_b,
        "moe_w_gate": moe_w_gate, "moe_w_up": moe_w_up, "moe_w_down": moe_w_down,
        "final_norm_g": final_norm_g,
    }


def reference(x, norm_mix_g, w_in, conv_dw_w, conv_dw_b, conv_ln_g, conv_ln_b, w_conv_proj,
              ssm_a_re, ssm_a_im, ssm_log_dt, ssm_b_re, ssm_b_im, ssm_c_re, ssm_c_im, ssm_d,
              ssm_w_glu, ssm_b_glu, w_ssm_proj, w_out, norm_ffn_g,
              ffn_w_gate, ffn_w_up, ffn_w_down, router_w, router_b,
              moe_w_gate, moe_w_up, moe_w_down, final_norm_g):
    split_at = [D_CONV, 2 * D_CONV, 2 * D_CONV + D_SSM, 2 * D_CONV + D_SSM + D_MODEL]
    for layer in range(DEPTH):
        h = rms_norm(x, norm_mix_g[layer])
        z = h @ w_in[layer]
        conv_v, conv_g, ssm_u, gate_conv, gate_ssm = jnp.split(z, split_at, axis=-1)
        br_conv = conformer_conv(conv_v, conv_g, conv_dw_w[layer], conv_dw_b[layer],
                                 conv_ln_g[layer], conv_ln_b[layer], w_conv_proj[layer])
        y = s5_bidirectional(ssm_u, ssm_a_re[layer], ssm_a_im[layer], ssm_log_dt[layer],
                             ssm_b_re[layer], ssm_b_im[layer], ssm_c_re[layer],
                             ssm_c_im[layer], ssm_d[layer])
        y = jax.nn.gelu(y)
        y = y * jax.nn.sigmoid(y @ ssm_w_glu[layer] + ssm_b_glu[layer])
        br_ssm = y @ w_ssm_proj[layer]
        merged = jax.nn.sigmoid(gate_conv) * br_conv + jax.nn.sigmoid(gate_ssm) * br_ssm
        x = x + merged @ w_out[layer]
        h = rms_norm(x, norm_ffn_g[layer])
        if layer % 2 == 0:
            i = layer // 2
            x = x + swiglu(h, ffn_w_gate[i], ffn_w_up[i], ffn_w_down[i])
        else:
            i = layer // 2
            x = x + top2_moe(h, router_w[i], router_b[i], moe_w_gate[i],
                             moe_w_up[i], moe_w_down[i])
    return rms_norm(x, final_norm_g)
```

```python
import functools
import math

import jax
import jax.numpy as jnp
from jax import lax
from jax.experimental import pallas as pl
from jax.experimental.pallas import tpu as pltpu

F32 = jnp.float32
BF16 = jnp.bfloat16

RMS_EPS = 1e-6
LN_EPS = 1e-5
CONV_WIDTH = 31
CONV_HALF = CONV_WIDTH // 2
SSM_GROUP = 16
SSM_STATE = 64
CHUNK = 16
TOP_K = 2
LANES = 128
SUBLANES = 8
VMEM_LIMIT = 56 * 1024 * 1024

ROW_TILE = 512
MOE_TILE = 256
PERM_TILE = 256
CONV_TILE = 128
FF_CHUNK = 256


def _params(*sem):
    return pltpu.CompilerParams(dimension_semantics=sem, vmem_limit_bytes=VMEM_LIMIT)


def _rms(xf, g):
    ms = jnp.mean(xf * xf, axis=-1, keepdims=True)
    return xf * lax.rsqrt(ms + RMS_EPS) * g


def _inproj_kernel(x_ref, g_ref, w_ref, ch_ref, u_ref, gate_ref):
    dc = ch_ref.shape[-1]
    h = _rms(x_ref[...], g_ref[...]).astype(BF16)
    v = jnp.dot(h, w_ref[:, 0:dc], preferred_element_type=F32)
    gt = jnp.dot(h, w_ref[:, dc:2 * dc], preferred_element_type=F32)
    ch_ref[...] = v * jax.nn.sigmoid(gt)
    u_ref[...] = jnp.dot(h, w_ref[:, 2 * dc:3 * dc], preferred_element_type=F32).astype(BF16)
    gate_ref[...] = jnp.dot(h, w_ref[:, 3 * dc:], preferred_element_type=F32).astype(BF16)


def _inproj(x2, g, w_bf, d_conv, d_ssm):
    n, d = x2.shape
    d_in = w_bf.shape[1]
    d_gate = d_in - 2 * d_conv - d_ssm
    tm = ROW_TILE
    return pl.pallas_call(
        _inproj_kernel,
        grid=(n // tm,),
        in_specs=[
            pl.BlockSpec((tm, d), lambda i: (i, 0)),
            pl.BlockSpec((1, d), lambda i: (0, 0)),
            pl.BlockSpec((d, d_in), lambda i: (0, 0)),
        ],
        out_specs=[
            pl.BlockSpec((tm, d_conv), lambda i: (i, 0)),
            pl.BlockSpec((tm, d_ssm), lambda i: (i, 0)),
            pl.BlockSpec((tm, d_gate), lambda i: (i, 0)),
        ],
        out_shape=[
            jax.ShapeDtypeStruct((n, d_conv), F32),
            jax.ShapeDtypeStruct((n, d_ssm), BF16),
            jax.ShapeDtypeStruct((n, d_gate), BF16),
        ],
        compiler_params=_params("parallel"),
        name="inproj",
    )(x2, g.reshape(1, d), w_bf)


def _conv_kernel(h_ref, w_ref, b_ref, lg_ref, lb_ref, o_ref, pad_scr, tile_scr):
    seq, c = h_ref.shape[1], h_ref.shape[2]
    front = 2 * SUBLANES
    zeros = jnp.zeros((front, c), F32)
    pad_scr[0:front, :] = zeros
    pad_scr[front + seq:front + seq + front, :] = zeros
    pad_scr[front:front + seq, :] = h_ref[0]

    def tile_body(ti, carry):
        base = pl.multiple_of(ti * CONV_TILE, CONV_TILE)
        win_ref = pad_scr.at[pl.ds(base, CONV_TILE + 2 * front), :]
        for lb in range(c // LANES):
            ls = slice(lb * LANES, (lb + 1) * LANES)
            acc = jnp.zeros((CONV_TILE, LANES), F32)
            for k in range(CONV_WIDTH):
                rows = pl.ds(k - CONV_HALF + front, CONV_TILE)
                acc = acc + w_ref[k:k + 1, ls] * win_ref[rows, ls]
            tile_scr[:, ls] = acc + b_ref[:, ls]
        hh = tile_scr[...]
        mu = jnp.mean(hh, axis=-1, keepdims=True)
        cen = hh - mu
        var = jnp.mean(cen * cen, axis=-1, keepdims=True)
        y = cen * lax.rsqrt(var + LN_EPS) * lg_ref[...] + lb_ref[...]
        o_ref[0, pl.ds(base, CONV_TILE), :] = (y * jax.nn.sigmoid(y)).astype(BF16)
        return carry

    lax.fori_loop(0, seq // CONV_TILE, tile_body, 0)


def _conv_branch(ch, dw_w, dw_b, ln_g, ln_b):
    b, seq, c = ch.shape
    front = 2 * SUBLANES
    vec = lambda a: a.reshape(1, c)
    return pl.pallas_call(
        _conv_kernel,
        grid=(b,),
        in_specs=[
            pl.BlockSpec((1, seq, c), lambda i: (i, 0, 0)),
            pl.BlockSpec((CONV_WIDTH, c), lambda i: (0, 0)),
            pl.BlockSpec((1, c), lambda i: (0, 0)),
            pl.BlockSpec((1, c), lambda i: (0, 0)),
            pl.BlockSpec((1, c), lambda i: (0, 0)),
        ],
        out_specs=pl.BlockSpec((1, seq, c), lambda i: (i, 0, 0)),
        out_shape=jax.ShapeDtypeStruct((b, seq, c), BF16),
        scratch_shapes=[
            pltpu.VMEM((seq + 2 * front, c), F32),
            pltpu.VMEM((CONV_TILE, c), F32),
        ],
        compiler_params=_params("parallel"),
        name="conv_branch",
    )(ch, dw_w, vec(dw_b), vec(ln_g), vec(ln_b))


def _s5_matrices(a_re, a_im, log_dt, b_re, b_im, c_re, c_im, d_skip):
    t = CHUNK
    hi = lax.Precision.HIGHEST
    n_dir, g, p = a_re.shape
    c = b_re.shape[-1]
    a_re, a_im, log_dt = a_re.astype(F32), a_im.astype(F32), log_dt.astype(F32)
    b_re, b_im = b_re.astype(F32), b_im.astype(F32)
    c_re, c_im = c_re.astype(F32), c_im.astype(F32)
    dt = jnp.exp(log_dt)[:, :, None]
    lam_re, lam_im = a_re * dt, a_im * dt
    ks = jnp.arange(t + 1, dtype=F32)[:, None, None, None]
    mag = jnp.exp(ks * lam_re)
    pw_re, pw_im = mag * jnp.cos(ks * lam_im), mag * jnp.sin(ks * lam_im)
    ab_re, ab_im = pw_re[1], pw_im[1]
    n_re, n_im = ab_re - 1.0, ab_im
    den = a_re * a_re + a_im * a_im
    q_re = ((n_re * a_re + n_im * a_im) / den)[..., None]
    q_im = ((n_im * a_re - n_re * a_im) / den)[..., None]
    bb_re = q_re * b_re - q_im * b_im
    bb_im = q_re * b_im + q_im * b_re
    wb_re = pw_re[..., None] * bb_re - pw_im[..., None] * bb_im
    wb_im = pw_re[..., None] * bb_im + pw_im[..., None] * bb_re
    pc_re = pw_re[:, :, :, None, :]
    pc_im = pw_im[:, :, :, None, :]
    vc_re = c_re * pc_re - c_im * pc_im
    vc_im = c_re * pc_im + c_im * pc_re
    kk = (jnp.einsum("dgop,kdgpc->kdgoc", c_re, wb_re[:t], precision=hi)
          - jnp.einsum("dgop,kdgpc->kdgoc", c_im, wb_im[:t], precision=hi))
    kf, kb = kk[:, 0], kk[:, 1]
    jj = jnp.arange(t)[:, None]
    ii = jnp.arange(t)[None, :]
    lag = ii - jj
    lag_abs = jnp.abs(lag)
    diag = kf[0] + kb[0] + d_skip.astype(F32).reshape(g, c)[:, :, None] * jnp.eye(c, dtype=F32)
    sel = lag[:, :, None, None, None]
    m_intra = jnp.where(sel > 0, kf[lag_abs], jnp.where(sel < 0, kb[lag_abs], diag[None, None]))
    m_intra = m_intra.transpose(2, 0, 4, 1, 3).reshape(g, t * c, t * c)

    def zmat(w, order):
        return w[order].transpose(1, 0, 3, 2).reshape(g, t * c, p)

    of = jnp.arange(t - 1, -1, -1)
    ob = jnp.arange(t)
    zf_re, zf_im = zmat(wb_re[:, 0], of), zmat(wb_im[:, 0], of)
    zb_re, zb_im = zmat(wb_re[:, 1], ob), zmat(wb_im[:, 1], ob)
    m_a = jnp.concatenate([zf_re, zf_im, zf_im, zf_re, zb_re, zb_im, zb_im, zb_re], axis=-1)

    def omat(v, order):
        return v[order].transpose(1, 3, 0, 2).reshape(g, p, t * c)

    pf = jnp.arange(1, t + 1)
    pb = jnp.arange(t, 0, -1)
    m_b = jnp.concatenate([
        m_intra,
        omat(vc_re[:, 0], pf), -omat(vc_im[:, 0], pf),
        omat(vc_re[:, 1], pb), -omat(vc_im[:, 1], pb)], axis=1)
    at_re, at_im = pw_re[t], pw_im[t]
    rows = [jnp.concatenate([at_re[0], at_re[0]], -1), jnp.concatenate([-at_im[0], at_im[0]], -1),
            jnp.concatenate([at_re[1], at_re[1]], -1), jnp.concatenate([-at_im[1], at_im[1]], -1)]
    coef = jnp.stack(rows + [jnp.zeros_like(rows[0])] * (SUBLANES - len(rows)), axis=1)
    return m_a.astype(BF16), m_b.astype(BF16), coef


def _s5_kernel(x_ref, ma_ref, mb_ref, coef_ref, y_ref, z_scr, cin_scr):
    rows, kx = x_ref.shape[1], x_ref.shape[2]
    sw = coef_ref.shape[-1]
    n_chunks = rows // SUBLANES
    rc = min(rows, 512)
    for r0 in range(0, rows, rc):
        z_scr[r0:r0 + rc, :] = jnp.dot(x_ref[0, r0:r0 + rc, :], ma_ref[0],
                                       preferred_element_type=F32)
    coef = coef_ref[0]
    bc = lambda r: jnp.broadcast_to(coef[r:r + 1, :], (SUBLANES, sw))
    af1, af2, ab1, ab2 = bc(0), bc(1), bc(2), bc(3)

    def step(ci, carry):
        sf, sfp, sb, sbp = carry
        rf = pl.multiple_of(ci * SUBLANES, SUBLANES)
        rb = pl.multiple_of((n_chunks - 1 - ci) * SUBLANES, SUBLANES)
        cin_scr[pl.ds(rf, SUBLANES), 0:sw] = sf
        cin_scr[pl.ds(rb, SUBLANES), sw:2 * sw] = sb
        zf = z_scr[pl.ds(rf, SUBLANES), 0:sw]
        zfp = z_scr[pl.ds(rf, SUBLANES), sw:2 * sw]
        zb = z_scr[pl.ds(rb, SUBLANES), 2 * sw:3 * sw]
        zbp = z_scr[pl.ds(rb, SUBLANES), 3 * sw:4 * sw]
        return (af1 * sf + af2 * sfp + zf, af1 * sfp - af2 * sf + zfp,
                ab1 * sb + ab2 * sbp + zb, ab1 * sbp - ab2 * sb + zbp)

    zero = jnp.zeros((SUBLANES, sw), F32)
    lax.fori_loop(0, n_chunks, step, (zero, zero, zero, zero), unroll=4)

    for r0 in range(0, rows, rc):
        y_ref[0, r0:r0 + rc, :] = (
            jnp.dot(x_ref[0, r0:r0 + rc, :], mb_ref[0, 0:kx, :], preferred_element_type=F32)
            + jnp.dot(cin_scr[r0:r0 + rc, :].astype(BF16), mb_ref[0, kx:, :],
                      preferred_element_type=F32))


def _s5_branch(u, m_a, m_b, coef):
    b, seq, d = u.shape
    g = m_a.shape[0]
    c = d // g
    t = CHUNK
    n_chunks = seq // t
    assert b <= SUBLANES and seq % t == 0
    x = u.reshape(b, n_chunks, t, g, c).transpose(3, 1, 0, 2, 4)
    x = jnp.pad(x, ((0, 0), (0, 0), (0, SUBLANES - b), (0, 0), (0, 0)))
    rows = n_chunks * SUBLANES
    x = x.reshape(g, rows, t * c)
    kx, nz, kb = t * c, m_a.shape[-1], m_b.shape[1]
    y = pl.pallas_call(
        _s5_kernel,
        grid=(g,),
        in_specs=[
            pl.BlockSpec((1, rows, kx), lambda i: (i, 0, 0)),
            pl.BlockSpec((1, kx, nz), lambda i: (i, 0, 0)),
            pl.BlockSpec((1, kb, kx), lambda i: (i, 0, 0)),
            pl.BlockSpec((1, SUBLANES, coef.shape[-1]), lambda i: (i, 0, 0)),
        ],
        out_specs=pl.BlockSpec((1, rows, kx), lambda i: (i, 0, 0)),
        out_shape=jax.ShapeDtypeStruct((g, rows, kx), F32),
        scratch_shapes=[
            pltpu.VMEM((rows, nz), F32),
            pltpu.VMEM((rows, kb - kx), F32),
        ],
        compiler_params=_params("parallel"),
        name="s5_scan",
    )(x, m_a, m_b, coef)
    y = y.reshape(g, n_chunks, SUBLANES, t, c)[:, :, :b]
    return y.transpose(2, 1, 3, 0, 4).reshape(b, seq, d)


def _merge_kernel(x_ref, hc_ref, y_ref, gate_ref, wglu_ref, bglu_ref, ws_ref, wc_ref,
                  wout_ref, gn_ref, *rest, moe):
    if moe:
        rw_ref, rb_ref, xo_ref, h_ref, route_ref = rest
    else:
        xo_ref, h_ref = rest
    d = x_ref.shape[-1]
    yg = jax.nn.gelu(y_ref[...])
    glu = jnp.dot(yg.astype(BF16), wglu_ref[...], preferred_element_type=F32) + bglu_ref[...]
    y2 = yg * jax.nn.sigmoid(glu)
    br_s = jnp.dot(y2.astype(BF16), ws_ref[...], preferred_element_type=F32)
    br_c = jnp.dot(hc_ref[...], wc_ref[...], preferred_element_type=F32)
    gc = gate_ref[:, 0:d].astype(F32)
    gs = gate_ref[:, d:2 * d].astype(F32)
    merged = jax.nn.sigmoid(gc) * br_c + jax.nn.sigmoid(gs) * br_s
    xn = x_ref[...] + jnp.dot(merged.astype(BF16), wout_ref[...], preferred_element_type=F32)
    xo_ref[...] = xn
    h = _rms(xn, gn_ref[...])
    h_ref[...] = h.astype(h_ref.dtype)
    if moe:
        logits = jnp.dot(h, rw_ref[...], preferred_element_type=F32,
                         precision=lax.Precision.HIGHEST) + rb_ref[1:2, :]
        lane = lax.broadcasted_iota(jnp.int32, logits.shape, 1).astype(F32)
        valid = rb_ref[0:1, :] > 0.0
        neg = jnp.float32(-jnp.inf)
        logits = jnp.where(valid, logits, neg)
        m1 = jnp.max(logits, axis=-1, keepdims=True)
        i1 = jnp.min(jnp.where(logits == m1, lane, float(LANES)), axis=-1, keepdims=True)
        rest_l = jnp.where(lane == i1, neg, logits)
        m2 = jnp.max(rest_l, axis=-1, keepdims=True)
        i2 = jnp.min(jnp.where(rest_l == m2, lane, float(LANES)), axis=-1, keepdims=True)
        e2 = jnp.exp(m2 - m1)
        w1 = 1.0 / (1.0 + e2)
        w2 = e2 / (1.0 + e2)
        route_ref[...] = jnp.where(lane == 0.0, i1, jnp.where(lane == 1.0, i2,
                                   jnp.where(lane == 2.0, w1, jnp.where(lane == 3.0, w2, 0.0))))


def _merge(x2, hc, y, gates, wglu, bglu, ws, wc, wout, gn, router=None):
    n, d = x2.shape
    ds_ = hc.shape[1]
    tm = ROW_TILE
    moe = router is not None
    row = lambda w: pl.BlockSpec((tm, w), lambda i: (i, 0))
    full = lambda a: pl.BlockSpec(a.shape, lambda i: (0,) * a.ndim)
    ins = [x2, hc, y, gates, wglu, bglu.reshape(1, ds_), ws, wc, wout, gn.reshape(1, d)]
    in_specs = [row(d), row(ds_), row(ds_), row(gates.shape[1])] + [full(a) for a in ins[4:]]
    out_specs = [row(d), row(d)]
    out_shape = [jax.ShapeDtypeStruct((n, d), F32),
                 jax.ShapeDtypeStruct((n, d), F32 if moe else BF16)]
    if moe:
        rw, rb = router
        n_exp = rw.shape[1]
        rw_pad = jnp.zeros((d, LANES), F32).at[:, :n_exp].set(rw.astype(F32))
        rb_pad = jnp.zeros((SUBLANES, LANES), F32).at[0, :n_exp].set(1.0).at[1, :n_exp].set(rb.astype(F32))
        ins += [rw_pad, rb_pad]
        in_specs += [full(rw_pad), full(rb_pad)]
        out_specs.append(row(LANES))
        out_shape.append(jax.ShapeDtypeStruct((n, LANES), F32))
    return pl.pallas_call(
        functools.partial(_merge_kernel, moe=moe),
        grid=(n // tm,),
        in_specs=in_specs,
        out_specs=out_specs,
        out_shape=out_shape,
        compiler_params=_params("parallel"),
        name="merge_moe" if moe else "merge",
    )(*ins)


def _swiglu_rows(h_bf, wg_ref, wu_ref, wd_ref, a_scr):
    d_ff = a_scr.shape[1]
    for c0 in range(0, d_ff, FF_CHUNK):
        cs = slice(c0, c0 + FF_CHUNK)
        gt = jnp.dot(h_bf, wg_ref[:, cs], preferred_element_type=F32)
        up = jnp.dot(h_bf, wu_ref[:, cs], preferred_element_type=F32)
        a_scr[:, cs] = (gt * jax.nn.sigmoid(gt) * up).astype(BF16)
    return jnp.dot(a_scr[...], wd_ref[...], preferred_element_type=F32)


def _ffn_kernel(x_ref, h_ref, wg_ref, wu_ref, wd_ref, gf_ref, o_ref, a_scr, *, final_norm):
    out = x_ref[...] + _swiglu_rows(h_ref[...], wg_ref, wu_ref, wd_ref, a_scr)
    if final_norm:
        out = _rms(out, gf_ref[...])
    o_ref[...] = out


def _ffn_dense(x2, h, wg, wu, wd, gf, final_norm):
    n, d = x2.shape
    d_ff = wg.shape[1]
    tm = ROW_TILE
    once = pl.Buffered(1)
    return pl.pallas_call(
        functools.partial(_ffn_kernel, final_norm=final_norm),
        grid=(n // tm,),
        in_specs=[
            pl.BlockSpec((tm, d), lambda i: (i, 0)),
            pl.BlockSpec((tm, d), lambda i: (i, 0)),
            pl.BlockSpec((d, d_ff), lambda i: (0, 0), pipeline_mode=once),
            pl.BlockSpec((d, d_ff), lambda i: (0, 0), pipeline_mode=once),
            pl.BlockSpec((d_ff, d), lambda i: (0, 0), pipeline_mode=once),
            pl.BlockSpec((1, d), lambda i: (0, 0)),
        ],
        out_specs=pl.BlockSpec((tm, d), lambda i: (i, 0)),
        out_shape=jax.ShapeDtypeStruct((n, d), F32),
        scratch_shapes=[pltpu.VMEM((tm, d_ff), BF16)],
        compiler_params=_params("parallel"),
        name="ffn_dense",
    )(x2, h, wg, wu, wd, gf.reshape(1, d))


def _row_copy(src, dst, sem, s, t):
    return pltpu.make_async_copy(src.at[pl.ds(s, 1), :], dst.at[pl.ds(t, 1), :], sem)


def _dispatch_kernel(zb_ref, nu_ref, p0_ref, p1_ref, h_ref, xs_ref, zero_scr, sem):
    tile = h_ref.shape[0]
    n_exp = zb_ref.shape[0]

    @pl.when(pl.program_id(0) == 0)
    def _():
        zero_scr[...] = jnp.zeros(zero_scr.shape, F32)
        rows = zero_scr.shape[0]

        def fill(blk):
            start = pl.multiple_of(blk * rows, rows)
            return pltpu.make_async_copy(zero_scr, xs_ref.at[pl.ds(start, rows), :], sem)

        for e in range(n_exp):
            fill(zb_ref[e]).start()
        for e in range(n_exp):
            fill(zb_ref[e]).wait()

        def fill_unused(blk, carry):
            fill(blk).start()
            fill(blk).wait()
            return carry

        lax.fori_loop(nu_ref[0], xs_ref.shape[0] // rows, fill_unused, 0)

    def issue(r, carry):
        _row_copy(h_ref, xs_ref, sem, r, p0_ref[0, 0, r]).start()
        _row_copy(h_ref, xs_ref, sem, r, p1_ref[0, 0, r]).start()
        return carry

    def drain(r, carry):
        _row_copy(h_ref, xs_ref, sem, r, p0_ref[0, 0, r]).wait()
        _row_copy(h_ref, xs_ref, sem, r, p1_ref[0, 0, r]).wait()
        return carry

    lax.fori_loop(0, tile, issue, 0)
    lax.fori_loop(0, tile, drain, 0)


def _dispatch(h, pos0, pos1, zero_blocks, n_used, n_sorted):
    n, d = h.shape
    tp = PERM_TILE
    smem_rows = lambda: pl.BlockSpec((1, 1, tp), lambda i, zb, nu: (i, 0, 0),
                                     memory_space=pltpu.SMEM)
    grid_spec = pltpu.PrefetchScalarGridSpec(
        num_scalar_prefetch=2,
        grid=(n // tp,),
        in_specs=[smem_rows(), smem_rows(),
                  pl.BlockSpec((tp, d), lambda i, zb, nu: (i, 0))],
        out_specs=pl.BlockSpec(memory_space=pl.ANY),
        scratch_shapes=[pltpu.VMEM((MOE_TILE, d), F32), pltpu.SemaphoreType.DMA(())],
    )
    return pl.pallas_call(
        _dispatch_kernel,
        grid_spec=grid_spec,
        out_shape=jax.ShapeDtypeStruct((n_sorted, d), F32),
        compiler_params=_params("arbitrary"),
        name="moe_dispatch",
    )(zero_blocks, n_used, pos0.reshape(n // tp, 1, tp), pos1.reshape(n // tp, 1, tp), h)


def _expert_kernel(be_ref, nu_ref, xs_ref, wg_ref, wu_ref, wd_ref, ys_ref, a_scr):
    @pl.when(pl.program_id(0) < nu_ref[0])
    def _():
        ys_ref[...] = _swiglu_rows(xs_ref[...].astype(BF16), wg_ref.at[0], wu_ref.at[0],
                                   wd_ref.at[0], a_scr)

    @pl.when(pl.program_id(0) >= nu_ref[0])
    def _():
        ys_ref[...] = jnp.zeros(ys_ref.shape, F32)


def _experts(xs, wg, wu, wd, block_expert, n_used):
    n_sorted, d = xs.shape
    d_ff = wg.shape[2]
    tm = MOE_TILE
    grid_spec = pltpu.PrefetchScalarGridSpec(
        num_scalar_prefetch=2,
        grid=(n_sorted // tm,),
        in_specs=[
            pl.BlockSpec((tm, d), lambda i, be, nu: (jnp.minimum(i, nu[0] - 1), 0)),
            pl.BlockSpec((1, d, d_ff), lambda i, be, nu: (be[i], 0, 0)),
            pl.BlockSpec((1, d, d_ff), lambda i, be, nu: (be[i], 0, 0)),
            pl.BlockSpec((1, d_ff, d), lambda i, be, nu: (be[i], 0, 0)),
        ],
        out_specs=pl.BlockSpec((tm, d), lambda i, be, nu: (i, 0)),
        scratch_shapes=[pltpu.VMEM((tm, d_ff), BF16)],
    )
    return pl.pallas_call(
        _expert_kernel,
        grid_spec=grid_spec,
        out_shape=jax.ShapeDtypeStruct((n_sorted, d), F32),
        compiler_params=_params("arbitrary"),
        name="moe_experts",
    )(block_expert, n_used, xs, wg, wu, wd)


def _combine_kernel(p0_ref, p1_ref, x_ref, route_ref, gf_ref, ys_ref, o_ref, g_scr, sem,
                    *, final_norm):
    tile = x_ref.shape[0]

    def issue(r, carry):
        _row_copy(ys_ref, g_scr.at[0], sem, p0_ref[0, 0, r], r).start()
        _row_copy(ys_ref, g_scr.at[1], sem, p1_ref[0, 0, r], r).start()
        return carry

    def drain(r, carry):
        _row_copy(ys_ref, g_scr.at[0], sem, p0_ref[0, 0, r], r).wait()
        _row_copy(ys_ref, g_scr.at[1], sem, p1_ref[0, 0, r], r).wait()
        return carry

    lax.fori_loop(0, tile, issue, 0)
    lax.fori_loop(0, tile, drain, 0)
    w1 = route_ref[:, 2:3]
    w2 = route_ref[:, 3:4]
    out = x_ref[...] + w1 * g_scr[0] + w2 * g_scr[1]
    if final_norm:
        out = _rms(out, gf_ref[...])
    o_ref[...] = out


def _combine(x2, route, ys, pos0, pos1, gf, final_norm):
    n, d = x2.shape
    tp = PERM_TILE
    smem_rows = lambda: pl.BlockSpec((1, 1, tp), lambda i: (i, 0, 0), memory_space=pltpu.SMEM)
    return pl.pallas_call(
        functools.partial(_combine_kernel, final_norm=final_norm),
        grid=(n // tp,),
        in_specs=[smem_rows(), smem_rows(),
                  pl.BlockSpec((tp, d), lambda i: (i, 0)),
                  pl.BlockSpec((tp, LANES), lambda i: (i, 0)),
                  pl.BlockSpec((1, d), lambda i: (0, 0)),
                  pl.BlockSpec(memory_space=pl.ANY)],
        out_specs=pl.BlockSpec((tp, d), lambda i: (i, 0)),
        out_shape=jax.ShapeDtypeStruct((n, d), F32),
        scratch_shapes=[pltpu.VMEM((2, tp, d), F32), pltpu.SemaphoreType.DMA(())],
        compiler_params=_params("arbitrary"),
        name="moe_combine",
    )(pos0.reshape(n // tp, 1, tp), pos1.reshape(n // tp, 1, tp), x2, route,
      gf.reshape(1, d), ys)


def _routing_tables(route, n_exp):
    n = route.shape[0]
    tm = MOE_TILE
    i1 = route[:, 0].astype(jnp.int32)
    i2 = route[:, 1].astype(jnp.int32)
    oh1 = jax.nn.one_hot(i1, n_exp, dtype=jnp.int32)
    oh2 = jax.nn.one_hot(i2, n_exp, dtype=jnp.int32)
    tot = oh1 + oh2
    csum = jnp.cumsum(tot, axis=0)
    excl = csum - tot
    rank1 = jnp.sum(excl * oh1, axis=-1)
    rank2 = jnp.sum((excl + oh1) * oh2, axis=-1)
    counts = csum[-1]
    nblk = (counts + tm - 1) // tm
    blk_end = jnp.cumsum(nblk)
    blk_off = blk_end - nblk
    row_off = blk_off * tm
    pos0 = row_off[i1] + rank1
    pos1 = row_off[i2] + rank2
    n_blocks = TOP_K * n // tm + n_exp
    n_used = blk_end[-1]
    jb = jnp.minimum(jnp.arange(n_blocks, dtype=jnp.int32), n_used - 1)
    block_expert = jnp.sum((blk_end[None, :] <= jb[:, None]).astype(jnp.int32), axis=-1)
    block_expert = jnp.minimum(block_expert, n_exp - 1)
    zero_blocks = jnp.where(nblk > 0, blk_end - 1, n_blocks - 1).astype(jnp.int32)
    return (pos0.astype(jnp.int32), pos1.astype(jnp.int32), block_expert,
            n_used.astype(jnp.int32).reshape(1), zero_blocks, n_blocks * tm)


def _moe(x2, h, route, wg, wu, wd, gf, final_norm):
    n_exp = wg.shape[0]
    pos0, pos1, block_expert, n_used, zero_blocks, n_sorted = _routing_tables(route, n_exp)
    xs = _dispatch(h, pos0, pos1, zero_blocks, n_used, n_sorted)
    ys = _experts(xs, wg, wu, wd, block_expert, n_used)
    return _combine(x2, route, ys, pos0, pos1, gf, final_norm)


def kernel(x, norm_mix_g, w_in, conv_dw_w, conv_dw_b, conv_ln_g, conv_ln_b, w_conv_proj,
           ssm_a_re, ssm_a_im, ssm_log_dt, ssm_b_re, ssm_b_im, ssm_c_re, ssm_c_im, ssm_d,
           ssm_w_glu, ssm_b_glu, w_ssm_proj, w_out, norm_ffn_g,
           ffn_w_gate, ffn_w_up, ffn_w_down, router_w, router_b,
           moe_w_gate, moe_w_up, moe_w_down, final_norm_g):
    bsz, seq, d = x.shape
    depth = w_in.shape[0]
    d_conv = conv_dw_w.shape[-1]
    d_ssm = ssm_d.shape[-1]
    n = bsz * seq
    bf = lambda a: a.astype(BF16)
    x2 = x.reshape(n, d).astype(F32)
    for layer in range(depth):
        last = layer == depth - 1
        ch, u, gates = _inproj(x2, norm_mix_g[layer], bf(w_in[layer]), d_conv, d_ssm)
        hc = _conv_branch(ch.reshape(bsz, seq, d_conv), conv_dw_w[layer], conv_dw_b[layer],
                          conv_ln_g[layer], conv_ln_b[layer]).reshape(n, d_conv)
        m_a, m_b, coef = _s5_matrices(ssm_a_re[layer], ssm_a_im[layer], ssm_log_dt[layer],
                                      ssm_b_re[layer], ssm_b_im[layer], ssm_c_re[layer],
                                      ssm_c_im[layer], ssm_d[layer])
        y = _s5_branch(u.reshape(bsz, seq, d_ssm), m_a, m_b, coef).reshape(n, d_ssm)
        i = layer // 2
        moe = layer % 2 == 1
        router = (router_w[i], router_b[i]) if moe else None
        outs = _merge(x2, hc, y, gates, bf(ssm_w_glu[layer]), ssm_b_glu[layer],
                      bf(w_ssm_proj[layer]), bf(w_conv_proj[layer]), bf(w_out[layer]),
                      norm_ffn_g[layer], router)
        if moe:
            x2, h, route = outs
            x2 = _moe(x2, h, route, bf(moe_w_gate[i]), bf(moe_w_up[i]), bf(moe_w_down[i]),
                      final_norm_g, last)
        else:
            x2, h = outs
            x2 = _ffn_dense(x2, h, bf(ffn_w_gate[i]), bf(ffn_w_up[i]), bf(ffn_w_down[i]),
                            final_norm_g, last)
    return x2.reshape(bsz, seq, d).astype(x.dtype)
```

```python
import functools
import math

import jax
import jax.numpy as jnp
from jax import lax
from jax.experimental import pallas as pl
from jax.experimental.pallas import tpu as pltpu

F32 = jnp.float32
BF16 = jnp.bfloat16

RMS_EPS = 1e-6
LN_EPS = 1e-5
CONV_WIDTH = 31
CONV_HALF = CONV_WIDTH // 2
SSM_GROUP = 16
SSM_STATE = 64
CHUNK = 16
TOP_K = 2
LANES = 128
SUBLANES = 8
VMEM_LIMIT = 56 * 1024 * 1024

ROW_TILE = 512
MOE_TILE = 256
PERM_TILE = 256
CONV_TILE = 128
FF_CHUNK = 256


def _params(*sem):
    return pltpu.CompilerParams(dimension_semantics=sem, vmem_limit_bytes=VMEM_LIMIT)


def _rms(xf, g):
    ms = jnp.mean(xf * xf, axis=-1, keepdims=True)
    return xf * lax.rsqrt(ms + RMS_EPS) * g


def _inproj_kernel(x_ref, g_ref, w_ref, ch_ref, u_ref, gate_ref):
    dc = ch_ref.shape[-1]
    ds_ = u_ref.shape[0] * LANES
    h = _rms(x_ref[...], g_ref[...]).astype(BF16)
    v = jnp.dot(h, w_ref[:, 0:dc], preferred_element_type=F32)
    gt = jnp.dot(h, w_ref[:, dc:2 * dc], preferred_element_type=F32)
    ch_ref[...] = v * jax.nn.sigmoid(gt)
    u = jnp.dot(h, w_ref[:, 2 * dc:2 * dc + ds_], preferred_element_type=F32)
    for q in range(u_ref.shape[0]):
        u_ref[q, 0] = u[:, q * LANES:(q + 1) * LANES]
    gate_ref[...] = jnp.dot(h, w_ref[:, 2 * dc + ds_:], preferred_element_type=F32).astype(BF16)


def _inproj(x2, g, w_bf, bsz, d_conv, d_ssm):
    n, d = x2.shape
    seq = n // bsz
    d_in = w_bf.shape[1]
    d_gate = d_in - 2 * d_conv - d_ssm
    tm = ROW_TILE
    per_seq = seq // tm
    nq = d_ssm // LANES
    return pl.pallas_call(
        _inproj_kernel,
        grid=(n // tm,),
        in_specs=[
            pl.BlockSpec((tm, d), lambda i: (i, 0)),
            pl.BlockSpec((1, d), lambda i: (0, 0)),
            pl.BlockSpec((d, d_in), lambda i: (0, 0)),
        ],
        out_specs=[
            pl.BlockSpec((tm, d_conv), lambda i: (i, 0)),
            pl.BlockSpec((nq, 1, tm, LANES), lambda i: (0, i // per_seq, i % per_seq, 0)),
            pl.BlockSpec((tm, d_gate), lambda i: (i, 0)),
        ],
        out_shape=[
            jax.ShapeDtypeStruct((n, d_conv), F32),
            jax.ShapeDtypeStruct((nq, bsz, seq, LANES), F32),
            jax.ShapeDtypeStruct((n, d_gate), BF16),
        ],
        compiler_params=_params("parallel"),
        name="inproj",
    )(x2, g.reshape(1, d), w_bf)


def _conv_kernel(h_ref, w_ref, b_ref, lg_ref, lb_ref, o_ref, pad_scr, tile_scr):
    seq, c = h_ref.shape[1], h_ref.shape[2]
    front = 2 * SUBLANES
    zeros = jnp.zeros((front, c), F32)
    pad_scr[0:front, :] = zeros
    pad_scr[front + seq:front + seq + front, :] = zeros
    pad_scr[front:front + seq, :] = h_ref[0]

    def tile_body(ti, carry):
        base = pl.multiple_of(ti * CONV_TILE, CONV_TILE)
        win_ref = pad_scr.at[pl.ds(base, CONV_TILE + 2 * front), :]
        for lb in range(c // LANES):
            ls = slice(lb * LANES, (lb + 1) * LANES)
            acc = jnp.zeros((CONV_TILE, LANES), F32)
            for k in range(CONV_WIDTH):
                rows = pl.ds(k - CONV_HALF + front, CONV_TILE)
                acc = acc + w_ref[k:k + 1, ls] * win_ref[rows, ls]
            tile_scr[:, ls] = acc + b_ref[:, ls]
        hh = tile_scr[...]
        mu = jnp.mean(hh, axis=-1, keepdims=True)
        cen = hh - mu
        var = jnp.mean(cen * cen, axis=-1, keepdims=True)
        y = cen * lax.rsqrt(var + LN_EPS) * lg_ref[...] + lb_ref[...]
        o_ref[0, pl.ds(base, CONV_TILE), :] = (y * jax.nn.sigmoid(y)).astype(BF16)
        return carry

    lax.fori_loop(0, seq // CONV_TILE, tile_body, 0)


def _conv_branch(ch, dw_w, dw_b, ln_g, ln_b):
    b, seq, c = ch.shape
    front = 2 * SUBLANES
    vec = lambda a: a.reshape(1, c)
    return pl.pallas_call(
        _conv_kernel,
        grid=(b,),
        in_specs=[
            pl.BlockSpec((1, seq, c), lambda i: (i, 0, 0)),
            pl.BlockSpec((CONV_WIDTH, c), lambda i: (0, 0)),
            pl.BlockSpec((1, c), lambda i: (0, 0)),
            pl.BlockSpec((1, c), lambda i: (0, 0)),
            pl.BlockSpec((1, c), lambda i: (0, 0)),
        ],
        out_specs=pl.BlockSpec((1, seq, c), lambda i: (i, 0, 0)),
        out_shape=jax.ShapeDtypeStruct((b, seq, c), BF16),
        scratch_shapes=[
            pltpu.VMEM((seq + 2 * front, c), F32),
            pltpu.VMEM((CONV_TILE, c), F32),
        ],
        compiler_params=_params("parallel"),
        name="conv_branch",
    )(ch, dw_w, vec(dw_b), vec(ln_g), vec(ln_b))


def _s5_matrices(a_re, a_im, log_dt, b_re, b_im, c_re, c_im, d_skip):
    t = CHUNK
    hi = lax.Precision.HIGHEST
    n_dir, g, p = a_re.shape
    c = b_re.shape[-1]
    a_re, a_im, log_dt = a_re.astype(F32), a_im.astype(F32), log_dt.astype(F32)
    b_re, b_im = b_re.astype(F32), b_im.astype(F32)
    c_re, c_im = c_re.astype(F32), c_im.astype(F32)
    dt = jnp.exp(log_dt)[:, :, None]
    lam_re, lam_im = a_re * dt, a_im * dt
    ks = jnp.arange(t + 1, dtype=F32)[:, None, None, None]
    mag = jnp.exp(ks * lam_re)
    pw_re, pw_im = mag * jnp.cos(ks * lam_im), mag * jnp.sin(ks * lam_im)
    ab_re, ab_im = pw_re[1], pw_im[1]
    n_re, n_im = ab_re - 1.0, ab_im
    den = a_re * a_re + a_im * a_im
    q_re = ((n_re * a_re + n_im * a_im) / den)[..., None]
    q_im = ((n_im * a_re - n_re * a_im) / den)[..., None]
    bb_re = q_re * b_re - q_im * b_im
    bb_im = q_re * b_im + q_im * b_re
    wb_re = pw_re[..., None] * bb_re - pw_im[..., None] * bb_im
    wb_im = pw_re[..., None] * bb_im + pw_im[..., None] * bb_re
    pc_re = pw_re[:, :, :, None, :]
    pc_im = pw_im[:, :, :, None, :]
    vc_re = c_re * pc_re - c_im * pc_im
    vc_im = c_re * pc_im + c_im * pc_re
    kk = (jnp.einsum("dgop,kdgpc->kdgoc", c_re, wb_re[:t], precision=hi)
          - jnp.einsum("dgop,kdgpc->kdgoc", c_im, wb_im[:t], precision=hi))
    kf, kb = kk[:, 0], kk[:, 1]
    jj = jnp.arange(t)[:, None]
    ii = jnp.arange(t)[None, :]
    lag = ii - jj
    lag_abs = jnp.abs(lag)
    diag = kf[0] + kb[0] + d_skip.astype(F32).reshape(g, c)[:, :, None] * jnp.eye(c, dtype=F32)
    sel = lag[:, :, None, None, None]
    m_intra = jnp.where(sel > 0, kf[lag_abs], jnp.where(sel < 0, kb[lag_abs], diag[None, None]))
    m_intra = m_intra.transpose(2, 0, 4, 1, 3).reshape(g, t * c, t * c)

    def zmat(w, order):
        return w[order].transpose(1, 0, 3, 2).reshape(g, t * c, p)

    of = jnp.arange(t - 1, -1, -1)
    ob = jnp.arange(t)
    zf_re, zf_im = zmat(wb_re[:, 0], of), zmat(wb_im[:, 0], of)
    zb_re, zb_im = zmat(wb_re[:, 1], ob), zmat(wb_im[:, 1], ob)
    m_a = jnp.concatenate([zf_re, zf_im, zf_im, zf_re, zb_re, zb_im, zb_im, zb_re], axis=-1)

    def omat(v, order):
        return v[order].transpose(1, 3, 0, 2).reshape(g, p, t * c)

    pf = jnp.arange(1, t + 1)
    pb = jnp.arange(t, 0, -1)
    m_b = jnp.concatenate([
        m_intra,
        omat(vc_re[:, 0], pf), -omat(vc_im[:, 0], pf),
        omat(vc_re[:, 1], pb), -omat(vc_im[:, 1], pb)], axis=1)
    at_re, at_im = pw_re[t], pw_im[t]
    rows = [jnp.concatenate([at_re[0], at_re[0]], -1), jnp.concatenate([-at_im[0], at_im[0]], -1),
            jnp.concatenate([at_re[1], at_re[1]], -1), jnp.concatenate([-at_im[1], at_im[1]], -1)]
    coef = jnp.stack(rows + [jnp.zeros_like(rows[0])] * (SUBLANES - len(rows)), axis=1)
    return m_a.astype(BF16), m_b.astype(BF16), coef


def _block_transpose(v):
    nblk = len(v)
    blk = lax.broadcasted_iota(jnp.int32, v[0].shape, 1) // SSM_GROUP
    d = nblk // 2
    while d >= 1:
        keep_lo = (blk & d) == 0
        out = list(v)
        for r in range(nblk):
            if r & d:
                continue
            lo, hi = v[r], v[r + d]
            out[r] = jnp.where(keep_lo, lo, pltpu.roll(hi, d * SSM_GROUP, 1))
            out[r + d] = jnp.where(keep_lo, pltpu.roll(lo, LANES - d * SSM_GROUP, 1), hi)
        v = out
        d //= 2
    return v


def _s5_kernel(u_ref, ma_ref, mb_ref, coef_ref, y_ref, x_scr, z_scr, cin_scr):
    nb, seq = u_ref.shape[1], u_ref.shape[2]
    gpl, halves, rows = x_scr.shape[0:3]
    kx = halves * LANES
    sw = coef_ref.shape[-1]
    n_chunks = seq // CHUNK
    cpt = SUBLANES // nb
    n_tiles = rows // SUBLANES
    span = SUBLANES * CHUNK
    pos_per_half = LANES // SSM_GROUP

    def relayout(co, to_chunks):
        tok0 = pl.multiple_of(co * span, span)
        row0 = pl.multiple_of(co * SUBLANES * nb, SUBLANES * nb)
        for b in range(nb):
            tok_ref = (u_ref if to_chunks else y_ref).at[0, b, pl.ds(tok0, span), :]
            for h in range(halves):
                chunk_refs = [x_scr.at[gp, h, pl.ds(row0, SUBLANES * nb), :] for gp in range(gpl)]
                tok_rows = [pl.ds(h * pos_per_half + ip, SUBLANES, stride=CHUNK)
                            for ip in range(pos_per_half)]
                chunk_rows = pl.ds(b, SUBLANES, stride=nb)
                if to_chunks:
                    dst = _block_transpose([tok_ref[r, :] for r in tok_rows])
                    for gp in range(gpl):
                        chunk_refs[gp][chunk_rows, :] = dst[gp]
                else:
                    dst = _block_transpose([chunk_refs[gp][chunk_rows, :] for gp in range(gpl)])
                    for ip in range(pos_per_half):
                        tok_ref[tok_rows[ip], :] = dst[ip]

    def relayout_in(co, carry):
        relayout(co, True)
        return carry

    def relayout_out(co, carry):
        relayout(co, False)
        return carry

    lax.fori_loop(0, n_chunks // SUBLANES, relayout_in, 0)

    rc = min(rows, 512)
    row_id = lax.broadcasted_iota(jnp.int32, (SUBLANES, sw), 0)
    x_rows = lambda gp, r0: jnp.concatenate(
        [x_scr[gp, h, r0:r0 + rc, :] for h in range(halves)], axis=-1).astype(BF16)
    for gp in range(gpl):
        for r0 in range(0, rows, rc):
            z_scr[r0:r0 + rc, :] = jnp.dot(x_rows(gp, r0), ma_ref[gp], preferred_element_type=F32)
        coef = coef_ref[gp]
        bc = lambda r: jnp.broadcast_to(coef[r:r + 1, :], (SUBLANES, sw))
        af1, af2, ab1, ab2 = bc(0), bc(1), bc(2), bc(3)

        def tile_step(ti, carry):
            sf, sfp, sb, sbp = carry
            rf = pl.multiple_of(ti * SUBLANES, SUBLANES)
            rb = pl.multiple_of((n_tiles - 1 - ti) * SUBLANES, SUBLANES)
            zf = z_scr[pl.ds(rf, SUBLANES), 0:sw]
            zfp = z_scr[pl.ds(rf, SUBLANES), sw:2 * sw]
            zb = z_scr[pl.ds(rb, SUBLANES), 2 * sw:3 * sw]
            zbp = z_scr[pl.ds(rb, SUBLANES), 3 * sw:4 * sw]
            cin_f, cin_b = sf, sb
            for k in range(cpt):
                nf, nfp = af1 * sf + af2 * sfp + zf, af1 * sfp - af2 * sf + zfp
                sf, sfp = pltpu.roll(nf, nb, 0), pltpu.roll(nfp, nb, 0)
                kb = cpt - 1 - k
                nbk, nbkp = ab1 * sb + ab2 * sbp + zb, ab1 * sbp - ab2 * sb + zbp
                sb, sbp = pltpu.roll(nbk, SUBLANES - nb, 0), pltpu.roll(nbkp, SUBLANES - nb, 0)
                if k < cpt - 1:
                    cin_f = jnp.where(row_id >= (k + 1) * nb, sf, cin_f)
                    cin_b = jnp.where(row_id < kb * nb, sb, cin_b)
            cin_scr[pl.ds(rf, SUBLANES), 0:sw] = cin_f
            cin_scr[pl.ds(rb, SUBLANES), sw:2 * sw] = cin_b
            return sf, sfp, sb, sbp

        zero = jnp.zeros((SUBLANES, sw), F32)
        lax.fori_loop(0, n_tiles, tile_step, (zero, zero, zero, zero), unroll=2)

        for r0 in range(0, rows, rc):
            y = (jnp.dot(x_rows(gp, r0), mb_ref[gp, 0:kx, :], preferred_element_type=F32)
                 + jnp.dot(cin_scr[r0:r0 + rc, :].astype(BF16), mb_ref[gp, kx:, :],
                           preferred_element_type=F32))
            for h in range(halves):
                x_scr[gp, h, r0:r0 + rc, :] = y[:, h * LANES:(h + 1) * LANES]

    lax.fori_loop(0, n_chunks // SUBLANES, relayout_out, 0)


def _s5_branch(u, m_a, m_b, coef):
    nq, b, seq, _ = u.shape
    g = m_a.shape[0]
    gpl = g // nq
    n_chunks = seq // CHUNK
    assert SUBLANES % b == 0 and n_chunks % SUBLANES == 0 and gpl * SSM_GROUP == LANES
    rows = n_chunks * b
    kx, nz, kb = m_a.shape[1], m_a.shape[2], m_b.shape[1]
    return pl.pallas_call(
        _s5_kernel,
        grid=(nq,),
        in_specs=[
            pl.BlockSpec((1, b, seq, LANES), lambda i: (i, 0, 0, 0), pipeline_mode=pl.Buffered(1)),
            pl.BlockSpec((gpl, kx, nz), lambda i: (i, 0, 0)),
            pl.BlockSpec((gpl, kb, kx), lambda i: (i, 0, 0)),
            pl.BlockSpec((gpl, SUBLANES, coef.shape[-1]), lambda i: (i, 0, 0)),
        ],
        out_specs=pl.BlockSpec((1, b, seq, LANES), lambda i: (i, 0, 0, 0)),
        out_shape=jax.ShapeDtypeStruct((nq, b, seq, LANES), F32),
        scratch_shapes=[
            pltpu.VMEM((gpl, kx // LANES, rows, LANES), F32),
            pltpu.VMEM((rows, nz), F32),
            pltpu.VMEM((rows, kb - kx), F32),
        ],
        compiler_params=_params("parallel"),
        name="s5_scan",
    )(u, m_a, m_b, coef)


def _merge_kernel(x_ref, hc_ref, y_ref, gate_ref, wglu_ref, bglu_ref, ws_ref, wc_ref,
                  wout_ref, gn_ref, *rest, moe):
    if moe:
        rw_ref, rb_ref, xo_ref, h_ref, route_ref = rest
    else:
        xo_ref, h_ref = rest
    d = x_ref.shape[-1]
    yg = jax.nn.gelu(jnp.concatenate([y_ref[q, 0] for q in range(y_ref.shape[0])], axis=-1))
    glu = jnp.dot(yg.astype(BF16), wglu_ref[...], preferred_element_type=F32) + bglu_ref[...]
    y2 = yg * jax.nn.sigmoid(glu)
    br_s = jnp.dot(y2.astype(BF16), ws_ref[...], preferred_element_type=F32)
    br_c = jnp.dot(hc_ref[...], wc_ref[...], preferred_element_type=F32)
    gc = gate_ref[:, 0:d].astype(F32)
    gs = gate_ref[:, d:2 * d].astype(F32)
    merged = jax.nn.sigmoid(gc) * br_c + jax.nn.sigmoid(gs) * br_s
    xn = x_ref[...] + jnp.dot(merged.astype(BF16), wout_ref[...], preferred_element_type=F32)
    xo_ref[...] = xn
    h = _rms(xn, gn_ref[...])
    h_ref[...] = h.astype(h_ref.dtype)
    if moe:
        h_hi = h.astype(BF16)
        h_lo = (h - h_hi.astype(F32)).astype(BF16)
        logits = (jnp.dot(h_hi, rw_ref[0], preferred_element_type=F32)
                  + jnp.dot(h_hi, rw_ref[1], preferred_element_type=F32)
                  + jnp.dot(h_lo, rw_ref[0], preferred_element_type=F32)) + rb_ref[1:2, :]
        lane = lax.broadcasted_iota(jnp.int32, logits.shape, 1).astype(F32)
        valid = rb_ref[0:1, :] > 0.0
        neg = jnp.float32(-jnp.inf)
        logits = jnp.where(valid, logits, neg)
        m1 = jnp.max(logits, axis=-1, keepdims=True)
        i1 = jnp.min(jnp.where(logits == m1, lane, float(LANES)), axis=-1, keepdims=True)
        rest_l = jnp.where(lane == i1, neg, logits)
        m2 = jnp.max(rest_l, axis=-1, keepdims=True)
        i2 = jnp.min(jnp.where(rest_l == m2, lane, float(LANES)), axis=-1, keepdims=True)
        e2 = jnp.exp(m2 - m1)
        w1 = 1.0 / (1.0 + e2)
        w2 = e2 / (1.0 + e2)
        route_ref[...] = jnp.where(lane == 0.0, i1, jnp.where(lane == 1.0, i2,
                                   jnp.where(lane == 2.0, w1, jnp.where(lane == 3.0, w2, 0.0))))


def _merge(x2, hc, y, gates, wglu, bglu, ws, wc, wout, gn, router=None):
    n, d = x2.shape
    ds_ = hc.shape[1]
    nq, _, seq, _ = y.shape
    tm = ROW_TILE
    per_seq = seq // tm
    moe = router is not None
    row = lambda w: pl.BlockSpec((tm, w), lambda i: (i, 0))
    full = lambda a: pl.BlockSpec(a.shape, lambda i: (0,) * a.ndim)
    y_spec = pl.BlockSpec((nq, 1, tm, LANES), lambda i: (0, i // per_seq, i % per_seq, 0))
    ins = [x2, hc, y, gates, wglu, bglu.reshape(1, ds_), ws, wc, wout, gn.reshape(1, d)]
    in_specs = [row(d), row(ds_), y_spec, row(gates.shape[1])] + [full(a) for a in ins[4:]]
    out_specs = [row(d), row(d)]
    out_shape = [jax.ShapeDtypeStruct((n, d), F32),
                 jax.ShapeDtypeStruct((n, d), F32 if moe else BF16)]
    if moe:
        rw, rb = router
        n_exp = rw.shape[1]
        rw_f = jnp.zeros((d, LANES), F32).at[:, :n_exp].set(rw.astype(F32))
        rw_hi = rw_f.astype(BF16)
        rw_pad = jnp.stack([rw_hi, (rw_f - rw_hi.astype(F32)).astype(BF16)])
        rb_pad = jnp.zeros((SUBLANES, LANES), F32).at[0, :n_exp].set(1.0).at[1, :n_exp].set(rb.astype(F32))
        ins += [rw_pad, rb_pad]
        in_specs += [full(rw_pad), full(rb_pad)]
        out_specs.append(row(LANES))
        out_shape.append(jax.ShapeDtypeStruct((n, LANES), F32))
    return pl.pallas_call(
        functools.partial(_merge_kernel, moe=moe),
        grid=(n // tm,),
        in_specs=in_specs,
        out_specs=out_specs,
        out_shape=out_shape,
        compiler_params=_params("parallel"),
        name="merge_moe" if moe else "merge",
    )(*ins)


def _swiglu_rows(h_bf, wg_ref, wu_ref, wd_ref, a_scr):
    d_ff = a_scr.shape[1]
    for c0 in range(0, d_ff, FF_CHUNK):
        cs = slice(c0, c0 + FF_CHUNK)
        gt = jnp.dot(h_bf, wg_ref[:, cs], preferred_element_type=F32)
        up = jnp.dot(h_bf, wu_ref[:, cs], preferred_element_type=F32)
        a_scr[:, cs] = (gt * jax.nn.sigmoid(gt) * up).astype(BF16)
    return jnp.dot(a_scr[...], wd_ref[...], preferred_element_type=F32)


def _ffn_kernel(x_ref, h_ref, wg_ref, wu_ref, wd_ref, gf_ref, o_ref, a_scr, *, final_norm):
    out = x_ref[...] + _swiglu_rows(h_ref[...], wg_ref, wu_ref, wd_ref, a_scr)
    if final_norm:
        out = _rms(out, gf_ref[...])
    o_ref[...] = out


def _ffn_dense(x2, h, wg, wu, wd, gf, final_norm):
    n, d = x2.shape
    d_ff = wg.shape[1]
    tm = ROW_TILE
    once = pl.Buffered(1)
    return pl.pallas_call(
        functools.partial(_ffn_kernel, final_norm=final_norm),
        grid=(n // tm,),
        in_specs=[
            pl.BlockSpec((tm, d), lambda i: (i, 0)),
            pl.BlockSpec((tm, d), lambda i: (i, 0)),
            pl.BlockSpec((d, d_ff), lambda i: (0, 0), pipeline_mode=once),
            pl.BlockSpec((d, d_ff), lambda i: (0, 0), pipeline_mode=once),
            pl.BlockSpec((d_ff, d), lambda i: (0, 0), pipeline_mode=once),
            pl.BlockSpec((1, d), lambda i: (0, 0)),
        ],
        out_specs=pl.BlockSpec((tm, d), lambda i: (i, 0)),
        out_shape=jax.ShapeDtypeStruct((n, d), F32),
        scratch_shapes=[pltpu.VMEM((tm, d_ff), BF16)],
        compiler_params=_params("parallel"),
        name="ffn_dense",
    )(x2, h, wg, wu, wd, gf.reshape(1, d))


def _row_copy(src, dst, sem, s, t):
    return pltpu.make_async_copy(src.at[pl.ds(s, 1), :], dst.at[pl.ds(t, 1), :], sem)


def _dispatch_kernel(zb_ref, nu_ref, p0_ref, p1_ref, h_ref, xs_ref, zero_scr, sem):
    tile = h_ref.shape[0]
    n_exp = zb_ref.shape[0]

    @pl.when(pl.program_id(0) == 0)
    def _():
        zero_scr[...] = jnp.zeros(zero_scr.shape, F32)
        rows = zero_scr.shape[0]

        def fill(blk):
            start = pl.multiple_of(blk * rows, rows)
            return pltpu.make_async_copy(zero_scr, xs_ref.at[pl.ds(start, rows), :], sem)

        for e in range(n_exp):
            fill(zb_ref[e]).start()
        for e in range(n_exp):
            fill(zb_ref[e]).wait()

        def fill_unused(blk, carry):
            fill(blk).start()
            fill(blk).wait()
            return carry

        lax.fori_loop(nu_ref[0], xs_ref.shape[0] // rows, fill_unused, 0)

    def issue(r, carry):
        _row_copy(h_ref, xs_ref, sem, r, p0_ref[0, 0, r]).start()
        _row_copy(h_ref, xs_ref, sem, r, p1_ref[0, 0, r]).start()
        return carry

    def drain(r, carry):
        _row_copy(h_ref, xs_ref, sem, r, p0_ref[0, 0, r]).wait()
        _row_copy(h_ref, xs_ref, sem, r, p1_ref[0, 0, r]).wait()
        return carry

    lax.fori_loop(0, tile, issue, 0)
    lax.fori_loop(0, tile, drain, 0)


def _dispatch(h, pos0, pos1, zero_blocks, n_used, n_sorted):
    n, d = h.shape
    tp = PERM_TILE
    smem_rows = lambda: pl.BlockSpec((1, 1, tp), lambda i, zb, nu: (i, 0, 0),
                                     memory_space=pltpu.SMEM)
    grid_spec = pltpu.PrefetchScalarGridSpec(
        num_scalar_prefetch=2,
        grid=(n // tp,),
        in_specs=[smem_rows(), smem_rows(),
                  pl.BlockSpec((tp, d), lambda i, zb, nu: (i, 0))],
        out_specs=pl.BlockSpec(memory_space=pl.ANY),
        scratch_shapes=[pltpu.VMEM((MOE_TILE, d), F32), pltpu.SemaphoreType.DMA(())],
    )
    return pl.pallas_call(
        _dispatch_kernel,
        grid_spec=grid_spec,
        out_shape=jax.ShapeDtypeStruct((n_sorted, d), F32),
        compiler_params=_params("arbitrary"),
        name="moe_dispatch",
    )(zero_blocks, n_used, pos0.reshape(n // tp, 1, tp), pos1.reshape(n // tp, 1, tp), h)


def _expert_kernel(be_ref, nu_ref, xs_ref, wg_ref, wu_ref, wd_ref, ys_ref, a_scr):
    @pl.when(pl.program_id(0) < nu_ref[0])
    def _():
        ys_ref[...] = _swiglu_rows(xs_ref[...].astype(BF16), wg_ref.at[0], wu_ref.at[0],
                                   wd_ref.at[0], a_scr)

    @pl.when(pl.program_id(0) >= nu_ref[0])
    def _():
        ys_ref[...] = jnp.zeros(ys_ref.shape, F32)


def _experts(xs, wg, wu, wd, block_expert, n_used):
    n_sorted, d = xs.shape
    d_ff = wg.shape[2]
    tm = MOE_TILE
    grid_spec = pltpu.PrefetchScalarGridSpec(
        num_scalar_prefetch=2,
        grid=(n_sorted // tm,),
        in_specs=[
            pl.BlockSpec((tm, d), lambda i, be, nu: (jnp.minimum(i, nu[0] - 1), 0)),
            pl.BlockSpec((1, d, d_ff), lambda i, be, nu: (be[i], 0, 0)),
            pl.BlockSpec((1, d, d_ff), lambda i, be, nu: (be[i], 0, 0)),
            pl.BlockSpec((1, d_ff, d), lambda i, be, nu: (be[i], 0, 0)),
        ],
        out_specs=pl.BlockSpec((tm, d), lambda i, be, nu: (i, 0)),
        scratch_shapes=[pltpu.VMEM((tm, d_ff), BF16)],
    )
    return pl.pallas_call(
        _expert_kernel,
        grid_spec=grid_spec,
        out_shape=jax.ShapeDtypeStruct((n_sorted, d), F32),
        compiler_params=_params("arbitrary"),
        name="moe_experts",
    )(block_expert, n_used, xs, wg, wu, wd)


def _combine_kernel(p0_ref, p1_ref, x_ref, route_ref, gf_ref, ys_ref, o_ref, g_scr, sem,
                    *, final_norm):
    tile = x_ref.shape[0]

    def issue(r, carry):
        _row_copy(ys_ref, g_scr.at[0], sem, p0_ref[0, 0, r], r).start()
        _row_copy(ys_ref, g_scr.at[1], sem, p1_ref[0, 0, r], r).start()
        return carry

    def drain(r, carry):
        _row_copy(ys_ref, g_scr.at[0], sem, p0_ref[0, 0, r], r).wait()
        _row_copy(ys_ref, g_scr.at[1], sem, p1_ref[0, 0, r], r).wait()
        return carry

    lax.fori_loop(0, tile, issue, 0)
    lax.fori_loop(0, tile, drain, 0)
    w1 = route_ref[:, 2:3]
    w2 = route_ref[:, 3:4]
    out = x_ref[...] + w1 * g_scr[0] + w2 * g_scr[1]
    if final_norm:
        out = _rms(out, gf_ref[...])
    o_ref[...] = out


def _combine(x2, route, ys, pos0, pos1, gf, final_norm):
    n, d = x2.shape
    tp = PERM_TILE
    smem_rows = lambda: pl.BlockSpec((1, 1, tp), lambda i: (i, 0, 0), memory_space=pltpu.SMEM)
    return pl.pallas_call(
        functools.partial(_combine_kernel, final_norm=final_norm),
        grid=(n // tp,),
        in_specs=[smem_rows(), smem_rows(),
                  pl.BlockSpec((tp, d), lambda i: (i, 0)),
                  pl.BlockSpec((tp, LANES), lambda i: (i, 0)),
                  pl.BlockSpec((1, d), lambda i: (0, 0)),
                  pl.BlockSpec(memory_space=pl.ANY)],
        out_specs=pl.BlockSpec((tp, d), lambda i: (i, 0)),
        out_shape=jax.ShapeDtypeStruct((n, d), F32),
        scratch_shapes=[pltpu.VMEM((2, tp, d), F32), pltpu.SemaphoreType.DMA(())],
        compiler_params=_params("arbitrary"),
        name="moe_combine",
    )(pos0.reshape(n // tp, 1, tp), pos1.reshape(n // tp, 1, tp), x2, route,
      gf.reshape(1, d), ys)


def _routing_tables(route, n_exp):
    n = route.shape[0]
    tm = MOE_TILE
    i1 = route[:, 0].astype(jnp.int32)
    i2 = route[:, 1].astype(jnp.int32)
    oh1 = jax.nn.one_hot(i1, n_exp, dtype=jnp.int32)
    oh2 = jax.nn.one_hot(i2, n_exp, dtype=jnp.int32)
    tot = oh1 + oh2
    csum = jnp.cumsum(tot, axis=0)
    excl = csum - tot
    rank1 = jnp.sum(excl * oh1, axis=-1)
    rank2 = jnp.sum((excl + oh1) * oh2, axis=-1)
    counts = csum[-1]
    nblk = (counts + tm - 1) // tm
    blk_end = jnp.cumsum(nblk)
    blk_off = blk_end - nblk
    row_off = blk_off * tm
    pos0 = row_off[i1] + rank1
    pos1 = row_off[i2] + rank2
    n_blocks = TOP_K * n // tm + n_exp
    n_used = blk_end[-1]
    jb = jnp.minimum(jnp.arange(n_blocks, dtype=jnp.int32), n_used - 1)
    block_expert = jnp.sum((blk_end[None, :] <= jb[:, None]).astype(jnp.int32), axis=-1)
    block_expert = jnp.minimum(block_expert, n_exp - 1)
    zero_blocks = jnp.where(nblk > 0, blk_end - 1, n_blocks - 1).astype(jnp.int32)
    return (pos0.astype(jnp.int32), pos1.astype(jnp.int32), block_expert,
            n_used.astype(jnp.int32).reshape(1), zero_blocks, n_blocks * tm)


def _moe(x2, h, route, wg, wu, wd, gf, final_norm):
    n_exp = wg.shape[0]
    pos0, pos1, block_expert, n_used, zero_blocks, n_sorted = _routing_tables(route, n_exp)
    xs = _dispatch(h, pos0, pos1, zero_blocks, n_used, n_sorted)
    ys = _experts(xs, wg, wu, wd, block_expert, n_used)
    return _combine(x2, route, ys, pos0, pos1, gf, final_norm)


def kernel(x, norm_mix_g, w_in, conv_dw_w, conv_dw_b, conv_ln_g, conv_ln_b, w_conv_proj,
           ssm_a_re, ssm_a_im, ssm_log_dt, ssm_b_re, ssm_b_im, ssm_c_re, ssm_c_im, ssm_d,
           ssm_w_glu, ssm_b_glu, w_ssm_proj, w_out, norm_ffn_g,
           ffn_w_gate, ffn_w_up, ffn_w_down, router_w, router_b,
           moe_w_gate, moe_w_up, moe_w_down, final_norm_g):
    bsz, seq, d = x.shape
    depth = w_in.shape[0]
    d_conv = conv_dw_w.shape[-1]
    d_ssm = ssm_d.shape[-1]
    n = bsz * seq
    bf = lambda a: a.astype(BF16)
    x2 = x.reshape(n, d).astype(F32)
    for layer in range(depth):
        last = layer == depth - 1
        ch, u, gates = _inproj(x2, norm_mix_g[layer], bf(w_in[layer]), bsz, d_conv, d_ssm)
        hc = _conv_branch(ch.reshape(bsz, seq, d_conv), conv_dw_w[layer], conv_dw_b[layer],
                          conv_ln_g[layer], conv_ln_b[layer]).reshape(n, d_conv)
        m_a, m_b, coef = _s5_matrices(ssm_a_re[layer], ssm_a_im[layer], ssm_log_dt[layer],
                                      ssm_b_re[layer], ssm_b_im[layer], ssm_c_re[layer],
                                      ssm_c_im[layer], ssm_d[layer])
        y = _s5_branch(u, m_a, m_b, coef)
        i = layer // 2
        moe = layer % 2 == 1
        router = (router_w[i], router_b[i]) if moe else None
        outs = _merge(x2, hc, y, gates, bf(ssm_w_glu[layer]), ssm_b_glu[layer],
                      bf(w_ssm_proj[layer]), bf(w_conv_proj[layer]), bf(w_out[layer]),
                      norm_ffn_g[layer], router)
        if moe:
            x2, h, route = outs
            x2 = _moe(x2, h, route, bf(moe_w_gate[i]), bf(moe_w_up[i]), bf(moe_w_down[i]),
                      final_norm_g, last)
        else:
            x2, h = outs
            x2 = _ffn_dense(x2, h, bf(ffn_w_gate[i]), bf(ffn_w_up[i]), bf(ffn_w_down[i]),
                            final_norm_g, last)
    return x2.reshape(bsz, seq, d).astype(x.dtype)
```

```python
import functools
import math

import jax
import jax.numpy as jnp
from jax import lax
from jax.experimental import pallas as pl
from jax.experimental.pallas import tpu as pltpu

F32 = jnp.float32
BF16 = jnp.bfloat16

RMS_EPS = 1e-6
LN_EPS = 1e-5
CONV_WIDTH = 31
CONV_HALF = CONV_WIDTH // 2
SSM_GROUP = 16
SSM_STATE = 64
CHUNK = 16
TOP_K = 2
LANES = 128
SUBLANES = 8
VMEM_LIMIT = 56 * 1024 * 1024

ROW_TILE = 512
MOE_TILE = 256
PERM_TILE = 256
CONV_TILE = 128
FF_CHUNK = 256


def _params(*sem):
    return pltpu.CompilerParams(dimension_semantics=sem, vmem_limit_bytes=VMEM_LIMIT)


def _rms(xf, g):
    ms = jnp.mean(xf * xf, axis=-1, keepdims=True)
    return xf * lax.rsqrt(ms + RMS_EPS) * g


def _inproj_kernel(x_ref, g_ref, w_ref, ch_ref, u_ref, gate_ref):
    dc = ch_ref.shape[-1]
    ds_ = u_ref.shape[0] * LANES
    h = _rms(x_ref[...], g_ref[...]).astype(BF16)
    v = jnp.dot(h, w_ref[:, 0:dc], preferred_element_type=F32)
    gt = jnp.dot(h, w_ref[:, dc:2 * dc], preferred_element_type=F32)
    ch_ref[...] = v * jax.nn.sigmoid(gt)
    u = jnp.dot(h, w_ref[:, 2 * dc:2 * dc + ds_], preferred_element_type=F32)
    for q in range(u_ref.shape[0]):
        u_ref[q, 0] = u[:, q * LANES:(q + 1) * LANES]
    gate_ref[...] = jnp.dot(h, w_ref[:, 2 * dc + ds_:], preferred_element_type=F32).astype(BF16)


def _inproj(x2, g, w_bf, bsz, d_conv, d_ssm):
    n, d = x2.shape
    seq = n // bsz
    d_in = w_bf.shape[1]
    d_gate = d_in - 2 * d_conv - d_ssm
    tm = ROW_TILE
    per_seq = seq // tm
    nq = d_ssm // LANES
    return pl.pallas_call(
        _inproj_kernel,
        grid=(n // tm,),
        in_specs=[
            pl.BlockSpec((tm, d), lambda i: (i, 0)),
            pl.BlockSpec((1, d), lambda i: (0, 0)),
            pl.BlockSpec((d, d_in), lambda i: (0, 0)),
        ],
        out_specs=[
            pl.BlockSpec((tm, d_conv), lambda i: (i, 0)),
            pl.BlockSpec((nq, 1, tm, LANES), lambda i: (0, i // per_seq, i % per_seq, 0)),
            pl.BlockSpec((tm, d_gate), lambda i: (i, 0)),
        ],
        out_shape=[
            jax.ShapeDtypeStruct((n, d_conv), F32),
            jax.ShapeDtypeStruct((nq, bsz, seq, LANES), F32),
            jax.ShapeDtypeStruct((n, d_gate), BF16),
        ],
        compiler_params=_params("parallel"),
        name="inproj",
    )(x2, g.reshape(1, d), w_bf)


def _conv_kernel(h_ref, w_ref, b_ref, lg_ref, lb_ref, o_ref, pad_scr, tile_scr):
    seq, c = h_ref.shape[1], h_ref.shape[2]
    front = 2 * SUBLANES
    zeros = jnp.zeros((front, c), F32)
    pad_scr[0:front, :] = zeros
    pad_scr[front + seq:front + seq + front, :] = zeros
    pad_scr[front:front + seq, :] = h_ref[0]

    phases = [[] for _ in range(SUBLANES)]
    for k in range(CONV_WIDTH):
        off = k - CONV_HALF + front
        phases[off % SUBLANES].append((k, off - off % SUBLANES))

    def tile_body(ti, carry):
        base = pl.multiple_of(ti * CONV_TILE, CONV_TILE)
        win_ref = pad_scr.at[pl.ds(base, CONV_TILE + 2 * front), :]
        for lb in range(c // LANES):
            ls = slice(lb * LANES, (lb + 1) * LANES)
            acc = b_ref[:, ls]
            for s, taps in enumerate(phases):
                part = None
                for k, row0 in taps:
                    term = w_ref[k:k + 1, ls] * win_ref[pl.ds(row0, CONV_TILE + SUBLANES), ls]
                    part = term if part is None else part + term
                if part is not None:
                    acc = acc + part[s:s + CONV_TILE, :]
            tile_scr[:, ls] = acc
        hh = tile_scr[...]
        mu = jnp.mean(hh, axis=-1, keepdims=True)
        cen = hh - mu
        var = jnp.mean(cen * cen, axis=-1, keepdims=True)
        y = cen * lax.rsqrt(var + LN_EPS) * lg_ref[...] + lb_ref[...]
        o_ref[0, pl.ds(base, CONV_TILE), :] = (y * jax.nn.sigmoid(y)).astype(BF16)
        return carry

    lax.fori_loop(0, seq // CONV_TILE, tile_body, 0)


def _conv_branch(ch, dw_w, dw_b, ln_g, ln_b):
    b, seq, c = ch.shape
    front = 2 * SUBLANES
    vec = lambda a: a.reshape(1, c)
    return pl.pallas_call(
        _conv_kernel,
        grid=(b,),
        in_specs=[
            pl.BlockSpec((1, seq, c), lambda i: (i, 0, 0)),
            pl.BlockSpec((CONV_WIDTH, c), lambda i: (0, 0)),
            pl.BlockSpec((1, c), lambda i: (0, 0)),
            pl.BlockSpec((1, c), lambda i: (0, 0)),
            pl.BlockSpec((1, c), lambda i: (0, 0)),
        ],
        out_specs=pl.BlockSpec((1, seq, c), lambda i: (i, 0, 0)),
        out_shape=jax.ShapeDtypeStruct((b, seq, c), BF16),
        scratch_shapes=[
            pltpu.VMEM((seq + 2 * front, c), F32),
            pltpu.VMEM((CONV_TILE, c), F32),
        ],
        compiler_params=_params("parallel"),
        name="conv_branch",
    )(ch, dw_w, vec(dw_b), vec(ln_g), vec(ln_b))


def _s5_matrices(a_re, a_im, log_dt, b_re, b_im, c_re, c_im, d_skip):
    t = CHUNK
    n_layers, _, g, p = a_re.shape
    c = b_re.shape[-1]
    f = lambda a: a.astype(F32)
    a_re, a_im, c_re, c_im = f(a_re), f(a_im), f(c_re), f(c_im)
    dt = jnp.exp(f(log_dt))[..., None]
    lam_re, lam_im = (a_re * dt)[:, :, :, None, :], (a_im * dt)[:, :, :, None, :]
    ks = jnp.arange(t + 1, dtype=F32)[:, None]
    mag = jnp.exp(ks * lam_re)
    pw_re, pw_im = mag * jnp.cos(ks * lam_im), mag * jnp.sin(ks * lam_im)
    ab_re, ab_im = pw_re[:, :, :, 1], pw_im[:, :, :, 1]
    n_re, n_im = ab_re - 1.0, ab_im
    den = a_re * a_re + a_im * a_im
    q_re = ((n_re * a_re + n_im * a_im) / den)[:, :, :, None, :]
    q_im = ((n_im * a_re - n_re * a_im) / den)[:, :, :, None, :]
    bt_re = jnp.swapaxes(f(b_re), -1, -2)[:, None]
    bt_im = jnp.swapaxes(f(b_im), -1, -2)[:, None]
    bb_re = q_re * bt_re - q_im * bt_im
    bb_im = q_re * bt_im + q_im * bt_re
    kr, ki = pw_re[:, :, :, :, None, :], pw_im[:, :, :, :, None, :]
    wb_re = kr * bb_re[:, :, :, None] - ki * bb_im[:, :, :, None]
    wb_im = kr * bb_im[:, :, :, None] + ki * bb_re[:, :, :, None]
    vc_re = kr * c_re[:, :, :, None] - ki * c_im[:, :, :, None]
    vc_im = kr * c_im[:, :, :, None] + ki * c_re[:, :, :, None]
    flat = lambda w: w.reshape(n_layers, g, t * c, p)
    zf_re, zf_im = flat(wb_re[:, 0, :, t - 1::-1]), flat(wb_im[:, 0, :, t - 1::-1])
    zb_re, zb_im = flat(wb_re[:, 1, :, :t]), flat(wb_im[:, 1, :, :t])
    m_a = jnp.concatenate([zf_re, zf_im, zf_im, zf_re, zb_re, zb_im, zb_im, zb_re], axis=-1)
    m_o = jnp.concatenate([flat(vc_re[:, 0, :, 1:]), -flat(vc_im[:, 0, :, 1:]),
                           flat(vc_re[:, 1, :, :0:-1]), -flat(vc_im[:, 1, :, :0:-1])], axis=-1)
    m_o = jnp.swapaxes(m_o, -1, -2)
    hi = lax.Precision.HIGHEST
    kk = (jnp.einsum("ldgcp,ldgkop->ldgcko", bb_re, vc_re[:, :, :, :t], precision=hi)
          - jnp.einsum("ldgcp,ldgkop->ldgcko", bb_im, vc_im[:, :, :, :t], precision=hi))
    kf, kb = kk[:, 0], kk[:, 1]
    diag = kf[:, :, :, 0] + kb[:, :, :, 0] + (f(d_skip).reshape(n_layers, g, 1, c)
                                              * jnp.eye(c, dtype=F32))
    lags = jnp.concatenate([kb[:, :, :, :0:-1], diag[:, :, :, None], kf[:, :, :, 1:]], axis=3)
    lags = lags.reshape(n_layers, g, c, (2 * t - 1) * c)
    m_intra = jnp.stack([lags[..., (t - 1 - j) * c:(2 * t - 1 - j) * c] for j in range(t)], axis=2)
    m_b = jnp.concatenate([m_intra.reshape(n_layers, g, t * c, t * c), m_o], axis=2)
    at_re, at_im = pw_re[:, :, :, t], pw_im[:, :, :, t]
    rows = [jnp.concatenate([at_re[:, 0], at_re[:, 0]], -1),
            jnp.concatenate([-at_im[:, 0], at_im[:, 0]], -1),
            jnp.concatenate([at_re[:, 1], at_re[:, 1]], -1),
            jnp.concatenate([-at_im[:, 1], at_im[:, 1]], -1)]
    coef = jnp.stack(rows + [jnp.zeros_like(rows[0])] * (SUBLANES - len(rows)), axis=2)
    return m_a.astype(BF16), m_b.astype(BF16), coef


def _block_transpose(v):
    nblk = len(v)
    blk = lax.broadcasted_iota(jnp.int32, v[0].shape, 1) // SSM_GROUP
    d = nblk // 2
    while d >= 1:
        keep_lo = (blk & d) == 0
        out = list(v)
        for r in range(nblk):
            if r & d:
                continue
            lo, hi = v[r], v[r + d]
            out[r] = jnp.where(keep_lo, lo, pltpu.roll(hi, d * SSM_GROUP, 1))
            out[r + d] = jnp.where(keep_lo, pltpu.roll(lo, LANES - d * SSM_GROUP, 1), hi)
        v = out
        d //= 2
    return v


def _s5_kernel(u_ref, ma_ref, mb_ref, coef_ref, y_ref, x_scr, z_scr, cin_scr):
    nb, seq = u_ref.shape[1], u_ref.shape[2]
    gpl, halves, rows = x_scr.shape[0:3]
    kx = halves * LANES
    sw = coef_ref.shape[-1]
    n_chunks = seq // CHUNK
    cpt = SUBLANES // nb
    n_tiles = rows // SUBLANES
    span = SUBLANES * CHUNK
    pos_per_half = LANES // SSM_GROUP

    def relayout(co, to_chunks):
        tok0 = pl.multiple_of(co * span, span)
        row0 = pl.multiple_of(co * SUBLANES * nb, SUBLANES * nb)
        for b in range(nb):
            tok_ref = (u_ref if to_chunks else y_ref).at[0, b, pl.ds(tok0, span), :]
            for h in range(halves):
                chunk_refs = [x_scr.at[gp, h, pl.ds(row0, SUBLANES * nb), :] for gp in range(gpl)]
                tok_rows = [pl.ds(h * pos_per_half + ip, SUBLANES, stride=CHUNK)
                            for ip in range(pos_per_half)]
                chunk_rows = pl.ds(b, SUBLANES, stride=nb)
                if to_chunks:
                    dst = _block_transpose([tok_ref[r, :] for r in tok_rows])
                    for gp in range(gpl):
                        chunk_refs[gp][chunk_rows, :] = dst[gp]
                else:
                    dst = _block_transpose([chunk_refs[gp][chunk_rows, :] for gp in range(gpl)])
                    for ip in range(pos_per_half):
                        tok_ref[tok_rows[ip], :] = dst[ip]

    def relayout_in(co, carry):
        relayout(co, True)
        return carry

    def relayout_out(co, carry):
        relayout(co, False)
        return carry

    lax.fori_loop(0, n_chunks // SUBLANES, relayout_in, 0)

    rc = min(rows, 512)
    row_id = lax.broadcasted_iota(jnp.int32, (SUBLANES, sw), 0)
    x_rows = lambda gp, r0: jnp.concatenate(
        [x_scr[gp, h, r0:r0 + rc, :] for h in range(halves)], axis=-1).astype(BF16)
    for gp in range(gpl):
        for r0 in range(0, rows, rc):
            z_scr[r0:r0 + rc, :] = jnp.dot(x_rows(gp, r0), ma_ref[gp], preferred_element_type=F32)
        coef = coef_ref[gp]
        bc = lambda r: jnp.broadcast_to(coef[r:r + 1, :], (SUBLANES, sw))
        af1, af2, ab1, ab2 = bc(0), bc(1), bc(2), bc(3)

        def tile_step(ti, carry):
            sf, sfp, sb, sbp = carry
            rf = pl.multiple_of(ti * SUBLANES, SUBLANES)
            rb = pl.multiple_of((n_tiles - 1 - ti) * SUBLANES, SUBLANES)
            zf = z_scr[pl.ds(rf, SUBLANES), 0:sw]
            zfp = z_scr[pl.ds(rf, SUBLANES), sw:2 * sw]
            zb = z_scr[pl.ds(rb, SUBLANES), 2 * sw:3 * sw]
            zbp = z_scr[pl.ds(rb, SUBLANES), 3 * sw:4 * sw]
            cin_f, cin_b = sf, sb
            for k in range(cpt):
                nf, nfp = af1 * sf + af2 * sfp + zf, af1 * sfp - af2 * sf + zfp
                sf, sfp = pltpu.roll(nf, nb, 0), pltpu.roll(nfp, nb, 0)
                kb = cpt - 1 - k
                nbk, nbkp = ab1 * sb + ab2 * sbp + zb, ab1 * sbp - ab2 * sb + zbp
                sb, sbp = pltpu.roll(nbk, SUBLANES - nb, 0), pltpu.roll(nbkp, SUBLANES - nb, 0)
                if k < cpt - 1:
                    cin_f = jnp.where(row_id >= (k + 1) * nb, sf, cin_f)
                    cin_b = jnp.where(row_id < kb * nb, sb, cin_b)
            cin_scr[pl.ds(rf, SUBLANES), 0:sw] = cin_f
            cin_scr[pl.ds(rb, SUBLANES), sw:2 * sw] = cin_b
            return sf, sfp, sb, sbp

        zero = jnp.zeros((SUBLANES, sw), F32)
        lax.fori_loop(0, n_tiles, tile_step, (zero, zero, zero, zero), unroll=2)

        for r0 in range(0, rows, rc):
            y = (jnp.dot(x_rows(gp, r0), mb_ref[gp, 0:kx, :], preferred_element_type=F32)
                 + jnp.dot(cin_scr[r0:r0 + rc, :].astype(BF16), mb_ref[gp, kx:, :],
                           preferred_element_type=F32))
            for h in range(halves):
                x_scr[gp, h, r0:r0 + rc, :] = y[:, h * LANES:(h + 1) * LANES]

    lax.fori_loop(0, n_chunks // SUBLANES, relayout_out, 0)


def _s5_branch(u, m_a, m_b, coef):
    nq, b, seq, _ = u.shape
    g = m_a.shape[0]
    gpl = g // nq
    n_chunks = seq // CHUNK
    assert SUBLANES % b == 0 and n_chunks % SUBLANES == 0 and gpl * SSM_GROUP == LANES
    rows = n_chunks * b
    kx, nz, kb = m_a.shape[1], m_a.shape[2], m_b.shape[1]
    return pl.pallas_call(
        _s5_kernel,
        grid=(nq,),
        in_specs=[
            pl.BlockSpec((1, b, seq, LANES), lambda i: (i, 0, 0, 0), pipeline_mode=pl.Buffered(1)),
            pl.BlockSpec((gpl, kx, nz), lambda i: (i, 0, 0)),
            pl.BlockSpec((gpl, kb, kx), lambda i: (i, 0, 0)),
            pl.BlockSpec((gpl, SUBLANES, coef.shape[-1]), lambda i: (i, 0, 0)),
        ],
        out_specs=pl.BlockSpec((1, b, seq, LANES), lambda i: (i, 0, 0, 0)),
        out_shape=jax.ShapeDtypeStruct((nq, b, seq, LANES), F32),
        scratch_shapes=[
            pltpu.VMEM((gpl, kx // LANES, rows, LANES), F32),
            pltpu.VMEM((rows, nz), F32),
            pltpu.VMEM((rows, kb - kx), F32),
        ],
        compiler_params=_params("parallel"),
        name="s5_scan",
    )(u, m_a, m_b, coef)


def _merge_kernel(x_ref, hc_ref, y_ref, gate_ref, wglu_ref, bglu_ref, ws_ref, wc_ref,
                  wout_ref, gn_ref, *rest, moe):
    if moe:
        rw_ref, rb_ref, xo_ref, h_ref, route_ref = rest
    else:
        xo_ref, h_ref = rest
    d = x_ref.shape[-1]
    yg = jax.nn.gelu(jnp.concatenate([y_ref[q, 0] for q in range(y_ref.shape[0])], axis=-1))
    glu = jnp.dot(yg.astype(BF16), wglu_ref[...], preferred_element_type=F32) + bglu_ref[...]
    y2 = yg * jax.nn.sigmoid(glu)
    br_s = jnp.dot(y2.astype(BF16), ws_ref[...], preferred_element_type=F32)
    br_c = jnp.dot(hc_ref[...], wc_ref[...], preferred_element_type=F32)
    gc = gate_ref[:, 0:d].astype(F32)
    gs = gate_ref[:, d:2 * d].astype(F32)
    merged = jax.nn.sigmoid(gc) * br_c + jax.nn.sigmoid(gs) * br_s
    xn = x_ref[...] + jnp.dot(merged.astype(BF16), wout_ref[...], preferred_element_type=F32)
    xo_ref[...] = xn
    h = _rms(xn, gn_ref[...])
    if not moe:
        h_ref[...] = h.astype(h_ref.dtype)
    if moe:
        _rows_to_tiles(h_ref, h)
        h_hi = h.astype(BF16)
        h_lo = (h - h_hi.astype(F32)).astype(BF16)
        logits = (jnp.dot(h_hi, rw_ref[0], preferred_element_type=F32)
                  + jnp.dot(h_hi, rw_ref[1], preferred_element_type=F32)
                  + jnp.dot(h_lo, rw_ref[0], preferred_element_type=F32)) + rb_ref[1:2, :]
        lane = lax.broadcasted_iota(jnp.int32, logits.shape, 1).astype(F32)
        valid = rb_ref[0:1, :] > 0.0
        neg = jnp.float32(-jnp.inf)
        logits = jnp.where(valid, logits, neg)
        m1 = jnp.max(logits, axis=-1, keepdims=True)
        i1 = jnp.min(jnp.where(logits == m1, lane, float(LANES)), axis=-1, keepdims=True)
        rest_l = jnp.where(lane == i1, neg, logits)
        m2 = jnp.max(rest_l, axis=-1, keepdims=True)
        i2 = jnp.min(jnp.where(rest_l == m2, lane, float(LANES)), axis=-1, keepdims=True)
        e2 = jnp.exp(m2 - m1)
        w1 = 1.0 / (1.0 + e2)
        w2 = e2 / (1.0 + e2)
        route_ref[...] = jnp.where(lane == 0.0, i1, jnp.where(lane == 1.0, i2,
                                   jnp.where(lane == 2.0, w1, jnp.where(lane == 3.0, w2, 0.0))))


def _merge(x2, hc, y, gates, wglu, bglu, ws, wc, wout, gn, router=None):
    n, d = x2.shape
    ds_ = hc.shape[1]
    nq, _, seq, _ = y.shape
    tm = ROW_TILE
    per_seq = seq // tm
    moe = router is not None
    row = lambda w: pl.BlockSpec((tm, w), lambda i: (i, 0))
    full = lambda a: pl.BlockSpec(a.shape, lambda i: (0,) * a.ndim)
    y_spec = pl.BlockSpec((nq, 1, tm, LANES), lambda i: (0, i // per_seq, i % per_seq, 0))
    ins = [x2, hc, y, gates, wglu, bglu.reshape(1, ds_), ws, wc, wout, gn.reshape(1, d)]
    in_specs = [row(d), row(ds_), y_spec, row(gates.shape[1])] + [full(a) for a in ins[4:]]
    out_specs = [row(d), row(d)]
    out_shape = [jax.ShapeDtypeStruct((n, d), F32), jax.ShapeDtypeStruct((n, d), BF16)]
    if moe:
        tr = d // LANES
        out_specs[1] = pl.BlockSpec((tm * tr, LANES), lambda i: (i, 0))
        out_shape[1] = jax.ShapeDtypeStruct((n * tr, LANES), F32)
        rw, rb = router
        n_exp = rw.shape[1]
        rw_f = jnp.zeros((d, LANES), F32).at[:, :n_exp].set(rw.astype(F32))
        rw_hi = rw_f.astype(BF16)
        rw_pad = jnp.stack([rw_hi, (rw_f - rw_hi.astype(F32)).astype(BF16)])
        rb_pad = jnp.zeros((SUBLANES, LANES), F32).at[0, :n_exp].set(1.0).at[1, :n_exp].set(rb.astype(F32))
        ins += [rw_pad, rb_pad]
        in_specs += [full(rw_pad), full(rb_pad)]
        out_specs.append(row(LANES))
        out_shape.append(jax.ShapeDtypeStruct((n, LANES), F32))
    return pl.pallas_call(
        functools.partial(_merge_kernel, moe=moe),
        grid=(n // tm,),
        in_specs=in_specs,
        out_specs=out_specs,
        out_shape=out_shape,
        compiler_params=_params("parallel"),
        name="merge_moe" if moe else "merge",
    )(*ins)


def _swiglu_rows(h_bf, wg_ref, wu_ref, wd_ref, a_scr):
    d_ff = a_scr.shape[1]
    for c0 in range(0, d_ff, FF_CHUNK):
        cs = slice(c0, c0 + FF_CHUNK)
        gt = jnp.dot(h_bf, wg_ref[:, cs], preferred_element_type=F32)
        up = jnp.dot(h_bf, wu_ref[:, cs], preferred_element_type=F32)
        a_scr[:, cs] = (gt * jax.nn.sigmoid(gt) * up).astype(BF16)
    return jnp.dot(a_scr[...], wd_ref[...], preferred_element_type=F32)


def _ffn_kernel(x_ref, h_ref, wg_ref, wu_ref, wd_ref, gf_ref, o_ref, a_scr, *, final_norm):
    out = x_ref[...] + _swiglu_rows(h_ref[...], wg_ref, wu_ref, wd_ref, a_scr)
    if final_norm:
        out = _rms(out, gf_ref[...])
    o_ref[...] = out


def _ffn_dense(x2, h, wg, wu, wd, gf, final_norm):
    n, d = x2.shape
    d_ff = wg.shape[1]
    tm = ROW_TILE
    once = pl.Buffered(1)
    return pl.pallas_call(
        functools.partial(_ffn_kernel, final_norm=final_norm),
        grid=(n // tm,),
        in_specs=[
            pl.BlockSpec((tm, d), lambda i: (i, 0)),
            pl.BlockSpec((tm, d), lambda i: (i, 0)),
            pl.BlockSpec((d, d_ff), lambda i: (0, 0), pipeline_mode=once),
            pl.BlockSpec((d, d_ff), lambda i: (0, 0), pipeline_mode=once),
            pl.BlockSpec((d_ff, d), lambda i: (0, 0), pipeline_mode=once),
            pl.BlockSpec((1, d), lambda i: (0, 0)),
        ],
        out_specs=pl.BlockSpec((tm, d), lambda i: (i, 0)),
        out_shape=jax.ShapeDtypeStruct((n, d), F32),
        scratch_shapes=[pltpu.VMEM((tm, d_ff), BF16)],
        compiler_params=_params("parallel"),
        name="ffn_dense",
    )(x2, h, wg, wu, wd, gf.reshape(1, d))


TOKEN_ROWS = SUBLANES
PERM_UNROLL = 4


def _rows_to_tiles(ref, val):
    m = val.shape[0]
    for j in range(TOKEN_ROWS):
        ref[pl.ds(j, m, stride=TOKEN_ROWS), :] = val[:, j * LANES:(j + 1) * LANES]


def _tiles_to_rows(ref):
    m = ref.shape[0] // TOKEN_ROWS
    return jnp.concatenate([ref[pl.ds(j, m, stride=TOKEN_ROWS), :] for j in range(TOKEN_ROWS)],
                           axis=-1)


def _token_copy(src, dst, sem, s, t):
    rows = lambda i: pl.ds(pl.multiple_of(i * TOKEN_ROWS, TOKEN_ROWS), TOKEN_ROWS)
    return pltpu.make_async_copy(src.at[rows(s), :], dst.at[rows(t), :], sem)


def _for_tokens(n_tok, fn):
    def body(i, carry):
        for j in range(PERM_UNROLL):
            fn(i * PERM_UNROLL + j)
        return carry
    lax.fori_loop(0, n_tok // PERM_UNROLL, body, 0)


def _dispatch_kernel(zb_ref, nu_ref, p0_ref, p1_ref, h_ref, xs_ref, zero_scr, sem):
    n_tok = h_ref.shape[0] // TOKEN_ROWS
    n_exp = zb_ref.shape[0]

    @pl.when(pl.program_id(0) == 0)
    def _():
        zero_scr[...] = jnp.zeros(zero_scr.shape, F32)
        rows = zero_scr.shape[0]

        def fill(blk):
            start = pl.multiple_of(blk * rows, rows)
            return pltpu.make_async_copy(zero_scr, xs_ref.at[pl.ds(start, rows), :], sem)

        for e in range(n_exp):
            fill(zb_ref[e]).start()
        for e in range(n_exp):
            fill(zb_ref[e]).wait()

        def fill_unused(blk, carry):
            fill(blk).start()
            fill(blk).wait()
            return carry

        lax.fori_loop(nu_ref[0], xs_ref.shape[0] // rows, fill_unused, 0)

    def copies(r):
        return (_token_copy(h_ref, xs_ref, sem, r, p0_ref[0, 0, r]),
                _token_copy(h_ref, xs_ref, sem, r, p1_ref[0, 0, r]))

    _for_tokens(n_tok, lambda r: [cp.start() for cp in copies(r)])
    _for_tokens(n_tok, lambda r: [cp.wait() for cp in copies(r)])


def _dispatch(h, pos0, pos1, zero_blocks, n_used, n_sorted):
    n = h.shape[0] // TOKEN_ROWS
    tp = PERM_TILE
    smem_rows = lambda: pl.BlockSpec((1, 1, tp), lambda i, zb, nu: (i, 0, 0),
                                     memory_space=pltpu.SMEM)
    grid_spec = pltpu.PrefetchScalarGridSpec(
        num_scalar_prefetch=2,
        grid=(n // tp,),
        in_specs=[smem_rows(), smem_rows(),
                  pl.BlockSpec((tp * TOKEN_ROWS, LANES), lambda i, zb, nu: (i, 0))],
        out_specs=pl.BlockSpec(memory_space=pl.ANY),
        scratch_shapes=[pltpu.VMEM((MOE_TILE * TOKEN_ROWS, LANES), F32),
                        pltpu.SemaphoreType.DMA(())],
    )
    return pl.pallas_call(
        _dispatch_kernel,
        grid_spec=grid_spec,
        out_shape=jax.ShapeDtypeStruct((n_sorted * TOKEN_ROWS, LANES), F32),
        compiler_params=_params("arbitrary"),
        name="moe_dispatch",
    )(zero_blocks, n_used, pos0.reshape(n // tp, 1, tp), pos1.reshape(n // tp, 1, tp), h)


def _expert_kernel(be_ref, nu_ref, xs_ref, wg_ref, wu_ref, wd_ref, ys_ref, a_scr):
    @pl.when(pl.program_id(0) < nu_ref[0])
    def _():
        _rows_to_tiles(ys_ref, _swiglu_rows(_tiles_to_rows(xs_ref).astype(BF16), wg_ref.at[0],
                                            wu_ref.at[0], wd_ref.at[0], a_scr))

    @pl.when(pl.program_id(0) >= nu_ref[0])
    def _():
        ys_ref[...] = jnp.zeros(ys_ref.shape, F32)


def _experts(xs, wg, wu, wd, block_expert, n_used):
    _, d, d_ff = wg.shape
    tm = MOE_TILE
    tile = (tm * TOKEN_ROWS, LANES)
    grid_spec = pltpu.PrefetchScalarGridSpec(
        num_scalar_prefetch=2,
        grid=(xs.shape[0] // tile[0],),
        in_specs=[
            pl.BlockSpec(tile, lambda i, be, nu: (jnp.minimum(i, nu[0] - 1), 0)),
            pl.BlockSpec((1, d, d_ff), lambda i, be, nu: (be[i], 0, 0)),
            pl.BlockSpec((1, d, d_ff), lambda i, be, nu: (be[i], 0, 0)),
            pl.BlockSpec((1, d_ff, d), lambda i, be, nu: (be[i], 0, 0)),
        ],
        out_specs=pl.BlockSpec(tile, lambda i, be, nu: (i, 0)),
        scratch_shapes=[pltpu.VMEM((tm, d_ff), BF16)],
    )
    return pl.pallas_call(
        _expert_kernel,
        grid_spec=grid_spec,
        out_shape=jax.ShapeDtypeStruct(xs.shape, F32),
        compiler_params=_params("arbitrary"),
        name="moe_experts",
    )(block_expert, n_used, xs, wg, wu, wd)


def _combine_kernel(p0_ref, p1_ref, x_ref, route_ref, gf_ref, ys_ref, o_ref, g_scr, sem,
                    *, final_norm):
    n_tok = x_ref.shape[0]

    def copies(r):
        return (_token_copy(ys_ref, g_scr.at[0], sem, p0_ref[0, 0, r], r),
                _token_copy(ys_ref, g_scr.at[1], sem, p1_ref[0, 0, r], r))

    _for_tokens(n_tok, lambda r: [cp.start() for cp in copies(r)])
    _for_tokens(n_tok, lambda r: [cp.wait() for cp in copies(r)])
    w1 = route_ref[:, 2:3]
    w2 = route_ref[:, 3:4]
    out = x_ref[...] + w1 * _tiles_to_rows(g_scr.at[0]) + w2 * _tiles_to_rows(g_scr.at[1])
    if final_norm:
        out = _rms(out, gf_ref[...])
    o_ref[...] = out


def _combine(x2, route, ys, pos0, pos1, gf, final_norm):
    n, d = x2.shape
    tp = PERM_TILE
    smem_rows = lambda: pl.BlockSpec((1, 1, tp), lambda i: (i, 0, 0), memory_space=pltpu.SMEM)
    return pl.pallas_call(
        functools.partial(_combine_kernel, final_norm=final_norm),
        grid=(n // tp,),
        in_specs=[smem_rows(), smem_rows(),
                  pl.BlockSpec((tp, d), lambda i: (i, 0)),
                  pl.BlockSpec((tp, LANES), lambda i: (i, 0)),
                  pl.BlockSpec((1, d), lambda i: (0, 0)),
                  pl.BlockSpec(memory_space=pl.ANY)],
        out_specs=pl.BlockSpec((tp, d), lambda i: (i, 0)),
        out_shape=jax.ShapeDtypeStruct((n, d), F32),
        scratch_shapes=[pltpu.VMEM((2, tp * TOKEN_ROWS, LANES), F32),
                        pltpu.SemaphoreType.DMA(())],
        compiler_params=_params("arbitrary"),
        name="moe_combine",
    )(pos0.reshape(n // tp, 1, tp), pos1.reshape(n // tp, 1, tp), x2, route,
      gf.reshape(1, d), ys)


def _routing_tables(route, n_exp):
    n = route.shape[0]
    tm = MOE_TILE
    i1 = route[:, 0].astype(jnp.int32)
    i2 = route[:, 1].astype(jnp.int32)
    oh1 = jax.nn.one_hot(i1, n_exp, dtype=jnp.int32)
    oh2 = jax.nn.one_hot(i2, n_exp, dtype=jnp.int32)
    tot = oh1 + oh2
    csum = jnp.cumsum(tot, axis=0)
    excl = csum - tot
    rank1 = jnp.sum(excl * oh1, axis=-1)
    rank2 = jnp.sum((excl + oh1) * oh2, axis=-1)
    counts = csum[-1]
    nblk = (counts + tm - 1) // tm
    blk_end = jnp.cumsum(nblk)
    blk_off = blk_end - nblk
    row_off = blk_off * tm
    pos0 = row_off[i1] + rank1
    pos1 = row_off[i2] + rank2
    n_blocks = TOP_K * n // tm + n_exp
    n_used = blk_end[-1]
    jb = jnp.minimum(jnp.arange(n_blocks, dtype=jnp.int32), n_used - 1)
    block_expert = jnp.sum((blk_end[None, :] <= jb[:, None]).astype(jnp.int32), axis=-1)
    block_expert = jnp.minimum(block_expert, n_exp - 1)
    zero_blocks = jnp.where(nblk > 0, blk_end - 1, n_blocks - 1).astype(jnp.int32)
    return (pos0.astype(jnp.int32), pos1.astype(jnp.int32), block_expert,
            n_used.astype(jnp.int32).reshape(1), zero_blocks, n_blocks * tm)


def _moe(x2, h, route, wg, wu, wd, gf, final_norm):
    n_exp = wg.shape[0]
    pos0, pos1, block_expert, n_used, zero_blocks, n_sorted = _routing_tables(route, n_exp)
    xs = _dispatch(h, pos0, pos1, zero_blocks, n_used, n_sorted)
    ys = _experts(xs, wg, wu, wd, block_expert, n_used)
    return _combine(x2, route, ys, pos0, pos1, gf, final_norm)


def kernel(x, norm_mix_g, w_in, conv_dw_w, conv_dw_b, conv_ln_g, conv_ln_b, w_conv_proj,
           ssm_a_re, ssm_a_im, ssm_log_dt, ssm_b_re, ssm_b_im, ssm_c_re, ssm_c_im, ssm_d,
           ssm_w_glu, ssm_b_glu, w_ssm_proj, w_out, norm_ffn_g,
           ffn_w_gate, ffn_w_up, ffn_w_down, router_w, router_b,
           moe_w_gate, moe_w_up, moe_w_down, final_norm_g):
    bsz, seq, d = x.shape
    depth = w_in.shape[0]
    d_conv = conv_dw_w.shape[-1]
    d_ssm = ssm_d.shape[-1]
    n = bsz * seq
    bf = lambda a: a.astype(BF16)
    x2 = x.reshape(n, d).astype(F32)
    m_a, m_b, coef = _s5_matrices(ssm_a_re, ssm_a_im, ssm_log_dt, ssm_b_re, ssm_b_im,
                                  ssm_c_re, ssm_c_im, ssm_d)
    for layer in range(depth):
        last = layer == depth - 1
        ch, u, gates = _inproj(x2, norm_mix_g[layer], bf(w_in[layer]), bsz, d_conv, d_ssm)
        hc = _conv_branch(ch.reshape(bsz, seq, d_conv), conv_dw_w[layer], conv_dw_b[layer],
                          conv_ln_g[layer], conv_ln_b[layer]).reshape(n, d_conv)
        y = _s5_branch(u, m_a[layer], m_b[layer], coef[layer])
        i = layer // 2
        moe = layer % 2 == 1
        router = (router_w[i], router_b[i]) if moe else None
        outs = _merge(x2, hc, y, gates, bf(ssm_w_glu[layer]), ssm_b_glu[layer],
                      bf(w_ssm_proj[layer]), bf(w_conv_proj[layer]), bf(w_out[layer]),
                      norm_ffn_g[layer], router)
        if moe:
            x2, h, route = outs
            x2 = _moe(x2, h, route, bf(moe_w_gate[i]), bf(moe_w_up[i]), bf(moe_w_down[i]),
                      final_norm_g, last)
        else:
            x2, h = outs
            x2 = _ffn_dense(x2, h, bf(ffn_w_gate[i]), bf(ffn_w_up[i]), bf(ffn_w_down[i]),
                            final_norm_g, last)
    return x2.reshape(bsz, seq, d).astype(x.dtype)
```

```python
import functools
import math

import jax
import jax.numpy as jnp
from jax import lax
from jax.experimental import pallas as pl
from jax.experimental.pallas import tpu as pltpu

F32 = jnp.float32
BF16 = jnp.bfloat16

RMS_EPS = 1e-6
LN_EPS = 1e-5
CONV_WIDTH = 31
CONV_HALF = CONV_WIDTH // 2
SSM_GROUP = 16
SSM_STATE = 64
CHUNK = 16
TOP_K = 2
LANES = 128
SUBLANES = 8
VMEM_LIMIT = 56 * 1024 * 1024

ROW_TILE = 512
MOE_TILE = 256
PERM_TILE = 256
CONV_TILE = 128
FF_CHUNK = 256


def _params(*sem):
    return pltpu.CompilerParams(dimension_semantics=sem, vmem_limit_bytes=VMEM_LIMIT)


def _rms(xf, g):
    ms = jnp.mean(xf * xf, axis=-1, keepdims=True)
    return xf * lax.rsqrt(ms + RMS_EPS) * g


def _inproj_kernel(x_ref, g_ref, w_ref, ch_ref, u_ref, gate_ref):
    dc = ch_ref.shape[-1]
    ds_ = u_ref.shape[0] * LANES
    h = _rms(x_ref[...], g_ref[...]).astype(BF16)
    v = jnp.dot(h, w_ref[:, 0:dc], preferred_element_type=F32)
    gt = jnp.dot(h, w_ref[:, dc:2 * dc], preferred_element_type=F32)
    ch_ref[...] = v * jax.nn.sigmoid(gt)
    u = jnp.dot(h, w_ref[:, 2 * dc:2 * dc + ds_], preferred_element_type=F32)
    for q in range(u_ref.shape[0]):
        u_ref[q, 0] = u[:, q * LANES:(q + 1) * LANES]
    gate_ref[...] = jnp.dot(h, w_ref[:, 2 * dc + ds_:], preferred_element_type=F32).astype(BF16)


def _inproj(x2, g, w_bf, bsz, d_conv, d_ssm):
    n, d = x2.shape
    seq = n // bsz
    d_in = w_bf.shape[1]
    d_gate = d_in - 2 * d_conv - d_ssm
    tm = ROW_TILE
    per_seq = seq // tm
    nq = d_ssm // LANES
    return pl.pallas_call(
        _inproj_kernel,
        grid=(n // tm,),
        in_specs=[
            pl.BlockSpec((tm, d), lambda i: (i, 0)),
            pl.BlockSpec((1, d), lambda i: (0, 0)),
            pl.BlockSpec((d, d_in), lambda i: (0, 0)),
        ],
        out_specs=[
            pl.BlockSpec((tm, d_conv), lambda i: (i, 0)),
            pl.BlockSpec((nq, 1, tm, LANES), lambda i: (0, i // per_seq, i % per_seq, 0)),
            pl.BlockSpec((tm, d_gate), lambda i: (i, 0)),
        ],
        out_shape=[
            jax.ShapeDtypeStruct((n, d_conv), F32),
            jax.ShapeDtypeStruct((nq, bsz, seq, LANES), F32),
            jax.ShapeDtypeStruct((n, d_gate), BF16),
        ],
        compiler_params=_params("parallel"),
        name="inproj",
    )(x2, g.reshape(1, d), w_bf)


def _conv_kernel(h_ref, w_ref, b_ref, lg_ref, lb_ref, o_ref, pad_scr, tile_scr):
    seq, c = h_ref.shape[1], h_ref.shape[2]
    front = 2 * SUBLANES
    zeros = jnp.zeros((front, c), F32)
    pad_scr[0:front, :] = zeros
    pad_scr[front + seq:front + seq + front, :] = zeros
    pad_scr[front:front + seq, :] = h_ref[0]

    phases = [[] for _ in range(SUBLANES)]
    for k in range(CONV_WIDTH):
        off = k - CONV_HALF + front
        phases[off % SUBLANES].append((k, off - off % SUBLANES))

    def tile_body(ti, carry):
        base = pl.multiple_of(ti * CONV_TILE, CONV_TILE)
        win_ref = pad_scr.at[pl.ds(base, CONV_TILE + 2 * front), :]
        for lb in range(c // LANES):
            ls = slice(lb * LANES, (lb + 1) * LANES)
            acc = b_ref[:, ls]
            for s, taps in enumerate(phases):
                part = None
                for k, row0 in taps:
                    term = w_ref[k:k + 1, ls] * win_ref[pl.ds(row0, CONV_TILE + SUBLANES), ls]
                    part = term if part is None else part + term
                if part is not None:
                    acc = acc + part[s:s + CONV_TILE, :]
            tile_scr[:, ls] = acc
        hh = tile_scr[...]
        mu = jnp.mean(hh, axis=-1, keepdims=True)
        cen = hh - mu
        var = jnp.mean(cen * cen, axis=-1, keepdims=True)
        y = cen * lax.rsqrt(var + LN_EPS) * lg_ref[...] + lb_ref[...]
        o_ref[0, pl.ds(base, CONV_TILE), :] = (y * jax.nn.sigmoid(y)).astype(BF16)
        return carry

    lax.fori_loop(0, seq // CONV_TILE, tile_body, 0)


def _conv_branch(ch, dw_w, dw_b, ln_g, ln_b):
    b, seq, c = ch.shape
    front = 2 * SUBLANES
    vec = lambda a: a.reshape(1, c)
    return pl.pallas_call(
        _conv_kernel,
        grid=(b,),
        in_specs=[
            pl.BlockSpec((1, seq, c), lambda i: (i, 0, 0)),
            pl.BlockSpec((CONV_WIDTH, c), lambda i: (0, 0)),
            pl.BlockSpec((1, c), lambda i: (0, 0)),
            pl.BlockSpec((1, c), lambda i: (0, 0)),
            pl.BlockSpec((1, c), lambda i: (0, 0)),
        ],
        out_specs=pl.BlockSpec((1, seq, c), lambda i: (i, 0, 0)),
        out_shape=jax.ShapeDtypeStruct((b, seq, c), BF16),
        scratch_shapes=[
            pltpu.VMEM((seq + 2 * front, c), F32),
            pltpu.VMEM((CONV_TILE, c), F32),
        ],
        compiler_params=_params("parallel"),
        name="conv_branch",
    )(ch, dw_w, vec(dw_b), vec(ln_g), vec(ln_b))


def _s5_matrices(a_re, a_im, log_dt, b_re, b_im, c_re, c_im, d_skip):
    t = CHUNK
    n_layers, _, g, p = a_re.shape
    c = b_re.shape[-1]
    f = lambda a: a.astype(F32)
    a_re, a_im, c_re, c_im = f(a_re), f(a_im), f(c_re), f(c_im)
    dt = jnp.exp(f(log_dt))[..., None]
    lam_re, lam_im = (a_re * dt)[:, :, :, None, :], (a_im * dt)[:, :, :, None, :]
    ks = jnp.arange(t + 1, dtype=F32)[:, None]
    mag = jnp.exp(ks * lam_re)
    pw_re, pw_im = mag * jnp.cos(ks * lam_im), mag * jnp.sin(ks * lam_im)
    ab_re, ab_im = pw_re[:, :, :, 1], pw_im[:, :, :, 1]
    n_re, n_im = ab_re - 1.0, ab_im
    den = a_re * a_re + a_im * a_im
    q_re = ((n_re * a_re + n_im * a_im) / den)[:, :, :, None, :]
    q_im = ((n_im * a_re - n_re * a_im) / den)[:, :, :, None, :]
    bt_re = jnp.swapaxes(f(b_re), -1, -2)[:, None]
    bt_im = jnp.swapaxes(f(b_im), -1, -2)[:, None]
    bb_re = q_re * bt_re - q_im * bt_im
    bb_im = q_re * bt_im + q_im * bt_re
    kr, ki = pw_re[:, :, :, :, None, :], pw_im[:, :, :, :, None, :]
    wb_re = kr * bb_re[:, :, :, None] - ki * bb_im[:, :, :, None]
    wb_im = kr * bb_im[:, :, :, None] + ki * bb_re[:, :, :, None]
    vc_re = kr * c_re[:, :, :, None] - ki * c_im[:, :, :, None]
    vc_im = kr * c_im[:, :, :, None] + ki * c_re[:, :, :, None]
    flat = lambda w: w.reshape(n_layers, g, t * c, p)
    zf_re, zf_im = flat(wb_re[:, 0, :, t - 1::-1]), flat(wb_im[:, 0, :, t - 1::-1])
    zb_re, zb_im = flat(wb_re[:, 1, :, :t]), flat(wb_im[:, 1, :, :t])
    m_a = jnp.concatenate([zf_re, zf_im, zf_im, zf_re, zb_re, zb_im, zb_im, zb_re], axis=-1)
    m_o = jnp.concatenate([flat(vc_re[:, 0, :, 1:]), -flat(vc_im[:, 0, :, 1:]),
                           flat(vc_re[:, 1, :, :0:-1]), -flat(vc_im[:, 1, :, :0:-1])], axis=-1)
    m_o = jnp.swapaxes(m_o, -1, -2)
    kk = jnp.sum(bb_re[:, :, :, :, None, None] * vc_re[:, :, :, None, :t]
                 - bb_im[:, :, :, :, None, None] * vc_im[:, :, :, None, :t], axis=-1)
    kf, kb = kk[:, 0], kk[:, 1]
    diag = kf[:, :, :, 0] + kb[:, :, :, 0] + (f(d_skip).reshape(n_layers, g, 1, c)
                                              * jnp.eye(c, dtype=F32))
    lags = jnp.concatenate([kb[:, :, :, :0:-1], diag[:, :, :, None], kf[:, :, :, 1:]], axis=3)
    lags = lags.reshape(n_layers, g, c, (2 * t - 1) * c)
    m_intra = jnp.stack([lags[..., (t - 1 - j) * c:(2 * t - 1 - j) * c] for j in range(t)], axis=2)
    m_b = jnp.concatenate([m_intra.reshape(n_layers, g, t * c, t * c), m_o], axis=2)
    at_re, at_im = pw_re[:, :, :, t], pw_im[:, :, :, t]
    rows = [jnp.concatenate([at_re[:, 0], at_re[:, 0]], -1),
            jnp.concatenate([-at_im[:, 0], at_im[:, 0]], -1),
            jnp.concatenate([at_re[:, 1], at_re[:, 1]], -1),
            jnp.concatenate([-at_im[:, 1], at_im[:, 1]], -1)]
    coef = jnp.stack(rows + [jnp.zeros_like(rows[0])] * (SUBLANES - len(rows)), axis=2)
    return m_a.astype(BF16), m_b.astype(BF16), coef


def _block_transpose(v):
    nblk = len(v)
    blk = lax.broadcasted_iota(jnp.int32, v[0].shape, 1) // SSM_GROUP
    d = nblk // 2
    while d >= 1:
        keep_lo = (blk & d) == 0
        out = list(v)
        for r in range(nblk):
            if r & d:
                continue
            lo, hi = v[r], v[r + d]
            out[r] = jnp.where(keep_lo, lo, pltpu.roll(hi, d * SSM_GROUP, 1))
            out[r + d] = jnp.where(keep_lo, pltpu.roll(lo, LANES - d * SSM_GROUP, 1), hi)
        v = out
        d //= 2
    return v


def _s5_kernel(u_ref, ma_ref, mb_ref, coef_ref, y_ref, x_scr, z_scr, cin_scr):
    nb, seq = u_ref.shape[1], u_ref.shape[2]
    gpl, halves, rows = x_scr.shape[0:3]
    kx = halves * LANES
    sw = coef_ref.shape[-1]
    n_chunks = seq // CHUNK
    cpt = SUBLANES // nb
    n_tiles = rows // SUBLANES
    span = SUBLANES * CHUNK
    pos_per_half = LANES // SSM_GROUP

    def relayout(co, to_chunks):
        tok0 = pl.multiple_of(co * span, span)
        row0 = pl.multiple_of(co * SUBLANES * nb, SUBLANES * nb)
        for b in range(nb):
            tok_ref = (u_ref if to_chunks else y_ref).at[0, b, pl.ds(tok0, span), :]
            for h in range(halves):
                chunk_refs = [x_scr.at[gp, h, pl.ds(row0, SUBLANES * nb), :] for gp in range(gpl)]
                tok_rows = [pl.ds(h * pos_per_half + ip, SUBLANES, stride=CHUNK)
                            for ip in range(pos_per_half)]
                chunk_rows = pl.ds(b, SUBLANES, stride=nb)
                if to_chunks:
                    dst = _block_transpose([tok_ref[r, :] for r in tok_rows])
                    for gp in range(gpl):
                        chunk_refs[gp][chunk_rows, :] = dst[gp]
                else:
                    dst = _block_transpose([chunk_refs[gp][chunk_rows, :] for gp in range(gpl)])
                    for ip in range(pos_per_half):
                        tok_ref[tok_rows[ip], :] = dst[ip]

    def relayout_in(co, carry):
        relayout(co, True)
        return carry

    def relayout_out(co, carry):
        relayout(co, False)
        return carry

    lax.fori_loop(0, n_chunks // SUBLANES, relayout_in, 0)

    rc = min(rows, 512)
    row_id = lax.broadcasted_iota(jnp.int32, (SUBLANES, sw), 0)
    x_rows = lambda gp, r0: jnp.concatenate(
        [x_scr[gp, h, r0:r0 + rc, :] for h in range(halves)], axis=-1).astype(BF16)
    for gp in range(gpl):
        for r0 in range(0, rows, rc):
            z_scr[r0:r0 + rc, :] = jnp.dot(x_rows(gp, r0), ma_ref[gp], preferred_element_type=F32)
        coef = coef_ref[gp]
        bc = lambda r: jnp.broadcast_to(coef[r:r + 1, :], (SUBLANES, sw))
        af1, af2, ab1, ab2 = bc(0), bc(1), bc(2), bc(3)

        def tile_step(ti, carry):
            sf, sfp, sb, sbp = carry
            rf = pl.multiple_of(ti * SUBLANES, SUBLANES)
            rb = pl.multiple_of((n_tiles - 1 - ti) * SUBLANES, SUBLANES)
            zf = z_scr[pl.ds(rf, SUBLANES), 0:sw]
            zfp = z_scr[pl.ds(rf, SUBLANES), sw:2 * sw]
            zb = z_scr[pl.ds(rb, SUBLANES), 2 * sw:3 * sw]
            zbp = z_scr[pl.ds(rb, SUBLANES), 3 * sw:4 * sw]
            cin_f, cin_b = sf, sb
            for k in range(cpt):
                nf, nfp = af1 * sf + af2 * sfp + zf, af1 * sfp - af2 * sf + zfp
                sf, sfp = pltpu.roll(nf, nb, 0), pltpu.roll(nfp, nb, 0)
                kb = cpt - 1 - k
                nbk, nbkp = ab1 * sb + ab2 * sbp + zb, ab1 * sbp - ab2 * sb + zbp
                sb, sbp = pltpu.roll(nbk, SUBLANES - nb, 0), pltpu.roll(nbkp, SUBLANES - nb, 0)
                if k < cpt - 1:
                    cin_f = jnp.where(row_id >= (k + 1) * nb, sf, cin_f)
                    cin_b = jnp.where(row_id < kb * nb, sb, cin_b)
            cin_scr[pl.ds(rf, SUBLANES), 0:sw] = cin_f
            cin_scr[pl.ds(rb, SUBLANES), sw:2 * sw] = cin_b
            return sf, sfp, sb, sbp

        zero = jnp.zeros((SUBLANES, sw), F32)
        lax.fori_loop(0, n_tiles, tile_step, (zero, zero, zero, zero), unroll=2)

        for r0 in range(0, rows, rc):
            y = (jnp.dot(x_rows(gp, r0), mb_ref[gp, 0:kx, :], preferred_element_type=F32)
                 + jnp.dot(cin_scr[r0:r0 + rc, :].astype(BF16), mb_ref[gp, kx:, :],
                           preferred_element_type=F32))
            for h in range(halves):
                x_scr[gp, h, r0:r0 + rc, :] = y[:, h * LANES:(h + 1) * LANES]

    lax.fori_loop(0, n_chunks // SUBLANES, relayout_out, 0)


def _s5_branch(u, m_a, m_b, coef):
    nq, b, seq, _ = u.shape
    g = m_a.shape[0]
    gpl = g // nq
    n_chunks = seq // CHUNK
    assert SUBLANES % b == 0 and n_chunks % SUBLANES == 0 and gpl * SSM_GROUP == LANES
    rows = n_chunks * b
    kx, nz, kb = m_a.shape[1], m_a.shape[2], m_b.shape[1]
    return pl.pallas_call(
        _s5_kernel,
        grid=(nq,),
        in_specs=[
            pl.BlockSpec((1, b, seq, LANES), lambda i: (i, 0, 0, 0), pipeline_mode=pl.Buffered(1)),
            pl.BlockSpec((gpl, kx, nz), lambda i: (i, 0, 0)),
            pl.BlockSpec((gpl, kb, kx), lambda i: (i, 0, 0)),
            pl.BlockSpec((gpl, SUBLANES, coef.shape[-1]), lambda i: (i, 0, 0)),
        ],
        out_specs=pl.BlockSpec((1, b, seq, LANES), lambda i: (i, 0, 0, 0)),
        out_shape=jax.ShapeDtypeStruct((nq, b, seq, LANES), F32),
        scratch_shapes=[
            pltpu.VMEM((gpl, kx // LANES, rows, LANES), F32),
            pltpu.VMEM((rows, nz), F32),
            pltpu.VMEM((rows, kb - kx), F32),
        ],
        compiler_params=_params("parallel"),
        name="s5_scan",
    )(u, m_a, m_b, coef)


def _merge_kernel(x_ref, hc_ref, y_ref, gate_ref, wglu_ref, bglu_ref, ws_ref, wc_ref,
                  wout_ref, gn_ref, *rest, moe):
    if moe:
        rw_ref, rb_ref, xo_ref, h_ref, route_ref = rest
    else:
        xo_ref, h_ref = rest
    d = x_ref.shape[-1]
    yg = jax.nn.gelu(jnp.concatenate([y_ref[q, 0] for q in range(y_ref.shape[0])], axis=-1))
    glu = jnp.dot(yg.astype(BF16), wglu_ref[...], preferred_element_type=F32) + bglu_ref[...]
    y2 = yg * jax.nn.sigmoid(glu)
    br_s = jnp.dot(y2.astype(BF16), ws_ref[...], preferred_element_type=F32)
    br_c = jnp.dot(hc_ref[...], wc_ref[...], preferred_element_type=F32)
    gc = gate_ref[:, 0:d].astype(F32)
    gs = gate_ref[:, d:2 * d].astype(F32)
    merged = jax.nn.sigmoid(gc) * br_c + jax.nn.sigmoid(gs) * br_s
    xn = x_ref[...] + jnp.dot(merged.astype(BF16), wout_ref[...], preferred_element_type=F32)
    xo_ref[...] = xn
    h = _rms(xn, gn_ref[...])
    if not moe:
        h_ref[...] = h.astype(h_ref.dtype)
    if moe:
        _rows_to_tiles(h_ref, h)
        h_hi = h.astype(BF16)
        h_lo = (h - h_hi.astype(F32)).astype(BF16)
        logits = (jnp.dot(h_hi, rw_ref[0], preferred_element_type=F32)
                  + jnp.dot(h_hi, rw_ref[1], preferred_element_type=F32)
                  + jnp.dot(h_lo, rw_ref[0], preferred_element_type=F32)) + rb_ref[1:2, :]
        lane = lax.broadcasted_iota(jnp.int32, logits.shape, 1).astype(F32)
        valid = rb_ref[0:1, :] > 0.0
        neg = jnp.float32(-jnp.inf)
        logits = jnp.where(valid, logits, neg)
        m1 = jnp.max(logits, axis=-1, keepdims=True)
        i1 = jnp.min(jnp.where(logits == m1, lane, float(LANES)), axis=-1, keepdims=True)
        rest_l = jnp.where(lane == i1, neg, logits)
        m2 = jnp.max(rest_l, axis=-1, keepdims=True)
        i2 = jnp.min(jnp.where(rest_l == m2, lane, float(LANES)), axis=-1, keepdims=True)
        e2 = jnp.exp(m2 - m1)
        w1 = 1.0 / (1.0 + e2)
        w2 = e2 / (1.0 + e2)
        route_ref[...] = jnp.where(lane == 0.0, i1, jnp.where(lane == 1.0, i2,
                                   jnp.where(lane == 2.0, w1, jnp.where(lane == 3.0, w2, 0.0))))


def _merge(x2, hc, y, gates, wglu, bglu, ws, wc, wout, gn, router=None):
    n, d = x2.shape
    ds_ = hc.shape[1]
    nq, _, seq, _ = y.shape
    tm = ROW_TILE
    per_seq = seq // tm
    moe = router is not None
    row = lambda w: pl.BlockSpec((tm, w), lambda i: (i, 0))
    full = lambda a: pl.BlockSpec(a.shape, lambda i: (0,) * a.ndim)
    y_spec = pl.BlockSpec((nq, 1, tm, LANES), lambda i: (0, i // per_seq, i % per_seq, 0))
    ins = [x2, hc, y, gates, wglu, bglu.reshape(1, ds_), ws, wc, wout, gn.reshape(1, d)]
    in_specs = [row(d), row(ds_), y_spec, row(gates.shape[1])] + [full(a) for a in ins[4:]]
    out_specs = [row(d), row(d)]
    out_shape = [jax.ShapeDtypeStruct((n, d), F32), jax.ShapeDtypeStruct((n, d), BF16)]
    if moe:
        tr = d // LANES
        out_specs[1] = pl.BlockSpec((tm * tr, LANES), lambda i: (i, 0))
        out_shape[1] = jax.ShapeDtypeStruct((n * tr, LANES), F32)
        rw, rb = router
        n_exp = rw.shape[1]
        rw_f = jnp.zeros((d, LANES), F32).at[:, :n_exp].set(rw.astype(F32))
        rw_hi = rw_f.astype(BF16)
        rw_pad = jnp.stack([rw_hi, (rw_f - rw_hi.astype(F32)).astype(BF16)])
        rb_pad = jnp.zeros((SUBLANES, LANES), F32).at[0, :n_exp].set(1.0).at[1, :n_exp].set(rb.astype(F32))
        ins += [rw_pad, rb_pad]
        in_specs += [full(rw_pad), full(rb_pad)]
        out_specs.append(row(LANES))
        out_shape.append(jax.ShapeDtypeStruct((n, LANES), F32))
    return pl.pallas_call(
        functools.partial(_merge_kernel, moe=moe),
        grid=(n // tm,),
        in_specs=in_specs,
        out_specs=out_specs,
        out_shape=out_shape,
        compiler_params=_params("parallel"),
        name="merge_moe" if moe else "merge",
    )(*ins)


def _swiglu_rows(h_bf, wg_ref, wu_ref, wd_ref, a_scr):
    d_ff = a_scr.shape[1]
    for c0 in range(0, d_ff, FF_CHUNK):
        cs = slice(c0, c0 + FF_CHUNK)
        gt = jnp.dot(h_bf, wg_ref[:, cs], preferred_element_type=F32)
        up = jnp.dot(h_bf, wu_ref[:, cs], preferred_element_type=F32)
        a_scr[:, cs] = (gt * jax.nn.sigmoid(gt) * up).astype(BF16)
    return jnp.dot(a_scr[...], wd_ref[...], preferred_element_type=F32)


def _ffn_kernel(x_ref, h_ref, wg_ref, wu_ref, wd_ref, gf_ref, o_ref, a_scr, *, final_norm):
    out = x_ref[...] + _swiglu_rows(h_ref[...], wg_ref, wu_ref, wd_ref, a_scr)
    if final_norm:
        out = _rms(out, gf_ref[...])
    o_ref[...] = out


def _ffn_dense(x2, h, wg, wu, wd, gf, final_norm):
    n, d = x2.shape
    d_ff = wg.shape[1]
    tm = ROW_TILE
    once = pl.Buffered(1)
    return pl.pallas_call(
        functools.partial(_ffn_kernel, final_norm=final_norm),
        grid=(n // tm,),
        in_specs=[
            pl.BlockSpec((tm, d), lambda i: (i, 0)),
            pl.BlockSpec((tm, d), lambda i: (i, 0)),
            pl.BlockSpec((d, d_ff), lambda i: (0, 0), pipeline_mode=once),
            pl.BlockSpec((d, d_ff), lambda i: (0, 0), pipeline_mode=once),
            pl.BlockSpec((d_ff, d), lambda i: (0, 0), pipeline_mode=once),
            pl.BlockSpec((1, d), lambda i: (0, 0)),
        ],
        out_specs=pl.BlockSpec((tm, d), lambda i: (i, 0)),
        out_shape=jax.ShapeDtypeStruct((n, d), F32),
        scratch_shapes=[pltpu.VMEM((tm, d_ff), BF16)],
        compiler_params=_params("parallel"),
        name="ffn_dense",
    )(x2, h, wg, wu, wd, gf.reshape(1, d))


TOKEN_ROWS = SUBLANES
PERM_UNROLL = 4


def _rows_to_tiles(ref, val):
    m = val.shape[0]
    for j in range(TOKEN_ROWS):
        ref[pl.ds(j, m, stride=TOKEN_ROWS), :] = val[:, j * LANES:(j + 1) * LANES]


def _tiles_to_rows(ref):
    m = ref.shape[0] // TOKEN_ROWS
    return jnp.concatenate([ref[pl.ds(j, m, stride=TOKEN_ROWS), :] for j in range(TOKEN_ROWS)],
                           axis=-1)


def _token_copy(src, dst, sem, s, t):
    rows = lambda i: pl.ds(pl.multiple_of(i * TOKEN_ROWS, TOKEN_ROWS), TOKEN_ROWS)
    return pltpu.make_async_copy(src.at[rows(s), :], dst.at[rows(t), :], sem)


def _for_tokens(n_tok, fn):
    def body(i, carry):
        for j in range(PERM_UNROLL):
            fn(i * PERM_UNROLL + j, j % 2)
        return carry
    lax.fori_loop(0, n_tok // PERM_UNROLL, body, 0)


def _dispatch_kernel(zb_ref, nu_ref, p0_ref, p1_ref, h_ref, xs_ref, zero_scr, sem):
    n_tok = h_ref.shape[0] // TOKEN_ROWS
    n_exp = zb_ref.shape[0]

    @pl.when(pl.program_id(0) == 0)
    def _():
        zero_scr[...] = jnp.zeros(zero_scr.shape, F32)
        rows = zero_scr.shape[0]

        def fill(blk):
            start = pl.multiple_of(blk * rows, rows)
            return pltpu.make_async_copy(zero_scr, xs_ref.at[pl.ds(start, rows), :], sem)

        for e in range(n_exp):
            fill(zb_ref[e]).start()
        for e in range(n_exp):
            fill(zb_ref[e]).wait()

        def fill_unused(blk, carry):
            fill(blk).start()
            fill(blk).wait()
            return carry

        lax.fori_loop(nu_ref[0], xs_ref.shape[0] // rows, fill_unused, 0)

    def copies(r):
        return (_token_copy(h_ref, xs_ref, sem, r, p0_ref[0, 0, r]),
                _token_copy(h_ref, xs_ref, sem, r, p1_ref[0, 0, r]))

    _for_tokens(n_tok, lambda r, lane: [cp.start(priority=lane) for cp in copies(r)])
    _for_tokens(n_tok, lambda r, lane: [cp.wait() for cp in copies(r)])


def _dispatch(h, pos0, pos1, zero_blocks, n_used, n_sorted):
    n = h.shape[0] // TOKEN_ROWS
    tp = PERM_TILE
    smem_rows = lambda: pl.BlockSpec((1, 1, tp), lambda i, zb, nu: (i, 0, 0),
                                     memory_space=pltpu.SMEM)
    grid_spec = pltpu.PrefetchScalarGridSpec(
        num_scalar_prefetch=2,
        grid=(n // tp,),
        in_specs=[smem_rows(), smem_rows(),
                  pl.BlockSpec((tp * TOKEN_ROWS, LANES), lambda i, zb, nu: (i, 0))],
        out_specs=pl.BlockSpec(memory_space=pl.ANY),
        scratch_shapes=[pltpu.VMEM((MOE_TILE * TOKEN_ROWS, LANES), F32),
                        pltpu.SemaphoreType.DMA(())],
    )
    return pl.pallas_call(
        _dispatch_kernel,
        grid_spec=grid_spec,
        out_shape=jax.ShapeDtypeStruct((n_sorted * TOKEN_ROWS, LANES), F32),
        compiler_params=_params("arbitrary"),
        name="moe_dispatch",
    )(zero_blocks, n_used, pos0.reshape(n // tp, 1, tp), pos1.reshape(n // tp, 1, tp), h)


def _expert_kernel(be_ref, nu_ref, xs_ref, wg_ref, wu_ref, wd_ref, ys_ref, a_scr):
    @pl.when(pl.program_id(0) < nu_ref[0])
    def _():
        _rows_to_tiles(ys_ref, _swiglu_rows(_tiles_to_rows(xs_ref).astype(BF16), wg_ref.at[0],
                                            wu_ref.at[0], wd_ref.at[0], a_scr))

    @pl.when(pl.program_id(0) >= nu_ref[0])
    def _():
        ys_ref[...] = jnp.zeros(ys_ref.shape, F32)


def _experts(xs, wg, wu, wd, block_expert, n_used):
    _, d, d_ff = wg.shape
    tm = MOE_TILE
    tile = (tm * TOKEN_ROWS, LANES)
    grid_spec = pltpu.PrefetchScalarGridSpec(
        num_scalar_prefetch=2,
        grid=(xs.shape[0] // tile[0],),
        in_specs=[
            pl.BlockSpec(tile, lambda i, be, nu: (jnp.minimum(i, nu[0] - 1), 0)),
            pl.BlockSpec((1, d, d_ff), lambda i, be, nu: (be[i], 0, 0)),
            pl.BlockSpec((1, d, d_ff), lambda i, be, nu: (be[i], 0, 0)),
            pl.BlockSpec((1, d_ff, d), lambda i, be, nu: (be[i], 0, 0)),
        ],
        out_specs=pl.BlockSpec(tile, lambda i, be, nu: (i, 0)),
        scratch_shapes=[pltpu.VMEM((tm, d_ff), BF16)],
    )
    return pl.pallas_call(
        _expert_kernel,
        grid_spec=grid_spec,
        out_shape=jax.ShapeDtypeStruct(xs.shape, F32),
        compiler_params=_params("arbitrary"),
        name="moe_experts",
    )(block_expert, n_used, xs, wg, wu, wd)


def _combine_kernel(p0_ref, p1_ref, x_ref, route_ref, gf_ref, ys_ref, o_ref, g_scr, sem,
                    *, final_norm):
    n_tok = x_ref.shape[0]

    def copies(r):
        return (_token_copy(ys_ref, g_scr.at[0], sem, p0_ref[0, 0, r], r),
                _token_copy(ys_ref, g_scr.at[1], sem, p1_ref[0, 0, r], r))

    _for_tokens(n_tok, lambda r, lane: [cp.start(priority=lane) for cp in copies(r)])
    _for_tokens(n_tok, lambda r, lane: [cp.wait() for cp in copies(r)])
    w1 = route_ref[:, 2:3]
    w2 = route_ref[:, 3:4]
    out = x_ref[...] + w1 * _tiles_to_rows(g_scr.at[0]) + w2 * _tiles_to_rows(g_scr.at[1])
    if final_norm:
        out = _rms(out, gf_ref[...])
    o_ref[...] = out


def _combine(x2, route, ys, pos0, pos1, gf, final_norm):
    n, d = x2.shape
    tp = PERM_TILE
    smem_rows = lambda: pl.BlockSpec((1, 1, tp), lambda i: (i, 0, 0), memory_space=pltpu.SMEM)
    return pl.pallas_call(
        functools.partial(_combine_kernel, final_norm=final_norm),
        grid=(n // tp,),
        in_specs=[smem_rows(), smem_rows(),
                  pl.BlockSpec((tp, d), lambda i: (i, 0)),
                  pl.BlockSpec((tp, LANES), lambda i: (i, 0)),
                  pl.BlockSpec((1, d), lambda i: (0, 0)),
                  pl.BlockSpec(memory_space=pl.ANY)],
        out_specs=pl.BlockSpec((tp, d), lambda i: (i, 0)),
        out_shape=jax.ShapeDtypeStruct((n, d), F32),
        scratch_shapes=[pltpu.VMEM((2, tp * TOKEN_ROWS, LANES), F32),
                        pltpu.SemaphoreType.DMA(())],
        compiler_params=_params("arbitrary"),
        name="moe_combine",
    )(pos0.reshape(n // tp, 1, tp), pos1.reshape(n // tp, 1, tp), x2, route,
      gf.reshape(1, d), ys)


def _routing_tables(route, n_exp):
    n = route.shape[0]
    tm = MOE_TILE
    i1 = route[:, 0].astype(jnp.int32)
    i2 = route[:, 1].astype(jnp.int32)
    oh1 = jax.nn.one_hot(i1, n_exp, dtype=jnp.int32)
    oh2 = jax.nn.one_hot(i2, n_exp, dtype=jnp.int32)
    tot = oh1 + oh2
    csum = jnp.cumsum(tot, axis=0)
    excl = csum - tot
    rank1 = jnp.sum(excl * oh1, axis=-1)
    rank2 = jnp.sum((excl + oh1) * oh2, axis=-1)
    counts = csum[-1]
    nblk = (counts + tm - 1) // tm
    blk_end = jnp.cumsum(nblk)
    blk_off = blk_end - nblk
    row_off = blk_off * tm
    pos0 = row_off[i1] + rank1
    pos1 = row_off[i2] + rank2
    n_blocks = TOP_K * n // tm + n_exp
    n_used = blk_end[-1]
    jb = jnp.minimum(jnp.arange(n_blocks, dtype=jnp.int32), n_used - 1)
    block_expert = jnp.sum((blk_end[None, :] <= jb[:, None]).astype(jnp.int32), axis=-1)
    block_expert = jnp.minimum(block_expert, n_exp - 1)
    zero_blocks = jnp.where(nblk > 0, blk_end - 1, n_blocks - 1).astype(jnp.int32)
    return (pos0.astype(jnp.int32), pos1.astype(jnp.int32), block_expert,
            n_used.astype(jnp.int32).reshape(1), zero_blocks, n_blocks * tm)


def _moe(x2, h, route, wg, wu, wd, gf, final_norm):
    n_exp = wg.shape[0]
    pos0, pos1, block_expert, n_used, zero_blocks, n_sorted = _routing_tables(route, n_exp)
    xs = _dispatch(h, pos0, pos1, zero_blocks, n_used, n_sorted)
    ys = _experts(xs, wg, wu, wd, block_expert, n_used)
    return _combine(x2, route, ys, pos0, pos1, gf, final_norm)


def kernel(x, norm_mix_g, w_in, conv_dw_w, conv_dw_b, conv_ln_g, conv_ln_b, w_conv_proj,
           ssm_a_re, ssm_a_im, ssm_log_dt, ssm_b_re, ssm_b_im, ssm_c_re, ssm_c_im, ssm_d,
           ssm_w_glu, ssm_b_glu, w_ssm_proj, w_out, norm_ffn_g,
           ffn_w_gate, ffn_w_up, ffn_w_down, router_w, router_b,
           moe_w_gate, moe_w_up, moe_w_down, final_norm_g):
    bsz, seq, d = x.shape
    depth = w_in.shape[0]
    d_conv = conv_dw_w.shape[-1]
    d_ssm = ssm_d.shape[-1]
    n = bsz * seq
    bf = lambda a: a.astype(BF16)
    x2 = x.reshape(n, d).astype(F32)
    m_a, m_b, coef = _s5_matrices(ssm_a_re, ssm_a_im, ssm_log_dt, ssm_b_re, ssm_b_im,
                                  ssm_c_re, ssm_c_im, ssm_d)
    for layer in range(depth):
        last = layer == depth - 1
        ch, u, gates = _inproj(x2, norm_mix_g[layer], bf(w_in[layer]), bsz, d_conv, d_ssm)
        hc = _conv_branch(ch.reshape(bsz, seq, d_conv), conv_dw_w[layer], conv_dw_b[layer],
                          conv_ln_g[layer], conv_ln_b[layer]).reshape(n, d_conv)
        y = _s5_branch(u, m_a[layer], m_b[layer], coef[layer])
        i = layer // 2
        moe = layer % 2 == 1
        router = (router_w[i], router_b[i]) if moe else None
        outs = _merge(x2, hc, y, gates, bf(ssm_w_glu[layer]), ssm_b_glu[layer],
                      bf(w_ssm_proj[layer]), bf(w_conv_proj[layer]), bf(w_out[layer]),
                      norm_ffn_g[layer], router)
        if moe:
            x2, h, route = outs
            x2 = _moe(x2, h, route, bf(moe_w_gate[i]), bf(moe_w_up[i]), bf(moe_w_down[i]),
                      final_norm_g, last)
        else:
            x2, h = outs
            x2 = _ffn_dense(x2, h, bf(ffn_w_gate[i]), bf(ffn_w_up[i]), bf(ffn_w_down[i]),
                            final_norm_g, last)
    return x2.reshape(bsz, seq, d).astype(x.dtype)
```

```python
import functools
import math

import jax
import jax.numpy as jnp
from jax import lax
from jax.experimental import pallas as pl
from jax.experimental.pallas import tpu as pltpu

F32 = jnp.float32
BF16 = jnp.bfloat16

RMS_EPS = 1e-6
LN_EPS = 1e-5
CONV_WIDTH = 31
CONV_HALF = CONV_WIDTH // 2
SSM_GROUP = 16
SSM_STATE = 64
CHUNK = 16
TOP_K = 2
LANES = 128
SUBLANES = 8
VMEM_LIMIT = 56 * 1024 * 1024

ROW_TILE = 512
MOE_TILE = 256
PERM_TILE = 256
CONV_TILE = 128
FF_CHUNK = 256


def _params(*sem):
    return pltpu.CompilerParams(dimension_semantics=sem, vmem_limit_bytes=VMEM_LIMIT)


def _rms(xf, g):
    ms = jnp.mean(xf * xf, axis=-1, keepdims=True)
    return xf * lax.rsqrt(ms + RMS_EPS) * g


def _inproj_kernel(x_ref, g_ref, w_ref, ch_ref, u_ref, gate_ref):
    dc = ch_ref.shape[-1]
    ds_ = u_ref.shape[0] * LANES
    h = _rms(x_ref[...], g_ref[...]).astype(BF16)
    v = jnp.dot(h, w_ref[:, 0:dc], preferred_element_type=F32)
    gt = jnp.dot(h, w_ref[:, dc:2 * dc], preferred_element_type=F32)
    ch_ref[...] = v * jax.nn.sigmoid(gt)
    u = jnp.dot(h, w_ref[:, 2 * dc:2 * dc + ds_], preferred_element_type=F32)
    for q in range(u_ref.shape[0]):
        u_ref[q, 0] = u[:, q * LANES:(q + 1) * LANES]
    gate_ref[...] = jnp.dot(h, w_ref[:, 2 * dc + ds_:], preferred_element_type=F32).astype(BF16)


def _inproj(x2, g, w_bf, bsz, d_conv, d_ssm):
    n, d = x2.shape
    seq = n // bsz
    d_in = w_bf.shape[1]
    d_gate = d_in - 2 * d_conv - d_ssm
    tm = ROW_TILE
    per_seq = seq // tm
    nq = d_ssm // LANES
    return pl.pallas_call(
        _inproj_kernel,
        grid=(n // tm,),
        in_specs=[
            pl.BlockSpec((tm, d), lambda i: (i, 0)),
            pl.BlockSpec((1, d), lambda i: (0, 0)),
            pl.BlockSpec((d, d_in), lambda i: (0, 0)),
        ],
        out_specs=[
            pl.BlockSpec((tm, d_conv), lambda i: (i, 0)),
            pl.BlockSpec((nq, 1, tm, LANES), lambda i: (0, i // per_seq, i % per_seq, 0)),
            pl.BlockSpec((tm, d_gate), lambda i: (i, 0)),
        ],
        out_shape=[
            jax.ShapeDtypeStruct((n, d_conv), F32),
            jax.ShapeDtypeStruct((nq, bsz, seq, LANES), F32),
            jax.ShapeDtypeStruct((n, d_gate), BF16),
        ],
        compiler_params=_params("parallel"),
        name="inproj",
    )(x2, g.reshape(1, d), w_bf)


def _conv_kernel(h_ref, w_ref, b_ref, lg_ref, lb_ref, o_ref, pad_scr, tile_scr):
    seq, c = h_ref.shape[1], h_ref.shape[2]
    front = 2 * SUBLANES
    zeros = jnp.zeros((front, c), F32)
    pad_scr[0:front, :] = zeros
    pad_scr[front + seq:front + seq + front, :] = zeros
    pad_scr[front:front + seq, :] = h_ref[0]

    phases = [[] for _ in range(SUBLANES)]
    for k in range(CONV_WIDTH):
        off = k - CONV_HALF + front
        phases[off % SUBLANES].append((k, off - off % SUBLANES))

    def tile_body(ti, carry):
        base = pl.multiple_of(ti * CONV_TILE, CONV_TILE)
        win_ref = pad_scr.at[pl.ds(base, CONV_TILE + 2 * front), :]
        for lb in range(c // LANES):
            ls = slice(lb * LANES, (lb + 1) * LANES)
            acc = b_ref[:, ls]
            for s, taps in enumerate(phases):
                part = None
                for k, row0 in taps:
                    term = w_ref[k:k + 1, ls] * win_ref[pl.ds(row0, CONV_TILE + SUBLANES), ls]
                    part = term if part is None else part + term
                if part is not None:
                    acc = acc + part[s:s + CONV_TILE, :]
            tile_scr[:, ls] = acc
        hh = tile_scr[...]
        mu = jnp.mean(hh, axis=-1, keepdims=True)
        cen = hh - mu
        var = jnp.mean(cen * cen, axis=-1, keepdims=True)
        y = cen * lax.rsqrt(var + LN_EPS) * lg_ref[...] + lb_ref[...]
        o_ref[0, pl.ds(base, CONV_TILE), :] = (y * jax.nn.sigmoid(y)).astype(BF16)
        return carry

    lax.fori_loop(0, seq // CONV_TILE, tile_body, 0)


def _conv_branch(ch, dw_w, dw_b, ln_g, ln_b):
    b, seq, c = ch.shape
    front = 2 * SUBLANES
    vec = lambda a: a.reshape(1, c)
    return pl.pallas_call(
        _conv_kernel,
        grid=(b,),
        in_specs=[
            pl.BlockSpec((1, seq, c), lambda i: (i, 0, 0)),
            pl.BlockSpec((CONV_WIDTH, c), lambda i: (0, 0)),
            pl.BlockSpec((1, c), lambda i: (0, 0)),
            pl.BlockSpec((1, c), lambda i: (0, 0)),
            pl.BlockSpec((1, c), lambda i: (0, 0)),
        ],
        out_specs=pl.BlockSpec((1, seq, c), lambda i: (i, 0, 0)),
        out_shape=jax.ShapeDtypeStruct((b, seq, c), BF16),
        scratch_shapes=[
            pltpu.VMEM((seq + 2 * front, c), F32),
            pltpu.VMEM((CONV_TILE, c), F32),
        ],
        compiler_params=_params("parallel"),
        name="conv_branch",
    )(ch, dw_w, vec(dw_b), vec(ln_g), vec(ln_b))


def _s5_matrices(a_re, a_im, log_dt, b_re, b_im, c_re, c_im, d_skip):
    t = CHUNK
    n_layers, n_dir, g, p = a_re.shape
    c = b_re.shape[-1]
    f = lambda a: a.astype(F32)
    cat = lambda *xs: jnp.concatenate(xs, axis=-1)
    a_re, a_im, c_re, c_im = f(a_re), f(a_im), f(c_re), f(c_im)
    dt = jnp.exp(f(log_dt))[..., None]
    lam_re, lam_im = a_re * dt, a_im * dt
    up, down = jnp.arange(t, dtype=F32), jnp.arange(t - 1, -1, -1, dtype=F32)
    expo = jnp.stack([cat(down, up + 1.0, up, jnp.full((1,), t, F32)),
                      cat(up, down + 1.0, up, jnp.full((1,), t, F32))])
    expo = expo[None, :, None, :, None]
    mag = jnp.exp(expo * lam_re[:, :, :, None, :])
    pw_re = mag * jnp.cos(expo * lam_im[:, :, :, None, :])
    pw_im = mag * jnp.sin(expo * lam_im[:, :, :, None, :])
    ab_re = jnp.exp(lam_re) * jnp.cos(lam_im)
    ab_im = jnp.exp(lam_re) * jnp.sin(lam_im)
    n_re, n_im = ab_re - 1.0, ab_im
    den = a_re * a_re + a_im * a_im
    q_re = ((n_re * a_re + n_im * a_im) / den)[:, :, :, None, :]
    q_im = ((n_im * a_re - n_re * a_im) / den)[:, :, :, None, :]
    bt_re = jnp.swapaxes(f(b_re), -1, -2)[:, None]
    bt_im = jnp.swapaxes(f(b_im), -1, -2)[:, None]
    bb_re = q_re * bt_re - q_im * bt_im
    bb_im = q_re * bt_im + q_im * bt_re

    def powers(lo, hi_, reps):
        sl = lambda w: jnp.tile(w[:, :, :, lo:hi_], (1, 1, 1, 1, reps))[:, :, :, :, None, :]
        return sl(pw_re), sl(pw_im)

    kr, ki = powers(0, t, 4)
    z = (kr * cat(bb_re, bb_im, bb_im, bb_re)[:, :, :, None]
         + ki * cat(-bb_im, bb_re, bb_re, -bb_im)[:, :, :, None])
    m_a = jnp.swapaxes(z.reshape(n_layers, n_dir, g, t * c, 4 * p), 1, 2).astype(BF16)
    kr, ki = powers(t, 2 * t, 2)
    v = (kr * cat(c_re, -c_im)[:, :, :, None] + ki * cat(-c_im, -c_re)[:, :, :, None])
    m_o = jnp.swapaxes(v.reshape(n_layers, n_dir, g, t * c, 2 * p), -1, -2)
    pr = jnp.swapaxes(pw_re[:, :, :, 2 * t:3 * t], -1, -2)[..., None]
    pi = jnp.swapaxes(pw_im[:, :, :, 2 * t:3 * t], -1, -2)[..., None]
    cp_re = jnp.swapaxes(c_re, -1, -2)[:, :, :, :, None, :]
    cp_im = jnp.swapaxes(c_im, -1, -2)[:, :, :, :, None, :]
    vp_re = (pr * cp_re - pi * cp_im)[:, :, :, :, None]
    vp_im = (pr * cp_im + pi * cp_re)[:, :, :, :, None]
    bp_re = jnp.swapaxes(bb_re, -1, -2)[..., None, None]
    bp_im = jnp.swapaxes(bb_im, -1, -2)[..., None, None]
    kk = jnp.sum(bp_re * vp_re - bp_im * vp_im, axis=3)
    kf, kb = kk[:, 0], kk[:, 1]
    diag = kf[:, :, :, 0] + kb[:, :, :, 0] + (f(d_skip).reshape(n_layers, g, 1, c)
                                              * jnp.eye(c, dtype=F32))
    lags = jnp.concatenate([kb[:, :, :, :0:-1], diag[:, :, :, None], kf[:, :, :, 1:]], axis=3)
    lags = lags.reshape(n_layers, g, c, (2 * t - 1) * c)
    m_intra = jnp.stack([lags[..., (t - 1 - j) * c:(2 * t - 1 - j) * c] for j in range(t)], axis=2)
    m_b = jnp.concatenate([m_intra.reshape(n_layers, g, t * c, t * c), m_o[:, 0], m_o[:, 1]],
                          axis=2).astype(BF16)
    at_re, at_im = pw_re[:, :, :, 3 * t], pw_im[:, :, :, 3 * t]
    rows = [cat(at_re[:, 0], at_re[:, 0]), cat(-at_im[:, 0], at_im[:, 0]),
            cat(at_re[:, 1], at_re[:, 1]), cat(-at_im[:, 1], at_im[:, 1])]
    coef = jnp.stack(rows + [jnp.zeros_like(rows[0])] * (SUBLANES - len(rows)), axis=2)
    return m_a, m_b, coef


def _block_transpose(v):
    nblk = len(v)
    blk = lax.broadcasted_iota(jnp.int32, v[0].shape, 1) // SSM_GROUP
    d = nblk // 2
    while d >= 1:
        keep_lo = (blk & d) == 0
        out = list(v)
        for r in range(nblk):
            if r & d:
                continue
            lo, hi = v[r], v[r + d]
            out[r] = jnp.where(keep_lo, lo, pltpu.roll(hi, d * SSM_GROUP, 1))
            out[r + d] = jnp.where(keep_lo, pltpu.roll(lo, LANES - d * SSM_GROUP, 1), hi)
        v = out
        d //= 2
    return v


def _s5_kernel(u_ref, ma_ref, mb_ref, coef_ref, y_ref, x_scr, z_scr, cin_scr):
    nb, seq = u_ref.shape[1], u_ref.shape[2]
    gpl, halves, rows = x_scr.shape[0:3]
    kx = halves * LANES
    sw = coef_ref.shape[-1]
    n_chunks = seq // CHUNK
    cpt = SUBLANES // nb
    n_tiles = rows // SUBLANES
    span = SUBLANES * CHUNK
    pos_per_half = LANES // SSM_GROUP

    def relayout(co, to_chunks):
        tok0 = pl.multiple_of(co * span, span)
        row0 = pl.multiple_of(co * SUBLANES * nb, SUBLANES * nb)
        for b in range(nb):
            tok_ref = (u_ref if to_chunks else y_ref).at[0, b, pl.ds(tok0, span), :]
            for h in range(halves):
                chunk_refs = [x_scr.at[gp, h, pl.ds(row0, SUBLANES * nb), :] for gp in range(gpl)]
                tok_rows = [pl.ds(h * pos_per_half + ip, SUBLANES, stride=CHUNK)
                            for ip in range(pos_per_half)]
                chunk_rows = pl.ds(b, SUBLANES, stride=nb)
                if to_chunks:
                    dst = _block_transpose([tok_ref[r, :] for r in tok_rows])
                    for gp in range(gpl):
                        chunk_refs[gp][chunk_rows, :] = dst[gp]
                else:
                    dst = _block_transpose([chunk_refs[gp][chunk_rows, :] for gp in range(gpl)])
                    for ip in range(pos_per_half):
                        tok_ref[tok_rows[ip], :] = dst[ip]

    def relayout_in(co, carry):
        relayout(co, True)
        return carry

    def relayout_out(co, carry):
        relayout(co, False)
        return carry

    lax.fori_loop(0, n_chunks // SUBLANES, relayout_in, 0)

    rc = min(rows, 512)
    row_id = lax.broadcasted_iota(jnp.int32, (SUBLANES, sw), 0)
    x_rows = lambda gp, r0: jnp.concatenate(
        [x_scr[gp, h, r0:r0 + rc, :] for h in range(halves)], axis=-1).astype(BF16)
    for gp in range(gpl):
        for r0 in range(0, rows, rc):
            xr = x_rows(gp, r0)
            for dr in range(ma_ref.shape[1]):
                z_scr[r0:r0 + rc, 2 * dr * sw:2 * (dr + 1) * sw] = jnp.dot(
                    xr, ma_ref[gp, dr], preferred_element_type=F32)
        coef = coef_ref[gp]
        bc = lambda r: jnp.broadcast_to(coef[r:r + 1, :], (SUBLANES, sw))
        af1, af2, ab1, ab2 = bc(0), bc(1), bc(2), bc(3)

        def tile_step(ti, carry):
            sf, sfp, sb, sbp = carry
            rf = pl.multiple_of(ti * SUBLANES, SUBLANES)
            rb = pl.multiple_of((n_tiles - 1 - ti) * SUBLANES, SUBLANES)
            zf = z_scr[pl.ds(rf, SUBLANES), 0:sw]
            zfp = z_scr[pl.ds(rf, SUBLANES), sw:2 * sw]
            zb = z_scr[pl.ds(rb, SUBLANES), 2 * sw:3 * sw]
            zbp = z_scr[pl.ds(rb, SUBLANES), 3 * sw:4 * sw]
            cin_f, cin_b = sf, sb
            for k in range(cpt):
                nf, nfp = af1 * sf + af2 * sfp + zf, af1 * sfp - af2 * sf + zfp
                sf, sfp = pltpu.roll(nf, nb, 0), pltpu.roll(nfp, nb, 0)
                kb = cpt - 1 - k
                nbk, nbkp = ab1 * sb + ab2 * sbp + zb, ab1 * sbp - ab2 * sb + zbp
                sb, sbp = pltpu.roll(nbk, SUBLANES - nb, 0), pltpu.roll(nbkp, SUBLANES - nb, 0)
                if k < cpt - 1:
                    cin_f = jnp.where(row_id >= (k + 1) * nb, sf, cin_f)
                    cin_b = jnp.where(row_id < kb * nb, sb, cin_b)
            cin_scr[pl.ds(rf, SUBLANES), 0:sw] = cin_f
            cin_scr[pl.ds(rb, SUBLANES), sw:2 * sw] = cin_b
            return sf, sfp, sb, sbp

        zero = jnp.zeros((SUBLANES, sw), F32)
        lax.fori_loop(0, n_tiles, tile_step, (zero, zero, zero, zero), unroll=2)

        for r0 in range(0, rows, rc):
            y = (jnp.dot(x_rows(gp, r0), mb_ref[gp, 0:kx, :], preferred_element_type=F32)
                 + jnp.dot(cin_scr[r0:r0 + rc, :].astype(BF16), mb_ref[gp, kx:, :],
                           preferred_element_type=F32))
            for h in range(halves):
                x_scr[gp, h, r0:r0 + rc, :] = y[:, h * LANES:(h + 1) * LANES]

    lax.fori_loop(0, n_chunks // SUBLANES, relayout_out, 0)


def _s5_branch(u, m_a, m_b, coef):
    nq, b, seq, _ = u.shape
    g = m_a.shape[0]
    gpl = g // nq
    n_chunks = seq // CHUNK
    assert SUBLANES % b == 0 and n_chunks % SUBLANES == 0 and gpl * SSM_GROUP == LANES
    rows = n_chunks * b
    n_dir, kx, kb = m_a.shape[1], m_a.shape[2], m_b.shape[1]
    nz = n_dir * m_a.shape[3]
    return pl.pallas_call(
        _s5_kernel,
        grid=(nq,),
        in_specs=[
            pl.BlockSpec((1, b, seq, LANES), lambda i: (i, 0, 0, 0), pipeline_mode=pl.Buffered(1)),
            pl.BlockSpec((gpl, n_dir, kx, nz // n_dir), lambda i: (i, 0, 0, 0)),
            pl.BlockSpec((gpl, kb, kx), lambda i: (i, 0, 0)),
            pl.BlockSpec((gpl, SUBLANES, coef.shape[-1]), lambda i: (i, 0, 0)),
        ],
        out_specs=pl.BlockSpec((1, b, seq, LANES), lambda i: (i, 0, 0, 0)),
        out_shape=jax.ShapeDtypeStruct((nq, b, seq, LANES), F32),
        scratch_shapes=[
            pltpu.VMEM((gpl, kx // LANES, rows, LANES), F32),
            pltpu.VMEM((rows, nz), F32),
            pltpu.VMEM((rows, kb - kx), F32),
        ],
        compiler_params=_params("parallel"),
        name="s5_scan",
    )(u, m_a, m_b, coef)


def _merge_kernel(x_ref, hc_ref, y_ref, gate_ref, wglu_ref, bglu_ref, ws_ref, wc_ref,
                  wout_ref, gn_ref, *rest, moe):
    if moe:
        rw_ref, rb_ref, xo_ref, h_ref, route_ref = rest
    else:
        xo_ref, h_ref = rest
    d = x_ref.shape[-1]
    yg = jax.nn.gelu(jnp.concatenate([y_ref[q, 0] for q in range(y_ref.shape[0])], axis=-1))
    glu = jnp.dot(yg.astype(BF16), wglu_ref[...], preferred_element_type=F32) + bglu_ref[...]
    y2 = yg * jax.nn.sigmoid(glu)
    br_s = jnp.dot(y2.astype(BF16), ws_ref[...], preferred_element_type=F32)
    br_c = jnp.dot(hc_ref[...], wc_ref[...], preferred_element_type=F32)
    gc = gate_ref[:, 0:d].astype(F32)
    gs = gate_ref[:, d:2 * d].astype(F32)
    merged = jax.nn.sigmoid(gc) * br_c + jax.nn.sigmoid(gs) * br_s
    xn = x_ref[...] + jnp.dot(merged.astype(BF16), wout_ref[...], preferred_element_type=F32)
    xo_ref[...] = xn
    h = _rms(xn, gn_ref[...])
    if not moe:
        h_ref[...] = h.astype(h_ref.dtype)
    if moe:
        _rows_to_tiles(h_ref, h)
        h_hi = h.astype(BF16)
        h_lo = (h - h_hi.astype(F32)).astype(BF16)
        logits = (jnp.dot(h_hi, rw_ref[0], preferred_element_type=F32)
                  + jnp.dot(h_hi, rw_ref[1], preferred_element_type=F32)
                  + jnp.dot(h_lo, rw_ref[0], preferred_element_type=F32)) + rb_ref[1:2, :]
        lane = lax.broadcasted_iota(jnp.int32, logits.shape, 1).astype(F32)
        valid = rb_ref[0:1, :] > 0.0
        neg = jnp.float32(-jnp.inf)
        logits = jnp.where(valid, logits, neg)
        m1 = jnp.max(logits, axis=-1, keepdims=True)
        i1 = jnp.min(jnp.where(logits == m1, lane, float(LANES)), axis=-1, keepdims=True)
        rest_l = jnp.where(lane == i1, neg, logits)
        m2 = jnp.max(rest_l, axis=-1, keepdims=True)
        i2 = jnp.min(jnp.where(rest_l == m2, lane, float(LANES)), axis=-1, keepdims=True)
        e2 = jnp.exp(m2 - m1)
        w1 = 1.0 / (1.0 + e2)
        w2 = e2 / (1.0 + e2)
        route_ref[...] = jnp.where(lane == 0.0, i1, jnp.where(lane == 1.0, i2,
                                   jnp.where(lane == 2.0, w1, jnp.where(lane == 3.0, w2, 0.0))))


def _merge(x2, hc, y, gates, wglu, bglu, ws, wc, wout, gn, router=None):
    n, d = x2.shape
    ds_ = hc.shape[1]
    nq, _, seq, _ = y.shape
    tm = ROW_TILE
    per_seq = seq // tm
    moe = router is not None
    row = lambda w: pl.BlockSpec((tm, w), lambda i: (i, 0))
    full = lambda a: pl.BlockSpec(a.shape, lambda i: (0,) * a.ndim)
    y_spec = pl.BlockSpec((nq, 1, tm, LANES), lambda i: (0, i // per_seq, i % per_seq, 0))
    ins = [x2, hc, y, gates, wglu, bglu.reshape(1, ds_), ws, wc, wout, gn.reshape(1, d)]
    in_specs = [row(d), row(ds_), y_spec, row(gates.shape[1])] + [full(a) for a in ins[4:]]
    out_specs = [row(d), row(d)]
    out_shape = [jax.ShapeDtypeStruct((n, d), F32), jax.ShapeDtypeStruct((n, d), BF16)]
    if moe:
        tr = d // LANES
        out_specs[1] = pl.BlockSpec((tm * tr, LANES), lambda i: (i, 0))
        out_shape[1] = jax.ShapeDtypeStruct((n * tr, LANES), F32)
        rw, rb = router
        n_exp = rw.shape[1]
        rw_f = jnp.zeros((d, LANES), F32).at[:, :n_exp].set(rw.astype(F32))
        rw_hi = rw_f.astype(BF16)
        rw_pad = jnp.stack([rw_hi, (rw_f - rw_hi.astype(F32)).astype(BF16)])
        rb_pad = jnp.zeros((SUBLANES, LANES), F32).at[0, :n_exp].set(1.0).at[1, :n_exp].set(rb.astype(F32))
        ins += [rw_pad, rb_pad]
        in_specs += [full(rw_pad), full(rb_pad)]
        out_specs.append(row(LANES))
        out_shape.append(jax.ShapeDtypeStruct((n, LANES), F32))
    return pl.pallas_call(
        functools.partial(_merge_kernel, moe=moe),
        grid=(n // tm,),
        in_specs=in_specs,
        out_specs=out_specs,
        out_shape=out_shape,
        compiler_params=_params("parallel"),
        name="merge_moe" if moe else "merge",
    )(*ins)


def _swiglu_rows(h_bf, wg_ref, wu_ref, wd_ref, a_scr):
    d_ff = a_scr.shape[1]
    for c0 in range(0, d_ff, FF_CHUNK):
        cs = slice(c0, c0 + FF_CHUNK)
        gt = jnp.dot(h_bf, wg_ref[:, cs], preferred_element_type=F32)
        up = jnp.dot(h_bf, wu_ref[:, cs], preferred_element_type=F32)
        a_scr[:, cs] = (gt * jax.nn.sigmoid(gt) * up).astype(BF16)
    return jnp.dot(a_scr[...], wd_ref[...], preferred_element_type=F32)


def _ffn_kernel(x_ref, h_ref, wg_ref, wu_ref, wd_ref, gf_ref, o_ref, a_scr, *, final_norm):
    out = x_ref[...] + _swiglu_rows(h_ref[...], wg_ref, wu_ref, wd_ref, a_scr)
    if final_norm:
        out = _rms(out, gf_ref[...])
    o_ref[...] = out


def _ffn_dense(x2, h, wg, wu, wd, gf, final_norm):
    n, d = x2.shape
    d_ff = wg.shape[1]
    tm = ROW_TILE
    once = pl.Buffered(1)
    return pl.pallas_call(
        functools.partial(_ffn_kernel, final_norm=final_norm),
        grid=(n // tm,),
        in_specs=[
            pl.BlockSpec((tm, d), lambda i: (i, 0)),
            pl.BlockSpec((tm, d), lambda i: (i, 0)),
            pl.BlockSpec((d, d_ff), lambda i: (0, 0), pipeline_mode=once),
            pl.BlockSpec((d, d_ff), lambda i: (0, 0), pipeline_mode=once),
            pl.BlockSpec((d_ff, d), lambda i: (0, 0), pipeline_mode=once),
            pl.BlockSpec((1, d), lambda i: (0, 0)),
        ],
        out_specs=pl.BlockSpec((tm, d), lambda i: (i, 0)),
        out_shape=jax.ShapeDtypeStruct((n, d), F32),
        scratch_shapes=[pltpu.VMEM((tm, d_ff), BF16)],
        compiler_params=_params("parallel"),
        name="ffn_dense",
    )(x2, h, wg, wu, wd, gf.reshape(1, d))


TOKEN_ROWS = SUBLANES
PERM_UNROLL = 4


def _rows_to_tiles(ref, val):
    m = val.shape[0]
    for j in range(TOKEN_ROWS):
        ref[pl.ds(j, m, stride=TOKEN_ROWS), :] = val[:, j * LANES:(j + 1) * LANES]


def _tiles_to_rows(ref):
    m = ref.shape[0] // TOKEN_ROWS
    return jnp.concatenate([ref[pl.ds(j, m, stride=TOKEN_ROWS), :] for j in range(TOKEN_ROWS)],
                           axis=-1)


def _token_copy(src, dst, sem, s, t):
    rows = lambda i: pl.ds(pl.multiple_of(i * TOKEN_ROWS, TOKEN_ROWS), TOKEN_ROWS)
    return pltpu.make_async_copy(src.at[rows(s), :], dst.at[rows(t), :], sem)


def _for_tokens(n_tok, fn):
    def body(i, carry):
        for j in range(PERM_UNROLL):
            fn(i * PERM_UNROLL + j, j % 2)
        return carry
    lax.fori_loop(0, n_tok // PERM_UNROLL, body, 0)


def _dispatch_kernel(zb_ref, nu_ref, p0_ref, p1_ref, h_ref, xs_ref, zero_scr, sem):
    n_tok = h_ref.shape[0] // TOKEN_ROWS
    n_exp = zb_ref.shape[0]

    @pl.when(pl.program_id(0) == 0)
    def _():
        zero_scr[...] = jnp.zeros(zero_scr.shape, F32)
        rows = zero_scr.shape[0]

        def fill(blk):
            start = pl.multiple_of(blk * rows, rows)
            return pltpu.make_async_copy(zero_scr, xs_ref.at[pl.ds(start, rows), :], sem)

        for e in range(n_exp):
            fill(zb_ref[e]).start()
        for e in range(n_exp):
            fill(zb_ref[e]).wait()

        def fill_unused(blk, carry):
            fill(blk).start()
            fill(blk).wait()
            return carry

        lax.fori_loop(nu_ref[0], xs_ref.shape[0] // rows, fill_unused, 0)

    def copies(r):
        return (_token_copy(h_ref, xs_ref, sem, r, p0_ref[0, 0, r]),
                _token_copy(h_ref, xs_ref, sem, r, p1_ref[0, 0, r]))

    _for_tokens(n_tok, lambda r, lane: [cp.start(priority=lane) for cp in copies(r)])
    _for_tokens(n_tok, lambda r, lane: [cp.wait() for cp in copies(r)])


def _dispatch(h, pos0, pos1, zero_blocks, n_used, n_sorted):
    n = h.shape[0] // TOKEN_ROWS
    tp = PERM_TILE
    smem_rows = lambda: pl.BlockSpec((1, 1, tp), lambda i, zb, nu: (i, 0, 0),
                                     memory_space=pltpu.SMEM)
    grid_spec = pltpu.PrefetchScalarGridSpec(
        num_scalar_prefetch=2,
        grid=(n // tp,),
        in_specs=[smem_rows(), smem_rows(),
                  pl.BlockSpec((tp * TOKEN_ROWS, LANES), lambda i, zb, nu: (i, 0))],
        out_specs=pl.BlockSpec(memory_space=pl.ANY),
        scratch_shapes=[pltpu.VMEM((MOE_TILE * TOKEN_ROWS, LANES), F32),
                        pltpu.SemaphoreType.DMA(())],
    )
    return pl.pallas_call(
        _dispatch_kernel,
        grid_spec=grid_spec,
        out_shape=jax.ShapeDtypeStruct((n_sorted * TOKEN_ROWS, LANES), F32),
        compiler_params=_params("arbitrary"),
        name="moe_dispatch",
    )(zero_blocks, n_used, pos0.reshape(n // tp, 1, tp), pos1.reshape(n // tp, 1, tp), h)


def _expert_kernel(be_ref, nu_ref, xs_ref, wg_ref, wu_ref, wd_ref, ys_ref, a_scr):
    @pl.when(pl.program_id(0) < nu_ref[0])
    def _():
        _rows_to_tiles(ys_ref, _swiglu_rows(_tiles_to_rows(xs_ref).astype(BF16), wg_ref.at[0],
                                            wu_ref.at[0], wd_ref.at[0], a_scr))

    @pl.when(pl.program_id(0) >= nu_ref[0])
    def _():
        ys_ref[...] = jnp.zeros(ys_ref.shape, F32)


def _experts(xs, wg, wu, wd, block_expert, n_used):
    _, d, d_ff = wg.shape
    tm = MOE_TILE
    tile = (tm * TOKEN_ROWS, LANES)
    grid_spec = pltpu.PrefetchScalarGridSpec(
        num_scalar_prefetch=2,
        grid=(xs.shape[0] // tile[0],),
        in_specs=[
            pl.BlockSpec(tile, lambda i, be, nu: (jnp.minimum(i, nu[0] - 1), 0)),
            pl.BlockSpec((1, d, d_ff), lambda i, be, nu: (be[i], 0, 0)),
            pl.BlockSpec((1, d, d_ff), lambda i, be, nu: (be[i], 0, 0)),
            pl.BlockSpec((1, d_ff, d), lambda i, be, nu: (be[i], 0, 0)),
        ],
        out_specs=pl.BlockSpec(tile, lambda i, be, nu: (i, 0)),
        scratch_shapes=[pltpu.VMEM((tm, d_ff), BF16)],
    )
    return pl.pallas_call(
        _expert_kernel,
        grid_spec=grid_spec,
        out_shape=jax.ShapeDtypeStruct(xs.shape, F32),
        compiler_params=_params("arbitrary"),
        name="moe_experts",
    )(block_expert, n_used, xs, wg, wu, wd)


def _combine_kernel(p0_ref, p1_ref, q0_ref, q1_ref, x_ref, route_ref, gf_ref, ys_ref, o_ref,
                    g_scr, sem, *, final_norm):
    n_tok = x_ref.shape[0]
    step = pl.program_id(0)
    slot = lax.rem(step, 2)

    def copies(pa_ref, pb_ref, sl):
        return lambda r: (
            _token_copy(ys_ref, g_scr.at[sl, 0], sem.at[sl], pa_ref[0, 0, r], r),
            _token_copy(ys_ref, g_scr.at[sl, 1], sem.at[sl], pb_ref[0, 0, r], r))

    def start(cps):
        _for_tokens(n_tok, lambda r, lane: [cp.start(priority=lane) for cp in cps(r)])

    @pl.when(step == 0)
    def _():
        start(copies(p0_ref, p1_ref, 0))

    @pl.when(step + 1 < pl.num_programs(0))
    def _():
        start(copies(q0_ref, q1_ref, 1 - slot))

    mine = copies(p0_ref, p1_ref, slot)
    _for_tokens(n_tok, lambda r, lane: [cp.wait() for cp in mine(r)])
    w1 = route_ref[:, 2:3]
    w2 = route_ref[:, 3:4]
    out = (x_ref[...] + w1 * _tiles_to_rows(g_scr.at[slot, 0])
           + w2 * _tiles_to_rows(g_scr.at[slot, 1]))
    if final_norm:
        out = _rms(out, gf_ref[...])
    o_ref[...] = out


def _combine(x2, route, ys, pos0, pos1, gf, final_norm):
    n, d = x2.shape
    tp = PERM_TILE
    n_steps = n // tp
    cur = lambda: pl.BlockSpec((1, 1, tp), lambda i: (i, 0, 0), memory_space=pltpu.SMEM)
    nxt = lambda: pl.BlockSpec((1, 1, tp), lambda i: (jnp.minimum(i + 1, n_steps - 1), 0, 0),
                               memory_space=pltpu.SMEM)
    pos0, pos1 = pos0.reshape(n_steps, 1, tp), pos1.reshape(n_steps, 1, tp)
    return pl.pallas_call(
        functools.partial(_combine_kernel, final_norm=final_norm),
        grid=(n_steps,),
        in_specs=[cur(), cur(), nxt(), nxt(),
                  pl.BlockSpec((tp, d), lambda i: (i, 0)),
                  pl.BlockSpec((tp, LANES), lambda i: (i, 0)),
                  pl.BlockSpec((1, d), lambda i: (0, 0)),
                  pl.BlockSpec(memory_space=pl.ANY)],
        out_specs=pl.BlockSpec((tp, d), lambda i: (i, 0)),
        out_shape=jax.ShapeDtypeStruct((n, d), F32),
        scratch_shapes=[pltpu.VMEM((2, 2, tp * TOKEN_ROWS, LANES), F32),
                        pltpu.SemaphoreType.DMA((2,))],
        compiler_params=_params("arbitrary"),
        name="moe_combine",
    )(pos0, pos1, pos0, pos1, x2, route, gf.reshape(1, d), ys)


def _routing_tables(route, n_exp):
    n = route.shape[0]
    tm = MOE_TILE
    i1 = route[:, 0].astype(jnp.int32)
    i2 = route[:, 1].astype(jnp.int32)
    oh1 = jax.nn.one_hot(i1, n_exp, dtype=jnp.int32)
    oh2 = jax.nn.one_hot(i2, n_exp, dtype=jnp.int32)
    tot = oh1 + oh2
    csum = jnp.cumsum(tot, axis=0)
    excl = csum - tot
    rank1 = jnp.sum(excl * oh1, axis=-1)
    rank2 = jnp.sum((excl + oh1) * oh2, axis=-1)
    counts = csum[-1]
    nblk = (counts + tm - 1) // tm
    blk_end = jnp.cumsum(nblk)
    blk_off = blk_end - nblk
    row_off = blk_off * tm
    pos0 = row_off[i1] + rank1
    pos1 = row_off[i2] + rank2
    n_blocks = TOP_K * n // tm + n_exp
    n_used = blk_end[-1]
    jb = jnp.minimum(jnp.arange(n_blocks, dtype=jnp.int32), n_used - 1)
    block_expert = jnp.sum((blk_end[None, :] <= jb[:, None]).astype(jnp.int32), axis=-1)
    block_expert = jnp.minimum(block_expert, n_exp - 1)
    zero_blocks = jnp.where(nblk > 0, blk_end - 1, n_blocks - 1).astype(jnp.int32)
    return (pos0.astype(jnp.int32), pos1.astype(jnp.int32), block_expert,
            n_used.astype(jnp.int32).reshape(1), zero_blocks, n_blocks * tm)


def _moe(x2, h, route, wg, wu, wd, gf, final_norm):
    n_exp = wg.shape[0]
    pos0, pos1, block_expert, n_used, zero_blocks, n_sorted = _routing_tables(route, n_exp)
    xs = _dispatch(h, pos0, pos1, zero_blocks, n_used, n_sorted)
    ys = _experts(xs, wg, wu, wd, block_expert, n_used)
    return _combine(x2, route, ys, pos0, pos1, gf, final_norm)


def kernel(x, norm_mix_g, w_in, conv_dw_w, conv_dw_b, conv_ln_g, conv_ln_b, w_conv_proj,
           ssm_a_re, ssm_a_im, ssm_log_dt, ssm_b_re, ssm_b_im, ssm_c_re, ssm_c_im, ssm_d,
           ssm_w_glu, ssm_b_glu, w_ssm_proj, w_out, norm_ffn_g,
           ffn_w_gate, ffn_w_up, ffn_w_down, router_w, router_b,
           moe_w_gate, moe_w_up, moe_w_down, final_norm_g):
    bsz, seq, d = x.shape
    depth = w_in.shape[0]
    d_conv = conv_dw_w.shape[-1]
    d_ssm = ssm_d.shape[-1]
    n = bsz * seq
    bf = lambda a: a.astype(BF16)
    x2 = x.reshape(n, d).astype(F32)
    m_a, m_b, coef = _s5_matrices(ssm_a_re, ssm_a_im, ssm_log_dt, ssm_b_re, ssm_b_im,
                                  ssm_c_re, ssm_c_im, ssm_d)
    for layer in range(depth):
        last = layer == depth - 1
        ch, u, gates = _inproj(x2, norm_mix_g[layer], bf(w_in[layer]), bsz, d_conv, d_ssm)
        hc = _conv_branch(ch.reshape(bsz, seq, d_conv), conv_dw_w[layer], conv_dw_b[layer],
                          conv_ln_g[layer], conv_ln_b[layer]).reshape(n, d_conv)
        y = _s5_branch(u, m_a[layer], m_b[layer], coef[layer])
        i = layer // 2
        moe = layer % 2 == 1
        router = (router_w[i], router_b[i]) if moe else None
        outs = _merge(x2, hc, y, gates, bf(ssm_w_glu[layer]), ssm_b_glu[layer],
                      bf(w_ssm_proj[layer]), bf(w_conv_proj[layer]), bf(w_out[layer]),
                      norm_ffn_g[layer], router)
        if moe:
            x2, h, route = outs
            x2 = _moe(x2, h, route, bf(moe_w_gate[i]), bf(moe_w_up[i]), bf(moe_w_down[i]),
                      final_norm_g, last)
        else:
            x2, h = outs
            x2 = _ffn_dense(x2, h, bf(ffn_w_gate[i]), bf(ffn_w_up[i]), bf(ffn_w_down[i]),
                            final_norm_g, last)
    return x2.reshape(bsz, seq, d).astype(x.dtype)
```

```python
import functools
import math

import jax
import jax.numpy as jnp
from jax import lax
from jax.experimental import pallas as pl
from jax.experimental.pallas import tpu as pltpu

F32 = jnp.float32
BF16 = jnp.bfloat16

RMS_EPS = 1e-6
LN_EPS = 1e-5
CONV_WIDTH = 31
CONV_HALF = CONV_WIDTH // 2
SSM_GROUP = 16
SSM_STATE = 64
CHUNK = 16
TOP_K = 2
LANES = 128
SUBLANES = 8
VMEM_LIMIT = 56 * 1024 * 1024

ROW_TILE = 512
MOE_TILE = 256
PERM_TILE = 512
CONV_TILE = 128
FF_CHUNK = 256


def _params(*sem):
    return pltpu.CompilerParams(dimension_semantics=sem, vmem_limit_bytes=VMEM_LIMIT)


def _rms(xf, g):
    ms = jnp.mean(xf * xf, axis=-1, keepdims=True)
    return xf * lax.rsqrt(ms + RMS_EPS) * g


def _stacked(arr, *lead):
    rest = arr.shape[len(lead):]
    return pl.BlockSpec((None,) * len(lead) + rest, lambda *_: tuple(lead) + (0,) * len(rest))


def _rows3(a):
    return a.reshape(a.shape[0], 1, a.shape[1])


def _inproj_kernel(x_ref, g_ref, w_ref, ch_ref, u_ref, gate_ref):
    dc = ch_ref.shape[-1]
    ds_ = u_ref.shape[0] * LANES
    h = _rms(x_ref[...], g_ref[...]).astype(BF16)
    v = jnp.dot(h, w_ref[:, 0:dc], preferred_element_type=F32)
    gt = jnp.dot(h, w_ref[:, dc:2 * dc], preferred_element_type=F32)
    ch_ref[...] = v * jax.nn.sigmoid(gt)
    u = jnp.dot(h, w_ref[:, 2 * dc:2 * dc + ds_], preferred_element_type=F32)
    for q in range(u_ref.shape[0]):
        u_ref[q, 0] = u[:, q * LANES:(q + 1) * LANES]
    gate_ref[...] = jnp.dot(h, w_ref[:, 2 * dc + ds_:], preferred_element_type=F32).astype(BF16)


def _inproj(x2, g, w_bf, layer, bsz, d_conv, d_ssm):
    n, d = x2.shape
    seq = n // bsz
    d_in = w_bf.shape[-1]
    d_gate = d_in - 2 * d_conv - d_ssm
    tm = ROW_TILE
    per_seq = seq // tm
    nq = d_ssm // LANES
    return pl.pallas_call(
        _inproj_kernel,
        grid=(n // tm,),
        in_specs=[
            pl.BlockSpec((tm, d), lambda i: (i, 0)),
            _stacked(g, layer),
            _stacked(w_bf, layer),
        ],
        out_specs=[
            pl.BlockSpec((tm, d_conv), lambda i: (i, 0)),
            pl.BlockSpec((nq, 1, tm, LANES), lambda i: (0, i // per_seq, i % per_seq, 0)),
            pl.BlockSpec((tm, d_gate), lambda i: (i, 0)),
        ],
        out_shape=[
            jax.ShapeDtypeStruct((n, d_conv), F32),
            jax.ShapeDtypeStruct((nq, bsz, seq, LANES), F32),
            jax.ShapeDtypeStruct((n, d_gate), BF16),
        ],
        compiler_params=_params("parallel"),
        name="inproj",
    )(x2, g, w_bf)


def _conv_kernel(h_ref, w_ref, b_ref, lg_ref, lb_ref, o_ref, pad_scr, tile_scr):
    seq, c = h_ref.shape[1], h_ref.shape[2]
    front = 2 * SUBLANES
    zeros = jnp.zeros((front, c), F32)
    pad_scr[0:front, :] = zeros
    pad_scr[front + seq:front + seq + front, :] = zeros
    pad_scr[front:front + seq, :] = h_ref[0]

    phases = [[] for _ in range(SUBLANES)]
    for k in range(CONV_WIDTH):
        off = k - CONV_HALF + front
        phases[off % SUBLANES].append((k, off - off % SUBLANES))

    def tile_body(ti, carry):
        base = pl.multiple_of(ti * CONV_TILE, CONV_TILE)
        win_ref = pad_scr.at[pl.ds(base, CONV_TILE + 2 * front), :]
        for lb in range(c // LANES):
            ls = slice(lb * LANES, (lb + 1) * LANES)
            acc = b_ref[:, ls]
            for s, taps in enumerate(phases):
                part = None
                for k, row0 in taps:
                    term = w_ref[k:k + 1, ls] * win_ref[pl.ds(row0, CONV_TILE + SUBLANES), ls]
                    part = term if part is None else part + term
                if part is not None:
                    acc = acc + part[s:s + CONV_TILE, :]
            tile_scr[:, ls] = acc
        hh = tile_scr[...]
        mu = jnp.mean(hh, axis=-1, keepdims=True)
        cen = hh - mu
        var = jnp.mean(cen * cen, axis=-1, keepdims=True)
        y = cen * lax.rsqrt(var + LN_EPS) * lg_ref[...] + lb_ref[...]
        o_ref[0, pl.ds(base, CONV_TILE), :] = (y * jax.nn.sigmoid(y)).astype(BF16)
        return carry

    lax.fori_loop(0, seq // CONV_TILE, tile_body, 0)


def _conv_branch(ch, dw_w, dw_b, ln_g, ln_b, layer):
    b, seq, c = ch.shape
    front = 2 * SUBLANES
    return pl.pallas_call(
        _conv_kernel,
        grid=(b,),
        in_specs=[
            pl.BlockSpec((1, seq, c), lambda i: (i, 0, 0)),
            _stacked(dw_w, layer), _stacked(dw_b, layer), _stacked(ln_g, layer),
            _stacked(ln_b, layer),
        ],
        out_specs=pl.BlockSpec((1, seq, c), lambda i: (i, 0, 0)),
        out_shape=jax.ShapeDtypeStruct((b, seq, c), BF16),
        scratch_shapes=[
            pltpu.VMEM((seq + 2 * front, c), F32),
            pltpu.VMEM((CONV_TILE, c), F32),
        ],
        compiler_params=_params("parallel"),
        name="conv_branch",
    )(ch, dw_w, dw_b, ln_g, ln_b)


def _s5_matrices(a_re, a_im, log_dt, b_re, b_im, c_re, c_im, d_skip):
    t = CHUNK
    n_layers, n_dir, g, p = a_re.shape
    c = b_re.shape[-1]
    f = lambda a: a.astype(F32)
    cat = lambda *xs: jnp.concatenate(xs, axis=-1)
    a_re, a_im, c_re, c_im = f(a_re), f(a_im), f(c_re), f(c_im)
    dt = jnp.exp(f(log_dt))[..., None]
    lam_re, lam_im = a_re * dt, a_im * dt
    up, down = jnp.arange(t, dtype=F32), jnp.arange(t - 1, -1, -1, dtype=F32)
    expo = jnp.stack([cat(down, up + 1.0, up, jnp.full((1,), t, F32)),
                      cat(up, down + 1.0, up, jnp.full((1,), t, F32))])
    expo = expo[None, :, None, :, None]
    mag = jnp.exp(expo * lam_re[:, :, :, None, :])
    pw_re = mag * jnp.cos(expo * lam_im[:, :, :, None, :])
    pw_im = mag * jnp.sin(expo * lam_im[:, :, :, None, :])
    ab_re = jnp.exp(lam_re) * jnp.cos(lam_im)
    ab_im = jnp.exp(lam_re) * jnp.sin(lam_im)
    n_re, n_im = ab_re - 1.0, ab_im
    den = a_re * a_re + a_im * a_im
    q_re = ((n_re * a_re + n_im * a_im) / den)[:, :, :, None, :]
    q_im = ((n_im * a_re - n_re * a_im) / den)[:, :, :, None, :]
    bt_re = jnp.swapaxes(f(b_re), -1, -2)[:, None]
    bt_im = jnp.swapaxes(f(b_im), -1, -2)[:, None]
    bb_re = q_re * bt_re - q_im * bt_im
    bb_im = q_re * bt_im + q_im * bt_re

    def powers(lo, hi_, reps):
        sl = lambda w: jnp.tile(w[:, :, :, lo:hi_], (1, 1, 1, 1, reps))[:, :, :, :, None, :]
        return sl(pw_re), sl(pw_im)

    kr, ki = powers(0, t, 4)
    z = (kr * cat(bb_re, bb_im, bb_im, bb_re)[:, :, :, None]
         + ki * cat(-bb_im, bb_re, bb_re, -bb_im)[:, :, :, None])
    m_a = jnp.swapaxes(z.reshape(n_layers, n_dir, g, t * c, 4 * p), 1, 2).astype(BF16)
    kr, ki = powers(t, 2 * t, 2)
    v = (kr * cat(c_re, -c_im)[:, :, :, None] + ki * cat(-c_im, -c_re)[:, :, :, None])
    m_o = jnp.swapaxes(v.reshape(n_layers, n_dir, g, t * c, 2 * p), -1, -2)
    pr = jnp.swapaxes(pw_re[:, :, :, 2 * t:3 * t], -1, -2)[..., None]
    pi = jnp.swapaxes(pw_im[:, :, :, 2 * t:3 * t], -1, -2)[..., None]
    cp_re = jnp.swapaxes(c_re, -1, -2)[:, :, :, :, None, :]
    cp_im = jnp.swapaxes(c_im, -1, -2)[:, :, :, :, None, :]
    vp_re = (pr * cp_re - pi * cp_im)[:, :, :, :, None]
    vp_im = (pr * cp_im + pi * cp_re)[:, :, :, :, None]
    bp_re = jnp.swapaxes(bb_re, -1, -2)[..., None, None]
    bp_im = jnp.swapaxes(bb_im, -1, -2)[..., None, None]
    kk = jnp.sum(bp_re * vp_re - bp_im * vp_im, axis=3)
    kf, kb = kk[:, 0], kk[:, 1]
    diag = kf[:, :, :, 0] + kb[:, :, :, 0] + (f(d_skip).reshape(n_layers, g, 1, c)
                                              * jnp.eye(c, dtype=F32))
    lags = jnp.concatenate([kb[:, :, :, :0:-1], diag[:, :, :, None], kf[:, :, :, 1:]], axis=3)
    lags = lags.reshape(n_layers, g, c, (2 * t - 1) * c)
    m_intra = jnp.stack([lags[..., (t - 1 - j) * c:(2 * t - 1 - j) * c] for j in range(t)], axis=2)
    m_b = jnp.concatenate([m_intra.reshape(n_layers, g, t * c, t * c), m_o[:, 0], m_o[:, 1]],
                          axis=2).astype(BF16)
    at_re, at_im = pw_re[:, :, :, 3 * t], pw_im[:, :, :, 3 * t]
    rows = [cat(at_re[:, 0], at_re[:, 0]), cat(-at_im[:, 0], at_im[:, 0]),
            cat(at_re[:, 1], at_re[:, 1]), cat(-at_im[:, 1], at_im[:, 1])]
    coef = jnp.stack(rows + [jnp.zeros_like(rows[0])] * (SUBLANES - len(rows)), axis=2)
    return m_a, m_b, coef


def _block_transpose(v):
    nblk = len(v)
    blk = lax.broadcasted_iota(jnp.int32, v[0].shape, 1) // SSM_GROUP
    d = nblk // 2
    while d >= 1:
        keep_lo = (blk & d) == 0
        out = list(v)
        for r in range(nblk):
            if r & d:
                continue
            lo, hi = v[r], v[r + d]
            out[r] = jnp.where(keep_lo, lo, pltpu.roll(hi, d * SSM_GROUP, 1))
            out[r + d] = jnp.where(keep_lo, pltpu.roll(lo, LANES - d * SSM_GROUP, 1), hi)
        v = out
        d //= 2
    return v


def _s5_kernel(u_ref, ma_ref, mb_ref, coef_ref, y_ref, x_scr, z_scr, cin_scr):
    nb, seq = u_ref.shape[1], u_ref.shape[2]
    gpl, halves, rows = x_scr.shape[0:3]
    kx = halves * LANES
    sw = coef_ref.shape[-1]
    n_chunks = seq // CHUNK
    cpt = SUBLANES // nb
    n_tiles = rows // SUBLANES
    span = SUBLANES * CHUNK
    pos_per_half = LANES // SSM_GROUP

    def relayout(co, to_chunks):
        tok0 = pl.multiple_of(co * span, span)
        row0 = pl.multiple_of(co * SUBLANES * nb, SUBLANES * nb)
        for b in range(nb):
            tok_ref = (u_ref if to_chunks else y_ref).at[0, b, pl.ds(tok0, span), :]
            for h in range(halves):
                chunk_refs = [x_scr.at[gp, h, pl.ds(row0, SUBLANES * nb), :] for gp in range(gpl)]
                tok_rows = [pl.ds(h * pos_per_half + ip, SUBLANES, stride=CHUNK)
                            for ip in range(pos_per_half)]
                chunk_rows = pl.ds(b, SUBLANES, stride=nb)
                if to_chunks:
                    dst = _block_transpose([tok_ref[r, :] for r in tok_rows])
                    for gp in range(gpl):
                        chunk_refs[gp][chunk_rows, :] = dst[gp]
                else:
                    dst = _block_transpose([chunk_refs[gp][chunk_rows, :] for gp in range(gpl)])
                    for ip in range(pos_per_half):
                        tok_ref[tok_rows[ip], :] = dst[ip]

    def relayout_in(co, carry):
        relayout(co, True)
        return carry

    def relayout_out(co, carry):
        relayout(co, False)
        return carry

    lax.fori_loop(0, n_chunks // SUBLANES, relayout_in, 0)

    rc = min(rows, 512)
    row_id = lax.broadcasted_iota(jnp.int32, (SUBLANES, sw), 0)
    x_rows = lambda gp, r0: jnp.concatenate(
        [x_scr[gp, h, r0:r0 + rc, :] for h in range(halves)], axis=-1).astype(BF16)
    for gp in range(gpl):
        for r0 in range(0, rows, rc):
            xr = x_rows(gp, r0)
            for dr in range(ma_ref.shape[1]):
                z_scr[r0:r0 + rc, 2 * dr * sw:2 * (dr + 1) * sw] = jnp.dot(
                    xr, ma_ref[gp, dr], preferred_element_type=F32)
        coef = coef_ref[gp]
        bc = lambda r: jnp.broadcast_to(coef[r:r + 1, :], (SUBLANES, sw))
        af1, af2, ab1, ab2 = bc(0), bc(1), bc(2), bc(3)

        def tile_step(ti, carry):
            sf, sfp, sb, sbp = carry
            rf = pl.multiple_of(ti * SUBLANES, SUBLANES)
            rb = pl.multiple_of((n_tiles - 1 - ti) * SUBLANES, SUBLANES)
            zf = z_scr[pl.ds(rf, SUBLANES), 0:sw]
            zfp = z_scr[pl.ds(rf, SUBLANES), sw:2 * sw]
            zb = z_scr[pl.ds(rb, SUBLANES), 2 * sw:3 * sw]
            zbp = z_scr[pl.ds(rb, SUBLANES), 3 * sw:4 * sw]
            cin_f, cin_b = sf, sb
            for k in range(cpt):
                nf, nfp = af1 * sf + af2 * sfp + zf, af1 * sfp - af2 * sf + zfp
                sf, sfp = pltpu.roll(nf, nb, 0), pltpu.roll(nfp, nb, 0)
                kb = cpt - 1 - k
                nbk, nbkp = ab1 * sb + ab2 * sbp + zb, ab1 * sbp - ab2 * sb + zbp
                sb, sbp = pltpu.roll(nbk, SUBLANES - nb, 0), pltpu.roll(nbkp, SUBLANES - nb, 0)
                if k < cpt - 1:
                    cin_f = jnp.where(row_id >= (k + 1) * nb, sf, cin_f)
                    cin_b = jnp.where(row_id < kb * nb, sb, cin_b)
            cin_scr[pl.ds(rf, SUBLANES), 0:sw] = cin_f
            cin_scr[pl.ds(rb, SUBLANES), sw:2 * sw] = cin_b
            return sf, sfp, sb, sbp

        zero = jnp.zeros((SUBLANES, sw), F32)
        lax.fori_loop(0, n_tiles, tile_step, (zero, zero, zero, zero), unroll=2)

        for r0 in range(0, rows, rc):
            y = (jnp.dot(x_rows(gp, r0), mb_ref[gp, 0:kx, :], preferred_element_type=F32)
                 + jnp.dot(cin_scr[r0:r0 + rc, :].astype(BF16), mb_ref[gp, kx:, :],
                           preferred_element_type=F32))
            for h in range(halves):
                x_scr[gp, h, r0:r0 + rc, :] = y[:, h * LANES:(h + 1) * LANES]

    lax.fori_loop(0, n_chunks // SUBLANES, relayout_out, 0)


def _s5_branch(u, m_a, m_b, coef, layer):
    nq, b, seq, _ = u.shape
    g = m_a.shape[1]
    gpl = g // nq
    n_chunks = seq // CHUNK
    assert SUBLANES % b == 0 and n_chunks % SUBLANES == 0 and gpl * SSM_GROUP == LANES
    rows = n_chunks * b
    n_dir, kx, kb = m_a.shape[2], m_a.shape[3], m_b.shape[2]
    nz = n_dir * m_a.shape[4]
    return pl.pallas_call(
        _s5_kernel,
        grid=(nq,),
        in_specs=[
            pl.BlockSpec((1, b, seq, LANES), lambda i: (i, 0, 0, 0), pipeline_mode=pl.Buffered(1)),
            pl.BlockSpec((None, gpl, n_dir, kx, nz // n_dir), lambda i: (layer, i, 0, 0, 0)),
            pl.BlockSpec((None, gpl, kb, kx), lambda i: (layer, i, 0, 0)),
            pl.BlockSpec((None, gpl, SUBLANES, coef.shape[-1]), lambda i: (layer, i, 0, 0)),
        ],
        out_specs=pl.BlockSpec((1, b, seq, LANES), lambda i: (i, 0, 0, 0)),
        out_shape=jax.ShapeDtypeStruct((nq, b, seq, LANES), F32),
        scratch_shapes=[
            pltpu.VMEM((gpl, kx // LANES, rows, LANES), F32),
            pltpu.VMEM((rows, nz), F32),
            pltpu.VMEM((rows, kb - kx), F32),
        ],
        compiler_params=_params("parallel"),
        name="s5_scan",
    )(u, m_a, m_b, coef)


def _merge_kernel(x_ref, hc_ref, y_ref, gate_ref, wglu_ref, bglu_ref, ws_ref, wc_ref,
                  wout_ref, gn_ref, *rest, moe):
    if moe:
        rw_ref, rb_ref, xo_ref, h_ref, route_ref = rest
    else:
        xo_ref, h_ref = rest
    d = x_ref.shape[-1]
    yg = jax.nn.gelu(jnp.concatenate([y_ref[q, 0] for q in range(y_ref.shape[0])], axis=-1))
    glu = jnp.dot(yg.astype(BF16), wglu_ref[...], preferred_element_type=F32) + bglu_ref[...]
    y2 = yg * jax.nn.sigmoid(glu)
    br_s = jnp.dot(y2.astype(BF16), ws_ref[...], preferred_element_type=F32)
    br_c = jnp.dot(hc_ref[...], wc_ref[...], preferred_element_type=F32)
    gc = gate_ref[:, 0:d].astype(F32)
    gs = gate_ref[:, d:2 * d].astype(F32)
    merged = jax.nn.sigmoid(gc) * br_c + jax.nn.sigmoid(gs) * br_s
    xn = x_ref[...] + jnp.dot(merged.astype(BF16), wout_ref[...], preferred_element_type=F32)
    xo_ref[...] = xn
    h = _rms(xn, gn_ref[...])
    if not moe:
        h_ref[...] = h.astype(h_ref.dtype)
    if moe:
        _rows_to_tiles(h_ref, h)
        h_hi = h.astype(BF16)
        h_lo = (h - h_hi.astype(F32)).astype(BF16)
        logits = (jnp.dot(h_hi, rw_ref[0], preferred_element_type=F32)
                  + jnp.dot(h_hi, rw_ref[1], preferred_element_type=F32)
                  + jnp.dot(h_lo, rw_ref[0], preferred_element_type=F32)) + rb_ref[1:2, :]
        lane = lax.broadcasted_iota(jnp.int32, logits.shape, 1).astype(F32)
        valid = rb_ref[0:1, :] > 0.0
        neg = jnp.float32(-jnp.inf)
        logits = jnp.where(valid, logits, neg)
        m1 = jnp.max(logits, axis=-1, keepdims=True)
        i1 = jnp.min(jnp.where(logits == m1, lane, float(LANES)), axis=-1, keepdims=True)
        rest_l = jnp.where(lane == i1, neg, logits)
        m2 = jnp.max(rest_l, axis=-1, keepdims=True)
        i2 = jnp.min(jnp.where(rest_l == m2, lane, float(LANES)), axis=-1, keepdims=True)
        e2 = jnp.exp(m2 - m1)
        w1 = 1.0 / (1.0 + e2)
        w2 = e2 / (1.0 + e2)
        route_ref[...] = jnp.where(lane == 0.0, i1, jnp.where(lane == 1.0, i2,
                                   jnp.where(lane == 2.0, w1, jnp.where(lane == 3.0, w2, 0.0))))


def _merge(x2, hc, y, gates, layer, wglu, bglu, ws, wc, wout, gn, router=None):
    n, d = x2.shape
    ds_ = hc.shape[1]
    nq, _, seq, _ = y.shape
    tm = ROW_TILE
    per_seq = seq // tm
    moe = router is not None
    row = lambda w: pl.BlockSpec((tm, w), lambda i: (i, 0))
    full = lambda a: pl.BlockSpec(a.shape, lambda i: (0,) * a.ndim)
    y_spec = pl.BlockSpec((nq, 1, tm, LANES), lambda i: (0, i // per_seq, i % per_seq, 0))
    ins = [x2, hc, y, gates, wglu, bglu, ws, wc, wout, gn]
    in_specs = ([row(d), row(ds_), y_spec, row(gates.shape[1])]
                + [_stacked(a, layer) for a in ins[4:]])
    out_specs = [row(d), row(d)]
    out_shape = [jax.ShapeDtypeStruct((n, d), F32), jax.ShapeDtypeStruct((n, d), BF16)]
    if moe:
        tr = d // LANES
        out_specs[1] = pl.BlockSpec((tm * tr, LANES), lambda i: (i, 0))
        out_shape[1] = jax.ShapeDtypeStruct((n * tr, LANES), F32)
        rw, rb = router
        n_exp = rw.shape[1]
        rw_f = jnp.zeros((d, LANES), F32).at[:, :n_exp].set(rw.astype(F32))
        rw_hi = rw_f.astype(BF16)
        rw_pad = jnp.stack([rw_hi, (rw_f - rw_hi.astype(F32)).astype(BF16)])
        rb_pad = jnp.zeros((SUBLANES, LANES), F32).at[0, :n_exp].set(1.0).at[1, :n_exp].set(rb.astype(F32))
        ins += [rw_pad, rb_pad]
        in_specs += [full(rw_pad), full(rb_pad)]
        out_specs.append(row(LANES))
        out_shape.append(jax.ShapeDtypeStruct((n, LANES), F32))
    return pl.pallas_call(
        functools.partial(_merge_kernel, moe=moe),
        grid=(n // tm,),
        in_specs=in_specs,
        out_specs=out_specs,
        out_shape=out_shape,
        compiler_params=_params("parallel"),
        name="merge_moe" if moe else "merge",
    )(*ins)


def _swiglu_rows(h_bf, wg_ref, wu_ref, wd_ref, a_scr):
    d_ff = a_scr.shape[1]
    for c0 in range(0, d_ff, FF_CHUNK):
        cs = slice(c0, c0 + FF_CHUNK)
        gt = jnp.dot(h_bf, wg_ref[:, cs], preferred_element_type=F32)
        up = jnp.dot(h_bf, wu_ref[:, cs], preferred_element_type=F32)
        a_scr[:, cs] = (gt * jax.nn.sigmoid(gt) * up).astype(BF16)
    return jnp.dot(a_scr[...], wd_ref[...], preferred_element_type=F32)


def _ffn_kernel(x_ref, h_ref, wg_ref, wu_ref, wd_ref, gf_ref, o_ref, a_scr, *, final_norm):
    out = x_ref[...] + _swiglu_rows(h_ref[...], wg_ref, wu_ref, wd_ref, a_scr)
    if final_norm:
        out = _rms(out, gf_ref[...])
    o_ref[...] = out


def _ffn_dense(x2, h, wg, wu, wd, idx, gf, final_norm):
    n, d = x2.shape
    d_ff = wg.shape[-1]
    tm = ROW_TILE
    once = pl.Buffered(1)
    return pl.pallas_call(
        functools.partial(_ffn_kernel, final_norm=final_norm),
        grid=(n // tm,),
        in_specs=[
            pl.BlockSpec((tm, d), lambda i: (i, 0)),
            pl.BlockSpec((tm, d), lambda i: (i, 0)),
            pl.BlockSpec((None, d, d_ff), lambda i: (idx, 0, 0), pipeline_mode=once),
            pl.BlockSpec((None, d, d_ff), lambda i: (idx, 0, 0), pipeline_mode=once),
            pl.BlockSpec((None, d_ff, d), lambda i: (idx, 0, 0), pipeline_mode=once),
            pl.BlockSpec((1, d), lambda i: (0, 0)),
        ],
        out_specs=pl.BlockSpec((tm, d), lambda i: (i, 0)),
        out_shape=jax.ShapeDtypeStruct((n, d), F32),
        scratch_shapes=[pltpu.VMEM((tm, d_ff), BF16)],
        compiler_params=_params("parallel"),
        name="ffn_dense",
    )(x2, h, wg, wu, wd, gf.reshape(1, d))


TOKEN_ROWS = SUBLANES
PERM_UNROLL = 4


def _rows_to_tiles(ref, val):
    m = val.shape[0]
    for j in range(TOKEN_ROWS):
        ref[pl.ds(j, m, stride=TOKEN_ROWS), :] = val[:, j * LANES:(j + 1) * LANES]


def _tiles_to_rows(ref):
    m = ref.shape[0] // TOKEN_ROWS
    return jnp.concatenate([ref[pl.ds(j, m, stride=TOKEN_ROWS), :] for j in range(TOKEN_ROWS)],
                           axis=-1)


def _token_copy(src, dst, sem, s, t):
    rows = lambda i: pl.ds(pl.multiple_of(i * TOKEN_ROWS, TOKEN_ROWS), TOKEN_ROWS)
    return pltpu.make_async_copy(src.at[rows(s), :], dst.at[rows(t), :], sem)


def _for_tokens(n_tok, fn):
    def body(i, carry):
        for j in range(PERM_UNROLL):
            fn(i * PERM_UNROLL + j, j % 2)
        return carry
    lax.fori_loop(0, n_tok // PERM_UNROLL, body, 0)


def _dispatch_kernel(zb_ref, nu_ref, p0_ref, p1_ref, h_ref, xs_ref, zero_scr, sem):
    n_tok = h_ref.shape[0] // TOKEN_ROWS
    n_exp = zb_ref.shape[0]

    @pl.when(pl.program_id(0) == 0)
    def _():
        zero_scr[...] = jnp.zeros(zero_scr.shape, F32)
        rows = zero_scr.shape[0]

        def fill(blk):
            start = pl.multiple_of(blk * rows, rows)
            return pltpu.make_async_copy(zero_scr, xs_ref.at[pl.ds(start, rows), :], sem)

        for e in range(n_exp):
            fill(zb_ref[e]).start()
        for e in range(n_exp):
            fill(zb_ref[e]).wait()

        def fill_unused(blk, carry):
            fill(blk).start()
            fill(blk).wait()
            return carry

        lax.fori_loop(nu_ref[0], xs_ref.shape[0] // rows, fill_unused, 0)

    def copies(r):
        return (_token_copy(h_ref, xs_ref, sem, r, p0_ref[0, 0, r]),
                _token_copy(h_ref, xs_ref, sem, r, p1_ref[0, 0, r]))

    _for_tokens(n_tok, lambda r, lane: [cp.start(priority=lane) for cp in copies(r)])
    _for_tokens(n_tok, lambda r, lane: [cp.wait() for cp in copies(r)])


def _dispatch(h, pos0, pos1, zero_blocks, n_used, n_sorted):
    n = h.shape[0] // TOKEN_ROWS
    tp = PERM_TILE
    smem_rows = lambda: pl.BlockSpec((1, 1, tp), lambda i, zb, nu: (i, 0, 0),
                                     memory_space=pltpu.SMEM)
    grid_spec = pltpu.PrefetchScalarGridSpec(
        num_scalar_prefetch=2,
        grid=(n // tp,),
        in_specs=[smem_rows(), smem_rows(),
                  pl.BlockSpec((tp * TOKEN_ROWS, LANES), lambda i, zb, nu: (i, 0))],
        out_specs=pl.BlockSpec(memory_space=pl.ANY),
        scratch_shapes=[pltpu.VMEM((MOE_TILE * TOKEN_ROWS, LANES), F32),
                        pltpu.SemaphoreType.DMA(())],
    )
    return pl.pallas_call(
        _dispatch_kernel,
        grid_spec=grid_spec,
        out_shape=jax.ShapeDtypeStruct((n_sorted * TOKEN_ROWS, LANES), F32),
        compiler_params=_params("arbitrary"),
        name="moe_dispatch",
    )(zero_blocks, n_used, pos0.reshape(n // tp, 1, tp), pos1.reshape(n // tp, 1, tp), h)


def _expert_kernel(be_ref, nu_ref, xs_ref, wg_ref, wu_ref, wd_ref, ys_ref, a_scr):
    @pl.when(pl.program_id(0) < nu_ref[0])
    def _():
        _rows_to_tiles(ys_ref, _swiglu_rows(_tiles_to_rows(xs_ref).astype(BF16), wg_ref, wu_ref,
                                            wd_ref, a_scr))

    @pl.when(pl.program_id(0) >= nu_ref[0])
    def _():
        ys_ref[...] = jnp.zeros(ys_ref.shape, F32)


def _experts(xs, wg, wu, wd, idx, block_expert, n_used):
    d, d_ff = wg.shape[-2:]
    tm = MOE_TILE
    tile = (tm * TOKEN_ROWS, LANES)
    grid_spec = pltpu.PrefetchScalarGridSpec(
        num_scalar_prefetch=2,
        grid=(xs.shape[0] // tile[0],),
        in_specs=[
            pl.BlockSpec(tile, lambda i, be, nu: (jnp.minimum(i, nu[0] - 1), 0)),
            pl.BlockSpec((None, None, d, d_ff), lambda i, be, nu: (idx, be[i], 0, 0)),
            pl.BlockSpec((None, None, d, d_ff), lambda i, be, nu: (idx, be[i], 0, 0)),
            pl.BlockSpec((None, None, d_ff, d), lambda i, be, nu: (idx, be[i], 0, 0)),
        ],
        out_specs=pl.BlockSpec(tile, lambda i, be, nu: (i, 0)),
        scratch_shapes=[pltpu.VMEM((tm, d_ff), BF16)],
    )
    return pl.pallas_call(
        _expert_kernel,
        grid_spec=grid_spec,
        out_shape=jax.ShapeDtypeStruct(xs.shape, F32),
        compiler_params=_params("arbitrary"),
        name="moe_experts",
    )(block_expert, n_used, xs, wg, wu, wd)


def _combine_kernel(p0_ref, p1_ref, q0_ref, q1_ref, x_ref, route_ref, gf_ref, ys_ref, o_ref,
                    g_scr, sem, *, final_norm):
    n_tok = x_ref.shape[0]
    step = pl.program_id(0)
    slot = lax.rem(step, 2)

    def copies(pa_ref, pb_ref, sl):
        return lambda r: (
            _token_copy(ys_ref, g_scr.at[sl, 0], sem.at[sl], pa_ref[0, 0, r], r),
            _token_copy(ys_ref, g_scr.at[sl, 1], sem.at[sl], pb_ref[0, 0, r], r))

    def start(cps):
        _for_tokens(n_tok, lambda r, lane: [cp.start(priority=lane) for cp in cps(r)])

    @pl.when(step == 0)
    def _():
        start(copies(p0_ref, p1_ref, 0))

    @pl.when(step + 1 < pl.num_programs(0))
    def _():
        start(copies(q0_ref, q1_ref, 1 - slot))

    mine = copies(p0_ref, p1_ref, slot)
    _for_tokens(n_tok, lambda r, lane: [cp.wait() for cp in mine(r)])
    w1 = route_ref[:, 2:3]
    w2 = route_ref[:, 3:4]
    out = (x_ref[...] + w1 * _tiles_to_rows(g_scr.at[slot, 0])
           + w2 * _tiles_to_rows(g_scr.at[slot, 1]))
    if final_norm:
        out = _rms(out, gf_ref[...])
    o_ref[...] = out


def _combine(x2, route, ys, pos0, pos1, gf, final_norm):
    n, d = x2.shape
    tp = PERM_TILE
    n_steps = n // tp
    cur = lambda: pl.BlockSpec((1, 1, tp), lambda i: (i, 0, 0), memory_space=pltpu.SMEM)
    nxt = lambda: pl.BlockSpec((1, 1, tp), lambda i: (jnp.minimum(i + 1, n_steps - 1), 0, 0),
                               memory_space=pltpu.SMEM)
    pos0, pos1 = pos0.reshape(n_steps, 1, tp), pos1.reshape(n_steps, 1, tp)
    return pl.pallas_call(
        functools.partial(_combine_kernel, final_norm=final_norm),
        grid=(n_steps,),
        in_specs=[cur(), cur(), nxt(), nxt(),
                  pl.BlockSpec((tp, d), lambda i: (i, 0)),
                  pl.BlockSpec((tp, LANES), lambda i: (i, 0)),
                  pl.BlockSpec((1, d), lambda i: (0, 0)),
                  pl.BlockSpec(memory_space=pl.ANY)],
        out_specs=pl.BlockSpec((tp, d), lambda i: (i, 0)),
        out_shape=jax.ShapeDtypeStruct((n, d), F32),
        scratch_shapes=[pltpu.VMEM((2, 2, tp * TOKEN_ROWS, LANES), F32),
                        pltpu.SemaphoreType.DMA((2,))],
        compiler_params=_params("arbitrary"),
        name="moe_combine",
    )(pos0, pos1, pos0, pos1, x2, route, gf.reshape(1, d), ys)


def _routing_tables(route, n_exp):
    n = route.shape[0]
    tm = MOE_TILE
    i1 = route[:, 0].astype(jnp.int32)
    i2 = route[:, 1].astype(jnp.int32)
    oh1 = jax.nn.one_hot(i1, n_exp, dtype=jnp.int32)
    oh2 = jax.nn.one_hot(i2, n_exp, dtype=jnp.int32)
    tot = oh1 + oh2
    csum = jnp.cumsum(tot, axis=0)
    excl = csum - tot
    rank1 = jnp.sum(excl * oh1, axis=-1)
    rank2 = jnp.sum((excl + oh1) * oh2, axis=-1)
    counts = csum[-1]
    nblk = (counts + tm - 1) // tm
    blk_end = jnp.cumsum(nblk)
    blk_off = blk_end - nblk
    row_off = blk_off * tm
    pos0 = row_off[i1] + rank1
    pos1 = row_off[i2] + rank2
    n_blocks = TOP_K * n // tm + n_exp
    n_used = blk_end[-1]
    jb = jnp.minimum(jnp.arange(n_blocks, dtype=jnp.int32), n_used - 1)
    block_expert = jnp.sum((blk_end[None, :] <= jb[:, None]).astype(jnp.int32), axis=-1)
    block_expert = jnp.minimum(block_expert, n_exp - 1)
    zero_blocks = jnp.where(nblk > 0, blk_end - 1, n_blocks - 1).astype(jnp.int32)
    return (pos0.astype(jnp.int32), pos1.astype(jnp.int32), block_expert,
            n_used.astype(jnp.int32).reshape(1), zero_blocks, n_blocks * tm)


def _moe(x2, h, route, wg, wu, wd, idx, gf, final_norm):
    n_exp = wg.shape[1]
    pos0, pos1, block_expert, n_used, zero_blocks, n_sorted = _routing_tables(route, n_exp)
    xs = _dispatch(h, pos0, pos1, zero_blocks, n_used, n_sorted)
    ys = _experts(xs, wg, wu, wd, idx, block_expert, n_used)
    return _combine(x2, route, ys, pos0, pos1, gf, final_norm)


def kernel(x, norm_mix_g, w_in, conv_dw_w, conv_dw_b, conv_ln_g, conv_ln_b, w_conv_proj,
           ssm_a_re, ssm_a_im, ssm_log_dt, ssm_b_re, ssm_b_im, ssm_c_re, ssm_c_im, ssm_d,
           ssm_w_glu, ssm_b_glu, w_ssm_proj, w_out, norm_ffn_g,
           ffn_w_gate, ffn_w_up, ffn_w_down, router_w, router_b,
           moe_w_gate, moe_w_up, moe_w_down, final_norm_g):
    bsz, seq, d = x.shape
    depth = w_in.shape[0]
    d_conv = conv_dw_w.shape[-1]
    d_ssm = ssm_d.shape[-1]
    n = bsz * seq
    bf = lambda a: a.astype(BF16)
    x2 = x.reshape(n, d).astype(F32)
    m_a, m_b, coef = _s5_matrices(ssm_a_re, ssm_a_im, ssm_log_dt, ssm_b_re, ssm_b_im,
                                  ssm_c_re, ssm_c_im, ssm_d)
    mix_g, ffn_g = _rows3(norm_mix_g), _rows3(norm_ffn_g)
    dw_b, ln_g, ln_b = _rows3(conv_dw_b), _rows3(conv_ln_g), _rows3(conv_ln_b)
    b_glu = _rows3(ssm_b_glu)
    w_in_bf, w_glu_bf, w_sp_bf = bf(w_in), bf(ssm_w_glu), bf(w_ssm_proj)
    w_cp_bf, w_out_bf = bf(w_conv_proj), bf(w_out)
    ffn_bf = (bf(ffn_w_gate), bf(ffn_w_up), bf(ffn_w_down))
    moe_bf = (bf(moe_w_gate), bf(moe_w_up), bf(moe_w_down))
    for layer in range(depth):
        last = layer == depth - 1
        ch, u, gates = _inproj(x2, mix_g, w_in_bf, layer, bsz, d_conv, d_ssm)
        hc = _conv_branch(ch.reshape(bsz, seq, d_conv), conv_dw_w, dw_b, ln_g, ln_b,
                          layer).reshape(n, d_conv)
        y = _s5_branch(u, m_a, m_b, coef, layer)
        i = layer // 2
        moe = layer % 2 == 1
        router = (router_w[i], router_b[i]) if moe else None
        outs = _merge(x2, hc, y, gates, layer, w_glu_bf, b_glu, w_sp_bf, w_cp_bf, w_out_bf,
                      ffn_g, router)
        if moe:
            x2, h, route = outs
            x2 = _moe(x2, h, route, *moe_bf, i, final_norm_g, last)
        else:
            x2, h = outs
            x2 = _ffn_dense(x2, h, *ffn_bf, i, final_norm_g, last)
    return x2.reshape(bsz, seq, d).astype(x.dtype)
```

```python
import functools
import math

import jax
import jax.numpy as jnp
from jax import lax
from jax.experimental import pallas as pl
from jax.experimental.pallas import tpu as pltpu

F32 = jnp.float32
BF16 = jnp.bfloat16

RMS_EPS = 1e-6
LN_EPS = 1e-5
CONV_WIDTH = 31
CONV_HALF = CONV_WIDTH // 2
SSM_GROUP = 16
SSM_STATE = 64
CHUNK = 16
TOP_K = 2
LANES = 128
SUBLANES = 8
VMEM_LIMIT = 56 * 1024 * 1024

ROW_TILE = 512
MOE_TILE = 256
PERM_TILE = 512
CONV_TILE = 128
FF_CHUNK = 256


def _params(*sem):
    return pltpu.CompilerParams(dimension_semantics=sem, vmem_limit_bytes=VMEM_LIMIT)


def _rms(xf, g):
    ms = jnp.mean(xf * xf, axis=-1, keepdims=True)
    return xf * lax.rsqrt(ms + RMS_EPS) * g


def _stacked(arr, *lead):
    rest = arr.shape[len(lead):]
    return pl.BlockSpec((None,) * len(lead) + rest, lambda *_: tuple(lead) + (0,) * len(rest))


def _rows3(a):
    return a.reshape(a.shape[0], 1, a.shape[1])


def _inproj_kernel(x_ref, g_ref, w_ref, ch_ref, u_ref, gate_ref):
    dc = ch_ref.shape[-1]
    ds_ = u_ref.shape[0] * LANES
    h = _rms(x_ref[...], g_ref[...]).astype(BF16)
    v = jnp.dot(h, w_ref[:, 0:dc], preferred_element_type=F32)
    gt = jnp.dot(h, w_ref[:, dc:2 * dc], preferred_element_type=F32)
    ch_ref[...] = v * jax.nn.sigmoid(gt)
    u = jnp.dot(h, w_ref[:, 2 * dc:2 * dc + ds_], preferred_element_type=F32)
    for q in range(u_ref.shape[0]):
        u_ref[q, 0] = u[:, q * LANES:(q + 1) * LANES]
    gate_ref[...] = jnp.dot(h, w_ref[:, 2 * dc + ds_:], preferred_element_type=F32).astype(BF16)


def _inproj(x2, g, w_bf, layer, bsz, d_conv, d_ssm):
    n, d = x2.shape
    seq = n // bsz
    d_in = w_bf.shape[-1]
    d_gate = d_in - 2 * d_conv - d_ssm
    tm = ROW_TILE
    per_seq = seq // tm
    nq = d_ssm // LANES
    return pl.pallas_call(
        _inproj_kernel,
        grid=(n // tm,),
        in_specs=[
            pl.BlockSpec((tm, d), lambda i: (i, 0)),
            _stacked(g, layer),
            _stacked(w_bf, layer),
        ],
        out_specs=[
            pl.BlockSpec((tm, d_conv), lambda i: (i, 0)),
            pl.BlockSpec((nq, 1, tm, LANES), lambda i: (0, i // per_seq, i % per_seq, 0)),
            pl.BlockSpec((tm, d_gate), lambda i: (i, 0)),
        ],
        out_shape=[
            jax.ShapeDtypeStruct((n, d_conv), F32),
            jax.ShapeDtypeStruct((nq, bsz, seq, LANES), F32),
            jax.ShapeDtypeStruct((n, d_gate), BF16),
        ],
        compiler_params=_params("parallel"),
        name="inproj",
    )(x2, g, w_bf)


def _conv_kernel(h_ref, w_ref, b_ref, lg_ref, lb_ref, o_ref, pad_scr, tile_scr):
    seq, c = h_ref.shape[1], h_ref.shape[2]
    front = 2 * SUBLANES
    zeros = jnp.zeros((front, c), F32)
    pad_scr[0:front, :] = zeros
    pad_scr[front + seq:front + seq + front, :] = zeros
    pad_scr[front:front + seq, :] = h_ref[0]

    phases = [[] for _ in range(SUBLANES)]
    for k in range(CONV_WIDTH):
        off = k - CONV_HALF + front
        phases[off % SUBLANES].append((k, off - off % SUBLANES))

    def tile_body(ti, carry):
        base = pl.multiple_of(ti * CONV_TILE, CONV_TILE)
        win_ref = pad_scr.at[pl.ds(base, CONV_TILE + 2 * front), :]
        for lb in range(c // LANES):
            ls = slice(lb * LANES, (lb + 1) * LANES)
            acc = b_ref[:, ls]
            for s, taps in enumerate(phases):
                part = None
                for k, row0 in taps:
                    term = w_ref[k:k + 1, ls] * win_ref[pl.ds(row0, CONV_TILE + SUBLANES), ls]
                    part = term if part is None else part + term
                if part is not None:
                    acc = acc + part[s:s + CONV_TILE, :]
            tile_scr[:, ls] = acc
        hh = tile_scr[...]
        mu = jnp.mean(hh, axis=-1, keepdims=True)
        cen = hh - mu
        var = jnp.mean(cen * cen, axis=-1, keepdims=True)
        y = cen * lax.rsqrt(var + LN_EPS) * lg_ref[...] + lb_ref[...]
        o_ref[0, pl.ds(base, CONV_TILE), :] = (y * jax.nn.sigmoid(y)).astype(BF16)
        return carry

    lax.fori_loop(0, seq // CONV_TILE, tile_body, 0)


def _conv_branch(ch, dw_w, dw_b, ln_g, ln_b, layer):
    b, seq, c = ch.shape
    front = 2 * SUBLANES
    return pl.pallas_call(
        _conv_kernel,
        grid=(b,),
        in_specs=[
            pl.BlockSpec((1, seq, c), lambda i: (i, 0, 0)),
            _stacked(dw_w, layer), _stacked(dw_b, layer), _stacked(ln_g, layer),
            _stacked(ln_b, layer),
        ],
        out_specs=pl.BlockSpec((1, seq, c), lambda i: (i, 0, 0)),
        out_shape=jax.ShapeDtypeStruct((b, seq, c), BF16),
        scratch_shapes=[
            pltpu.VMEM((seq + 2 * front, c), F32),
            pltpu.VMEM((CONV_TILE, c), F32),
        ],
        compiler_params=_params("parallel"),
        name="conv_branch",
    )(ch, dw_w, dw_b, ln_g, ln_b)


def _s5_matrices(a_re, a_im, log_dt, b_re, b_im, c_re, c_im, d_skip):
    t = CHUNK
    n_layers, n_dir, g, p = a_re.shape
    c = b_re.shape[-1]
    f = lambda a: a.astype(F32)
    cat = lambda *xs: jnp.concatenate(xs, axis=-1)
    a_re, a_im, c_re, c_im = f(a_re), f(a_im), f(c_re), f(c_im)
    dt = jnp.exp(f(log_dt))[..., None]
    lam_re, lam_im = a_re * dt, a_im * dt
    up, down = jnp.arange(t, dtype=F32), jnp.arange(t - 1, -1, -1, dtype=F32)
    expo = jnp.stack([cat(down, up + 1.0, up, jnp.full((1,), t, F32)),
                      cat(up, down + 1.0, up, jnp.full((1,), t, F32))])
    expo = expo[None, :, None, :, None]
    mag = jnp.exp(expo * lam_re[:, :, :, None, :])
    pw_re = mag * jnp.cos(expo * lam_im[:, :, :, None, :])
    pw_im = mag * jnp.sin(expo * lam_im[:, :, :, None, :])
    ab_re = jnp.exp(lam_re) * jnp.cos(lam_im)
    ab_im = jnp.exp(lam_re) * jnp.sin(lam_im)
    n_re, n_im = ab_re - 1.0, ab_im
    den = a_re * a_re + a_im * a_im
    q_re = ((n_re * a_re + n_im * a_im) / den)[:, :, :, None, :]
    q_im = ((n_im * a_re - n_re * a_im) / den)[:, :, :, None, :]
    bt_re = jnp.swapaxes(f(b_re), -1, -2)[:, None]
    bt_im = jnp.swapaxes(f(b_im), -1, -2)[:, None]
    bb_re = q_re * bt_re - q_im * bt_im
    bb_im = q_re * bt_im + q_im * bt_re

    def powers(lo, hi_, reps):
        sl = lambda w: jnp.tile(w[:, :, :, lo:hi_], (1, 1, 1, 1, reps))[:, :, :, :, None, :]
        return sl(pw_re), sl(pw_im)

    kr, ki = powers(0, t, 4)
    z = (kr * cat(bb_re, bb_im, bb_im, bb_re)[:, :, :, None]
         + ki * cat(-bb_im, bb_re, bb_re, -bb_im)[:, :, :, None])
    m_a = jnp.swapaxes(z.reshape(n_layers, n_dir, g, t * c, 4 * p), 1, 2).astype(BF16)
    kr, ki = powers(t, 2 * t, 2)
    v = (kr * cat(c_re, -c_im)[:, :, :, None] + ki * cat(-c_im, -c_re)[:, :, :, None])
    m_o = jnp.swapaxes(v.reshape(n_layers, n_dir, g, t * c, 2 * p), -1, -2)
    pr = jnp.swapaxes(pw_re[:, :, :, 2 * t:3 * t], -1, -2)[..., None]
    pi = jnp.swapaxes(pw_im[:, :, :, 2 * t:3 * t], -1, -2)[..., None]
    cp_re = jnp.swapaxes(c_re, -1, -2)[:, :, :, :, None, :]
    cp_im = jnp.swapaxes(c_im, -1, -2)[:, :, :, :, None, :]
    vp_re = (pr * cp_re - pi * cp_im)[:, :, :, :, None]
    vp_im = (pr * cp_im + pi * cp_re)[:, :, :, :, None]
    bp_re = jnp.swapaxes(bb_re, -1, -2)[..., None, None]
    bp_im = jnp.swapaxes(bb_im, -1, -2)[..., None, None]
    kk = jnp.sum(bp_re * vp_re - bp_im * vp_im, axis=3)
    kf, kb = kk[:, 0], kk[:, 1]
    diag = kf[:, :, :, 0] + kb[:, :, :, 0] + (f(d_skip).reshape(n_layers, g, 1, c)
                                              * jnp.eye(c, dtype=F32))
    lags = jnp.concatenate([kb[:, :, :, :0:-1], diag[:, :, :, None], kf[:, :, :, 1:]], axis=3)
    lags = lags.reshape(n_layers, g, c, (2 * t - 1) * c)
    m_intra = jnp.stack([lags[..., (t - 1 - j) * c:(2 * t - 1 - j) * c] for j in range(t)], axis=2)
    m_b = jnp.concatenate([m_intra.reshape(n_layers, g, t * c, t * c), m_o[:, 0], m_o[:, 1]],
                          axis=2).astype(BF16)
    at_re, at_im = pw_re[:, :, :, 3 * t], pw_im[:, :, :, 3 * t]
    rows = [cat(at_re[:, 0], at_re[:, 0]), cat(-at_im[:, 0], at_im[:, 0]),
            cat(at_re[:, 1], at_re[:, 1]), cat(-at_im[:, 1], at_im[:, 1])]
    coef = jnp.stack(rows + [jnp.zeros_like(rows[0])] * (SUBLANES - len(rows)), axis=2)
    return m_a, m_b, coef


def _block_transpose(v):
    nblk = len(v)
    blk = lax.broadcasted_iota(jnp.int32, v[0].shape, 1) // SSM_GROUP
    d = nblk // 2
    while d >= 1:
        keep_lo = (blk & d) == 0
        out = list(v)
        for r in range(nblk):
            if r & d:
                continue
            lo, hi = v[r], v[r + d]
            out[r] = jnp.where(keep_lo, lo, pltpu.roll(hi, d * SSM_GROUP, 1))
            out[r + d] = jnp.where(keep_lo, pltpu.roll(lo, LANES - d * SSM_GROUP, 1), hi)
        v = out
        d //= 2
    return v


def _s5_kernel(u_ref, ma_ref, mb_ref, coef_ref, y_ref, x_scr, z_scr, cin_scr):
    nb, seq = u_ref.shape[1], u_ref.shape[2]
    gpl, halves, rows = x_scr.shape[0:3]
    kx = halves * LANES
    sw = coef_ref.shape[-1]
    n_chunks = seq // CHUNK
    cpt = SUBLANES // nb
    n_tiles = rows // SUBLANES
    span = SUBLANES * CHUNK
    pos_per_half = LANES // SSM_GROUP

    def relayout(co, to_chunks):
        tok0 = pl.multiple_of(co * span, span)
        row0 = pl.multiple_of(co * SUBLANES * nb, SUBLANES * nb)
        for b in range(nb):
            tok_ref = (u_ref if to_chunks else y_ref).at[0, b, pl.ds(tok0, span), :]
            for h in range(halves):
                chunk_refs = [x_scr.at[gp, h, pl.ds(row0, SUBLANES * nb), :] for gp in range(gpl)]
                tok_rows = [pl.ds(h * pos_per_half + ip, SUBLANES, stride=CHUNK)
                            for ip in range(pos_per_half)]
                chunk_rows = pl.ds(b, SUBLANES, stride=nb)
                if to_chunks:
                    dst = _block_transpose([tok_ref[r, :] for r in tok_rows])
                    for gp in range(gpl):
                        chunk_refs[gp][chunk_rows, :] = dst[gp]
                else:
                    dst = _block_transpose([chunk_refs[gp][chunk_rows, :] for gp in range(gpl)])
                    for ip in range(pos_per_half):
                        tok_ref[tok_rows[ip], :] = dst[ip]

    def relayout_in(co, carry):
        relayout(co, True)
        return carry

    def relayout_out(co, carry):
        relayout(co, False)
        return carry

    lax.fori_loop(0, n_chunks // SUBLANES, relayout_in, 0)

    rc = min(rows, 512)
    row_id = lax.broadcasted_iota(jnp.int32, (SUBLANES, sw), 0)
    x_rows = lambda gp, r0: jnp.concatenate(
        [x_scr[gp, h, r0:r0 + rc, :] for h in range(halves)], axis=-1).astype(BF16)
    for gp in range(gpl):
        for r0 in range(0, rows, rc):
            xr = x_rows(gp, r0)
            for dr in range(ma_ref.shape[1]):
                z_scr[r0:r0 + rc, 2 * dr * sw:2 * (dr + 1) * sw] = jnp.dot(
                    xr, ma_ref[gp, dr], preferred_element_type=F32)
        coef = coef_ref[gp]
        bc = lambda r: jnp.broadcast_to(coef[r:r + 1, :], (SUBLANES, sw))
        af1, af2, ab1, ab2 = bc(0), bc(1), bc(2), bc(3)

        def tile_step(ti, carry):
            sf, sfp, sb, sbp = carry
            rf = pl.multiple_of(ti * SUBLANES, SUBLANES)
            rb = pl.multiple_of((n_tiles - 1 - ti) * SUBLANES, SUBLANES)
            zf = z_scr[pl.ds(rf, SUBLANES), 0:sw]
            zfp = z_scr[pl.ds(rf, SUBLANES), sw:2 * sw]
            zb = z_scr[pl.ds(rb, SUBLANES), 2 * sw:3 * sw]
            zbp = z_scr[pl.ds(rb, SUBLANES), 3 * sw:4 * sw]
            cin_f, cin_b = sf, sb
            for k in range(cpt):
                nf, nfp = af1 * sf + af2 * sfp + zf, af1 * sfp - af2 * sf + zfp
                sf, sfp = pltpu.roll(nf, nb, 0), pltpu.roll(nfp, nb, 0)
                kb = cpt - 1 - k
                nbk, nbkp = ab1 * sb + ab2 * sbp + zb, ab1 * sbp - ab2 * sb + zbp
                sb, sbp = pltpu.roll(nbk, SUBLANES - nb, 0), pltpu.roll(nbkp, SUBLANES - nb, 0)
                if k < cpt - 1:
                    cin_f = jnp.where(row_id >= (k + 1) * nb, sf, cin_f)
                    cin_b = jnp.where(row_id < kb * nb, sb, cin_b)
            cin_scr[pl.ds(rf, SUBLANES), 0:sw] = cin_f
            cin_scr[pl.ds(rb, SUBLANES), sw:2 * sw] = cin_b
            return sf, sfp, sb, sbp

        zero = jnp.zeros((SUBLANES, sw), F32)
        lax.fori_loop(0, n_tiles, tile_step, (zero, zero, zero, zero), unroll=2)

        for r0 in range(0, rows, rc):
            y = (jnp.dot(x_rows(gp, r0), mb_ref[gp, 0:kx, :], preferred_element_type=F32)
                 + jnp.dot(cin_scr[r0:r0 + rc, :].astype(BF16), mb_ref[gp, kx:, :],
                           preferred_element_type=F32))
            for h in range(halves):
                x_scr[gp, h, r0:r0 + rc, :] = y[:, h * LANES:(h + 1) * LANES]

    lax.fori_loop(0, n_chunks // SUBLANES, relayout_out, 0)


def _s5_branch(u, m_a, m_b, coef, layer):
    nq, b, seq, _ = u.shape
    g = m_a.shape[1]
    gpl = g // nq
    n_chunks = seq // CHUNK
    assert SUBLANES % b == 0 and n_chunks % SUBLANES == 0 and gpl * SSM_GROUP == LANES
    rows = n_chunks * b
    n_dir, kx, kb = m_a.shape[2], m_a.shape[3], m_b.shape[2]
    nz = n_dir * m_a.shape[4]
    return pl.pallas_call(
        _s5_kernel,
        grid=(nq,),
        in_specs=[
            pl.BlockSpec((1, b, seq, LANES), lambda i: (i, 0, 0, 0), pipeline_mode=pl.Buffered(1)),
            pl.BlockSpec((None, gpl, n_dir, kx, nz // n_dir), lambda i: (layer, i, 0, 0, 0)),
            pl.BlockSpec((None, gpl, kb, kx), lambda i: (layer, i, 0, 0)),
            pl.BlockSpec((None, gpl, SUBLANES, coef.shape[-1]), lambda i: (layer, i, 0, 0)),
        ],
        out_specs=pl.BlockSpec((1, b, seq, LANES), lambda i: (i, 0, 0, 0)),
        out_shape=jax.ShapeDtypeStruct((nq, b, seq, LANES), F32),
        scratch_shapes=[
            pltpu.VMEM((gpl, kx // LANES, rows, LANES), F32),
            pltpu.VMEM((rows, nz), F32),
            pltpu.VMEM((rows, kb - kx), F32),
        ],
        compiler_params=_params("parallel"),
        name="s5_scan",
    )(u, m_a, m_b, coef)


def _merge_kernel(x_ref, hc_ref, y_ref, gate_ref, wglu_ref, bglu_ref, ws_ref, wc_ref,
                  wout_ref, gn_ref, *rest, moe):
    if moe:
        rw_ref, rb_ref, xo_ref, h_ref, route_ref = rest
    else:
        xo_ref, h_ref = rest
    d = x_ref.shape[-1]
    yg = jax.nn.gelu(jnp.concatenate([y_ref[q, 0] for q in range(y_ref.shape[0])], axis=-1))
    glu = jnp.dot(yg.astype(BF16), wglu_ref[...], preferred_element_type=F32) + bglu_ref[...]
    y2 = yg * jax.nn.sigmoid(glu)
    br_s = jnp.dot(y2.astype(BF16), ws_ref[...], preferred_element_type=F32)
    br_c = jnp.dot(hc_ref[...], wc_ref[...], preferred_element_type=F32)
    gc = gate_ref[:, 0:d].astype(F32)
    gs = gate_ref[:, d:2 * d].astype(F32)
    merged = jax.nn.sigmoid(gc) * br_c + jax.nn.sigmoid(gs) * br_s
    xn = x_ref[...] + jnp.dot(merged.astype(BF16), wout_ref[...], preferred_element_type=F32)
    xo_ref[...] = xn
    h = _rms(xn, gn_ref[...])
    if not moe:
        h_ref[...] = h.astype(h_ref.dtype)
    if moe:
        _rows_to_tiles(h_ref, h)
        h_hi = h.astype(BF16)
        h_lo = (h - h_hi.astype(F32)).astype(BF16)
        logits = (jnp.dot(h_hi, rw_ref[0], preferred_element_type=F32)
                  + jnp.dot(h_hi, rw_ref[1], preferred_element_type=F32)
                  + jnp.dot(h_lo, rw_ref[0], preferred_element_type=F32)) + rb_ref[1:2, :]
        lane = lax.broadcasted_iota(jnp.int32, logits.shape, 1).astype(F32)
        valid = rb_ref[0:1, :] > 0.0
        neg = jnp.float32(-jnp.inf)
        logits = jnp.where(valid, logits, neg)
        m1 = jnp.max(logits, axis=-1, keepdims=True)
        i1 = jnp.min(jnp.where(logits == m1, lane, float(LANES)), axis=-1, keepdims=True)
        rest_l = jnp.where(lane == i1, neg, logits)
        m2 = jnp.max(rest_l, axis=-1, keepdims=True)
        i2 = jnp.min(jnp.where(rest_l == m2, lane, float(LANES)), axis=-1, keepdims=True)
        e2 = jnp.exp(m2 - m1)
        w1 = 1.0 / (1.0 + e2)
        w2 = e2 / (1.0 + e2)
        route_ref[...] = jnp.where(lane == 0.0, i1, jnp.where(lane == 1.0, i2,
                                   jnp.where(lane == 2.0, w1, jnp.where(lane == 3.0, w2, 0.0))))


def _merge(x2, hc, y, gates, layer, wglu, bglu, ws, wc, wout, gn, router=None):
    n, d = x2.shape
    ds_ = hc.shape[1]
    nq, _, seq, _ = y.shape
    tm = ROW_TILE
    per_seq = seq // tm
    moe = router is not None
    row = lambda w: pl.BlockSpec((tm, w), lambda i: (i, 0))
    full = lambda a: pl.BlockSpec(a.shape, lambda i: (0,) * a.ndim)
    y_spec = pl.BlockSpec((nq, 1, tm, LANES), lambda i: (0, i // per_seq, i % per_seq, 0))
    ins = [x2, hc, y, gates, wglu, bglu, ws, wc, wout, gn]
    in_specs = ([row(d), row(ds_), y_spec, row(gates.shape[1])]
                + [_stacked(a, layer) for a in ins[4:]])
    out_specs = [row(d), row(d)]
    out_shape = [jax.ShapeDtypeStruct((n, d), F32), jax.ShapeDtypeStruct((n, d), BF16)]
    if moe:
        tr = d // LANES
        out_specs[1] = pl.BlockSpec((tm * tr, LANES), lambda i: (i, 0))
        out_shape[1] = jax.ShapeDtypeStruct((n * tr, LANES), F32)
        rw, rb = router
        n_exp = rw.shape[1]
        rw_f = jnp.zeros((d, LANES), F32).at[:, :n_exp].set(rw.astype(F32))
        rw_hi = rw_f.astype(BF16)
        rw_pad = jnp.stack([rw_hi, (rw_f - rw_hi.astype(F32)).astype(BF16)])
        rb_pad = jnp.zeros((SUBLANES, LANES), F32).at[0, :n_exp].set(1.0).at[1, :n_exp].set(rb.astype(F32))
        ins += [rw_pad, rb_pad]
        in_specs += [full(rw_pad), full(rb_pad)]
        out_specs.append(row(LANES))
        out_shape.append(jax.ShapeDtypeStruct((n, LANES), F32))
    return pl.pallas_call(
        functools.partial(_merge_kernel, moe=moe),
        grid=(n // tm,),
        in_specs=in_specs,
        out_specs=out_specs,
        out_shape=out_shape,
        compiler_params=_params("parallel"),
        name="merge_moe" if moe else "merge",
    )(*ins)


def _swiglu_rows(h_bf, wg_ref, wu_ref, wd_ref, a_scr, wd_scr=None):
    d_ff = a_scr.shape[1]
    for c0 in range(0, d_ff, FF_CHUNK):
        cs = slice(c0, c0 + FF_CHUNK)
        gt = jnp.dot(h_bf, wg_ref[:, cs].astype(BF16), preferred_element_type=F32)
        up = jnp.dot(h_bf, wu_ref[:, cs].astype(BF16), preferred_element_type=F32)
        a_scr[:, cs] = (gt * jax.nn.sigmoid(gt) * up).astype(BF16)
        if wd_scr is not None:
            wd_scr[cs, :] = wd_ref[cs, :].astype(BF16)
    wd = wd_ref[...] if wd_scr is None else wd_scr[...]
    return jnp.dot(a_scr[...], wd, preferred_element_type=F32)


def _ffn_kernel(x_ref, h_ref, wg_ref, wu_ref, wd_ref, gf_ref, o_ref, a_scr, *, final_norm):
    out = x_ref[...] + _swiglu_rows(h_ref[...], wg_ref, wu_ref, wd_ref, a_scr)
    if final_norm:
        out = _rms(out, gf_ref[...])
    o_ref[...] = out


def _ffn_dense(x2, h, wg, wu, wd, idx, gf, final_norm):
    n, d = x2.shape
    d_ff = wg.shape[-1]
    tm = ROW_TILE
    once = pl.Buffered(1)
    return pl.pallas_call(
        functools.partial(_ffn_kernel, final_norm=final_norm),
        grid=(n // tm,),
        in_specs=[
            pl.BlockSpec((tm, d), lambda i: (i, 0)),
            pl.BlockSpec((tm, d), lambda i: (i, 0)),
            pl.BlockSpec((None, d, d_ff), lambda i: (idx, 0, 0), pipeline_mode=once),
            pl.BlockSpec((None, d, d_ff), lambda i: (idx, 0, 0), pipeline_mode=once),
            pl.BlockSpec((None, d_ff, d), lambda i: (idx, 0, 0), pipeline_mode=once),
            pl.BlockSpec((1, d), lambda i: (0, 0)),
        ],
        out_specs=pl.BlockSpec((tm, d), lambda i: (i, 0)),
        out_shape=jax.ShapeDtypeStruct((n, d), F32),
        scratch_shapes=[pltpu.VMEM((tm, d_ff), BF16)],
        compiler_params=_params("parallel"),
        name="ffn_dense",
    )(x2, h, wg, wu, wd, gf.reshape(1, d))


TOKEN_ROWS = SUBLANES
PERM_UNROLL = 4


def _rows_to_tiles(ref, val):
    m = val.shape[0]
    for j in range(TOKEN_ROWS):
        ref[pl.ds(j, m, stride=TOKEN_ROWS), :] = val[:, j * LANES:(j + 1) * LANES]


def _tiles_to_rows(ref):
    m = ref.shape[0] // TOKEN_ROWS
    return jnp.concatenate([ref[pl.ds(j, m, stride=TOKEN_ROWS), :] for j in range(TOKEN_ROWS)],
                           axis=-1)


def _token_copy(src, dst, sem, s, t):
    rows = lambda i: pl.ds(pl.multiple_of(i * TOKEN_ROWS, TOKEN_ROWS), TOKEN_ROWS)
    return pltpu.make_async_copy(src.at[rows(s), :], dst.at[rows(t), :], sem)


def _for_tokens(n_tok, fn):
    def body(i, carry):
        for j in range(PERM_UNROLL):
            fn(i * PERM_UNROLL + j, j % 2)
        return carry
    lax.fori_loop(0, n_tok // PERM_UNROLL, body, 0)


def _dispatch_kernel(zb_ref, nu_ref, p0_ref, p1_ref, h_ref, xs_ref, zero_scr, sem):
    n_tok = h_ref.shape[0] // TOKEN_ROWS
    n_exp = zb_ref.shape[0]

    @pl.when(pl.program_id(0) == 0)
    def _():
        zero_scr[...] = jnp.zeros(zero_scr.shape, F32)
        rows = zero_scr.shape[0]

        def fill(blk):
            start = pl.multiple_of(blk * rows, rows)
            return pltpu.make_async_copy(zero_scr, xs_ref.at[pl.ds(start, rows), :], sem)

        for e in range(n_exp):
            fill(zb_ref[e]).start()
        for e in range(n_exp):
            fill(zb_ref[e]).wait()

        def fill_unused(blk, carry):
            fill(blk).start()
            fill(blk).wait()
            return carry

        lax.fori_loop(nu_ref[0], xs_ref.shape[0] // rows, fill_unused, 0)

    def copies(r):
        return (_token_copy(h_ref, xs_ref, sem, r, p0_ref[0, 0, r]),
                _token_copy(h_ref, xs_ref, sem, r, p1_ref[0, 0, r]))

    _for_tokens(n_tok, lambda r, lane: [cp.start(priority=lane) for cp in copies(r)])
    _for_tokens(n_tok, lambda r, lane: [cp.wait() for cp in copies(r)])


def _dispatch(h, pos0, pos1, zero_blocks, n_used, n_sorted):
    n = h.shape[0] // TOKEN_ROWS
    tp = PERM_TILE
    smem_rows = lambda: pl.BlockSpec((1, 1, tp), lambda i, zb, nu: (i, 0, 0),
                                     memory_space=pltpu.SMEM)
    grid_spec = pltpu.PrefetchScalarGridSpec(
        num_scalar_prefetch=2,
        grid=(n // tp,),
        in_specs=[smem_rows(), smem_rows(),
                  pl.BlockSpec((tp * TOKEN_ROWS, LANES), lambda i, zb, nu: (i, 0))],
        out_specs=pl.BlockSpec(memory_space=pl.ANY),
        scratch_shapes=[pltpu.VMEM((MOE_TILE * TOKEN_ROWS, LANES), F32),
                        pltpu.SemaphoreType.DMA(())],
    )
    return pl.pallas_call(
        _dispatch_kernel,
        grid_spec=grid_spec,
        out_shape=jax.ShapeDtypeStruct((n_sorted * TOKEN_ROWS, LANES), F32),
        compiler_params=_params("arbitrary"),
        name="moe_dispatch",
    )(zero_blocks, n_used, pos0.reshape(n // tp, 1, tp), pos1.reshape(n // tp, 1, tp), h)


def _expert_kernel(be_ref, nu_ref, xs_ref, wg_ref, wu_ref, wd_ref, ys_ref, a_scr, wd_scr):
    @pl.when(pl.program_id(0) < nu_ref[0])
    def _():
        _rows_to_tiles(ys_ref, _swiglu_rows(_tiles_to_rows(xs_ref).astype(BF16), wg_ref, wu_ref,
                                            wd_ref, a_scr, wd_scr))

    @pl.when(pl.program_id(0) >= nu_ref[0])
    def _():
        ys_ref[...] = jnp.zeros(ys_ref.shape, F32)


def _experts(xs, wg, wu, wd, idx, block_expert, n_used):
    d, d_ff = wg.shape[-2:]
    tm = MOE_TILE
    tile = (tm * TOKEN_ROWS, LANES)
    once = pl.Buffered(1)
    grid_spec = pltpu.PrefetchScalarGridSpec(
        num_scalar_prefetch=2,
        grid=(xs.shape[0] // tile[0],),
        in_specs=[
            pl.BlockSpec(tile, lambda i, be, nu: (jnp.minimum(i, nu[0] - 1), 0)),
            pl.BlockSpec((None, None, d, d_ff), lambda i, be, nu: (idx, be[i], 0, 0),
                         pipeline_mode=once),
            pl.BlockSpec((None, None, d, d_ff), lambda i, be, nu: (idx, be[i], 0, 0),
                         pipeline_mode=once),
            pl.BlockSpec((None, None, d_ff, d), lambda i, be, nu: (idx, be[i], 0, 0),
                         pipeline_mode=once),
        ],
        out_specs=pl.BlockSpec(tile, lambda i, be, nu: (i, 0)),
        scratch_shapes=[pltpu.VMEM((tm, d_ff), BF16), pltpu.VMEM((d_ff, d), BF16)],
    )
    return pl.pallas_call(
        _expert_kernel,
        grid_spec=grid_spec,
        out_shape=jax.ShapeDtypeStruct(xs.shape, F32),
        compiler_params=_params("arbitrary"),
        name="moe_experts",
    )(block_expert, n_used, xs, wg, wu, wd)


def _combine_kernel(p0_ref, p1_ref, q0_ref, q1_ref, x_ref, route_ref, gf_ref, ys_ref, o_ref,
                    g_scr, sem, *, final_norm):
    n_tok = x_ref.shape[0]
    step = pl.program_id(0)
    slot = lax.rem(step, 2)

    def copies(pa_ref, pb_ref, sl):
        return lambda r: (
            _token_copy(ys_ref, g_scr.at[sl, 0], sem.at[sl], pa_ref[0, 0, r], r),
            _token_copy(ys_ref, g_scr.at[sl, 1], sem.at[sl], pb_ref[0, 0, r], r))

    def start(cps):
        _for_tokens(n_tok, lambda r, lane: [cp.start(priority=lane) for cp in cps(r)])

    @pl.when(step == 0)
    def _():
        start(copies(p0_ref, p1_ref, 0))

    @pl.when(step + 1 < pl.num_programs(0))
    def _():
        start(copies(q0_ref, q1_ref, 1 - slot))

    mine = copies(p0_ref, p1_ref, slot)
    _for_tokens(n_tok, lambda r, lane: [cp.wait() for cp in mine(r)])
    w1 = route_ref[:, 2:3]
    w2 = route_ref[:, 3:4]
    out = (x_ref[...] + w1 * _tiles_to_rows(g_scr.at[slot, 0])
           + w2 * _tiles_to_rows(g_scr.at[slot, 1]))
    if final_norm:
        out = _rms(out, gf_ref[...])
    o_ref[...] = out


def _combine(x2, route, ys, pos0, pos1, gf, final_norm):
    n, d = x2.shape
    tp = PERM_TILE
    n_steps = n // tp
    cur = lambda: pl.BlockSpec((1, 1, tp), lambda i: (i, 0, 0), memory_space=pltpu.SMEM)
    nxt = lambda: pl.BlockSpec((1, 1, tp), lambda i: (jnp.minimum(i + 1, n_steps - 1), 0, 0),
                               memory_space=pltpu.SMEM)
    pos0, pos1 = pos0.reshape(n_steps, 1, tp), pos1.reshape(n_steps, 1, tp)
    return pl.pallas_call(
        functools.partial(_combine_kernel, final_norm=final_norm),
        grid=(n_steps,),
        in_specs=[cur(), cur(), nxt(), nxt(),
                  pl.BlockSpec((tp, d), lambda i: (i, 0)),
                  pl.BlockSpec((tp, LANES), lambda i: (i, 0)),
                  pl.BlockSpec((1, d), lambda i: (0, 0)),
                  pl.BlockSpec(memory_space=pl.ANY)],
        out_specs=pl.BlockSpec((tp, d), lambda i: (i, 0)),
        out_shape=jax.ShapeDtypeStruct((n, d), F32),
        scratch_shapes=[pltpu.VMEM((2, 2, tp * TOKEN_ROWS, LANES), F32),
                        pltpu.SemaphoreType.DMA((2,))],
        compiler_params=_params("arbitrary"),
        name="moe_combine",
    )(pos0, pos1, pos0, pos1, x2, route, gf.reshape(1, d), ys)


def _routing_tables(route, n_exp):
    n = route.shape[0]
    tm = MOE_TILE
    i1 = route[:, 0].astype(jnp.int32)
    i2 = route[:, 1].astype(jnp.int32)
    oh1 = jax.nn.one_hot(i1, n_exp, dtype=jnp.int32)
    oh2 = jax.nn.one_hot(i2, n_exp, dtype=jnp.int32)
    tot = oh1 + oh2
    csum = jnp.cumsum(tot, axis=0)
    excl = csum - tot
    rank1 = jnp.sum(excl * oh1, axis=-1)
    rank2 = jnp.sum((excl + oh1) * oh2, axis=-1)
    counts = csum[-1]
    nblk = (counts + tm - 1) // tm
    blk_end = jnp.cumsum(nblk)
    blk_off = blk_end - nblk
    row_off = blk_off * tm
    pos0 = row_off[i1] + rank1
    pos1 = row_off[i2] + rank2
    n_blocks = TOP_K * n // tm + n_exp
    n_used = blk_end[-1]
    jb = jnp.minimum(jnp.arange(n_blocks, dtype=jnp.int32), n_used - 1)
    block_expert = jnp.sum((blk_end[None, :] <= jb[:, None]).astype(jnp.int32), axis=-1)
    block_expert = jnp.minimum(block_expert, n_exp - 1)
    zero_blocks = jnp.where(nblk > 0, blk_end - 1, n_blocks - 1).astype(jnp.int32)
    return (pos0.astype(jnp.int32), pos1.astype(jnp.int32), block_expert,
            n_used.astype(jnp.int32).reshape(1), zero_blocks, n_blocks * tm)


def _moe(x2, h, route, wg, wu, wd, idx, gf, final_norm):
    n_exp = wg.shape[1]
    pos0, pos1, block_expert, n_used, zero_blocks, n_sorted = _routing_tables(route, n_exp)
    xs = _dispatch(h, pos0, pos1, zero_blocks, n_used, n_sorted)
    ys = _experts(xs, wg, wu, wd, idx, block_expert, n_used)
    return _combine(x2, route, ys, pos0, pos1, gf, final_norm)


def kernel(x, norm_mix_g, w_in, conv_dw_w, conv_dw_b, conv_ln_g, conv_ln_b, w_conv_proj,
           ssm_a_re, ssm_a_im, ssm_log_dt, ssm_b_re, ssm_b_im, ssm_c_re, ssm_c_im, ssm_d,
           ssm_w_glu, ssm_b_glu, w_ssm_proj, w_out, norm_ffn_g,
           ffn_w_gate, ffn_w_up, ffn_w_down, router_w, router_b,
           moe_w_gate, moe_w_up, moe_w_down, final_norm_g):
    bsz, seq, d = x.shape
    depth = w_in.shape[0]
    d_conv = conv_dw_w.shape[-1]
    d_ssm = ssm_d.shape[-1]
    n = bsz * seq
    bf = lambda a: a.astype(BF16)
    x2 = x.reshape(n, d).astype(F32)
    m_a, m_b, coef = _s5_matrices(ssm_a_re, ssm_a_im, ssm_log_dt, ssm_b_re, ssm_b_im,
                                  ssm_c_re, ssm_c_im, ssm_d)
    mix_g, ffn_g = _rows3(norm_mix_g), _rows3(norm_ffn_g)
    dw_b, ln_g, ln_b = _rows3(conv_dw_b), _rows3(conv_ln_g), _rows3(conv_ln_b)
    b_glu = _rows3(ssm_b_glu)
    w_in_bf, w_glu_bf, w_sp_bf = bf(w_in), bf(ssm_w_glu), bf(w_ssm_proj)
    w_cp_bf, w_out_bf = bf(w_conv_proj), bf(w_out)
    ffn_bf = (bf(ffn_w_gate), bf(ffn_w_up), bf(ffn_w_down))
    moe_w = (moe_w_gate, moe_w_up, moe_w_down)
    for layer in range(depth):
        last = layer == depth - 1
        ch, u, gates = _inproj(x2, mix_g, w_in_bf, layer, bsz, d_conv, d_ssm)
        hc = _conv_branch(ch.reshape(bsz, seq, d_conv), conv_dw_w, dw_b, ln_g, ln_b,
                          layer).reshape(n, d_conv)
        y = _s5_branch(u, m_a, m_b, coef, layer)
        i = layer // 2
        moe = layer % 2 == 1
        router = (router_w[i], router_b[i]) if moe else None
        outs = _merge(x2, hc, y, gates, layer, w_glu_bf, b_glu, w_sp_bf, w_cp_bf, w_out_bf,
                      ffn_g, router)
        if moe:
            x2, h, route = outs
            x2 = _moe(x2, h, route, *moe_w, i, final_norm_g, last)
        else:
            x2, h = outs
            x2 = _ffn_dense(x2, h, *ffn_bf, i, final_norm_g, last)
    return x2.reshape(bsz, seq, d).astype(x.dtype)
```

```python
import functools
import math

import jax
import jax.numpy as jnp
from jax import lax
from jax.experimental import pallas as pl
from jax.experimental.pallas import tpu as pltpu

F32 = jnp.float32
BF16 = jnp.bfloat16

RMS_EPS = 1e-6
LN_EPS = 1e-5
CONV_WIDTH = 31
CONV_HALF = CONV_WIDTH // 2
SSM_GROUP = 16
SSM_STATE = 64
CHUNK = 16
TOP_K = 2
LANES = 128
SUBLANES = 8
VMEM_LIMIT = 56 * 1024 * 1024

ROW_TILE = 512
MOE_TILE = 256
PERM_TILE = 512
CONV_TILE = 128
FF_CHUNK = 256


def _params(*sem):
    return pltpu.CompilerParams(dimension_semantics=sem, vmem_limit_bytes=VMEM_LIMIT)


def _rms(xf, g):
    ms = jnp.mean(xf * xf, axis=-1, keepdims=True)
    return xf * lax.rsqrt(ms + RMS_EPS) * g


def _stacked(arr, *lead):
    rest = arr.shape[len(lead):]
    return pl.BlockSpec((None,) * len(lead) + rest, lambda *_: tuple(lead) + (0,) * len(rest))


def _rows3(a):
    return a.reshape(a.shape[0], 1, a.shape[1])


def _inproj_kernel(x_ref, g_ref, w_ref, ch_ref, u_ref, gate_ref):
    dc = ch_ref.shape[-1]
    ds_ = u_ref.shape[0] * LANES
    h = _rms(x_ref[...], g_ref[...]).astype(BF16)
    v = jnp.dot(h, w_ref[:, 0:dc], preferred_element_type=F32)
    gt = jnp.dot(h, w_ref[:, dc:2 * dc], preferred_element_type=F32)
    ch_ref[...] = v * jax.nn.sigmoid(gt)
    u = jnp.dot(h, w_ref[:, 2 * dc:2 * dc + ds_], preferred_element_type=F32)
    for q in range(u_ref.shape[0]):
        u_ref[q, 0] = u[:, q * LANES:(q + 1) * LANES]
    gate_ref[...] = jnp.dot(h, w_ref[:, 2 * dc + ds_:], preferred_element_type=F32).astype(BF16)


def _inproj(x2, g, w_bf, layer, bsz, d_conv, d_ssm):
    n, d = x2.shape
    seq = n // bsz
    d_in = w_bf.shape[-1]
    d_gate = d_in - 2 * d_conv - d_ssm
    tm = ROW_TILE
    per_seq = seq // tm
    nq = d_ssm // LANES
    return pl.pallas_call(
        _inproj_kernel,
        grid=(n // tm,),
        in_specs=[
            pl.BlockSpec((tm, d), lambda i: (i, 0)),
            _stacked(g, layer),
            _stacked(w_bf, layer),
        ],
        out_specs=[
            pl.BlockSpec((tm, d_conv), lambda i: (i, 0)),
            pl.BlockSpec((nq, 1, tm, LANES), lambda i: (0, i // per_seq, i % per_seq, 0)),
            pl.BlockSpec((tm, d_gate), lambda i: (i, 0)),
        ],
        out_shape=[
            jax.ShapeDtypeStruct((n, d_conv), F32),
            jax.ShapeDtypeStruct((nq, bsz, seq, LANES), F32),
            jax.ShapeDtypeStruct((n, d_gate), BF16),
        ],
        compiler_params=_params("parallel"),
        name="inproj",
    )(x2, g, w_bf)


def _conv_kernel(h_ref, w_ref, b_ref, lg_ref, lb_ref, o_ref, pad_scr, tile_scr):
    seq, c = h_ref.shape[1], h_ref.shape[2]
    front = 2 * SUBLANES
    zeros = jnp.zeros((front, c), F32)
    pad_scr[0:front, :] = zeros
    pad_scr[front + seq:front + seq + front, :] = zeros
    pad_scr[front:front + seq, :] = h_ref[0]

    phases = [[] for _ in range(SUBLANES)]
    for k in range(CONV_WIDTH):
        off = k - CONV_HALF + front
        phases[off % SUBLANES].append((k, off - off % SUBLANES))

    def tile_body(ti, carry):
        base = pl.multiple_of(ti * CONV_TILE, CONV_TILE)
        win_ref = pad_scr.at[pl.ds(base, CONV_TILE + 2 * front), :]
        for lb in range(c // LANES):
            ls = slice(lb * LANES, (lb + 1) * LANES)
            acc = b_ref[:, ls]
            for s, taps in enumerate(phases):
                part = None
                for k, row0 in taps:
                    term = w_ref[k:k + 1, ls] * win_ref[pl.ds(row0, CONV_TILE + SUBLANES), ls]
                    part = term if part is None else part + term
                if part is not None:
                    acc = acc + part[s:s + CONV_TILE, :]
            tile_scr[:, ls] = acc
        hh = tile_scr[...]
        mu = jnp.mean(hh, axis=-1, keepdims=True)
        cen = hh - mu
        var = jnp.mean(cen * cen, axis=-1, keepdims=True)
        y = cen * lax.rsqrt(var + LN_EPS) * lg_ref[...] + lb_ref[...]
        o_ref[0, pl.ds(base, CONV_TILE), :] = (y * jax.nn.sigmoid(y)).astype(BF16)
        return carry

    lax.fori_loop(0, seq // CONV_TILE, tile_body, 0)


def _conv_branch(ch, dw_w, dw_b, ln_g, ln_b, layer):
    b, seq, c = ch.shape
    front = 2 * SUBLANES
    return pl.pallas_call(
        _conv_kernel,
        grid=(b,),
        in_specs=[
            pl.BlockSpec((1, seq, c), lambda i: (i, 0, 0)),
            _stacked(dw_w, layer), _stacked(dw_b, layer), _stacked(ln_g, layer),
            _stacked(ln_b, layer),
        ],
        out_specs=pl.BlockSpec((1, seq, c), lambda i: (i, 0, 0)),
        out_shape=jax.ShapeDtypeStruct((b, seq, c), BF16),
        scratch_shapes=[
            pltpu.VMEM((seq + 2 * front, c), F32),
            pltpu.VMEM((CONV_TILE, c), F32),
        ],
        compiler_params=_params("parallel"),
        name="conv_branch",
    )(ch, dw_w, dw_b, ln_g, ln_b)


def _s5_matrices(a_re, a_im, log_dt, b_re, b_im, c_re, c_im, d_skip):
    t = CHUNK
    n_layers, n_dir, g, p = a_re.shape
    c = b_re.shape[-1]
    f = lambda a: a.astype(F32)
    cat = lambda *xs: jnp.concatenate(xs, axis=-1)
    a_re, a_im, c_re, c_im = f(a_re), f(a_im), f(c_re), f(c_im)
    dt = jnp.exp(f(log_dt))[..., None]
    lam_re, lam_im = a_re * dt, a_im * dt
    up, down = jnp.arange(t, dtype=F32), jnp.arange(t - 1, -1, -1, dtype=F32)
    expo = jnp.stack([cat(down, up + 1.0, up, jnp.full((1,), t, F32)),
                      cat(up, down + 1.0, down, jnp.full((1,), t, F32))])
    expo = expo[None, :, None, :, None]
    mag = jnp.exp(expo * lam_re[:, :, :, None, :])
    pw_re = mag * jnp.cos(expo * lam_im[:, :, :, None, :])
    pw_im = mag * jnp.sin(expo * lam_im[:, :, :, None, :])
    ab_re = jnp.exp(lam_re) * jnp.cos(lam_im)
    ab_im = jnp.exp(lam_re) * jnp.sin(lam_im)
    n_re, n_im = ab_re - 1.0, ab_im
    den = a_re * a_re + a_im * a_im
    q_re = ((n_re * a_re + n_im * a_im) / den)[:, :, :, None, :]
    q_im = ((n_im * a_re - n_re * a_im) / den)[:, :, :, None, :]
    bt_re = jnp.swapaxes(f(b_re), -1, -2)[:, None]
    bt_im = jnp.swapaxes(f(b_im), -1, -2)[:, None]
    bb_re = q_re * bt_re - q_im * bt_im
    bb_im = q_re * bt_im + q_im * bt_re

    def powers(lo, hi_, reps):
        sl = lambda w: jnp.tile(w[:, :, :, lo:hi_], (1, 1, 1, 1, reps))[:, :, :, :, None, :]
        return sl(pw_re), sl(pw_im)

    kr, ki = powers(0, t, 4)
    z = (kr * cat(bb_re, bb_im, bb_im, bb_re)[:, :, :, None]
         + ki * cat(-bb_im, bb_re, bb_re, -bb_im)[:, :, :, None])
    m_a = jnp.swapaxes(z.reshape(n_layers, n_dir, g, t * c, 4 * p), 1, 2).astype(BF16)
    kr, ki = powers(t, 2 * t, 2)
    v = (kr * cat(c_re, -c_im)[:, :, :, None] + ki * cat(-c_im, -c_re)[:, :, :, None])
    m_o = jnp.swapaxes(v.reshape(n_layers, n_dir, g, t * c, 2 * p), -1, -2)
    pr = jnp.swapaxes(pw_re[:, :, :, 2 * t:3 * t], -1, -2)[..., None]
    pi = jnp.swapaxes(pw_im[:, :, :, 2 * t:3 * t], -1, -2)[..., None]
    cp_re = jnp.swapaxes(c_re, -1, -2)[:, :, :, :, None, :]
    cp_im = jnp.swapaxes(c_im, -1, -2)[:, :, :, :, None, :]
    vp_re = (pr * cp_re - pi * cp_im)[:, :, :, :, None]
    vp_im = (pr * cp_im + pi * cp_re)[:, :, :, :, None]
    bp_re = jnp.swapaxes(bb_re, -1, -2)[..., None, None]
    bp_im = jnp.swapaxes(bb_im, -1, -2)[..., None, None]
    kk = jnp.sum(bp_re * vp_re - bp_im * vp_im, axis=3)
    kf, kb = kk[:, 0], kk[:, 1]
    diag = kf[:, :, :, 0] + kb[:, :, :, t - 1] + (f(d_skip).reshape(n_layers, g, 1, c)
                                                  * jnp.eye(c, dtype=F32))
    lags = jnp.concatenate([kb[:, :, :, :t - 1], diag[:, :, :, None], kf[:, :, :, 1:]], axis=3)
    lags = lags.reshape(n_layers, g, c, (2 * t - 1) * c)
    m_o = jnp.concatenate([m_o[:, 0], m_o[:, 1]], axis=2).astype(BF16)
    at_re, at_im = pw_re[:, :, :, 3 * t], pw_im[:, :, :, 3 * t]
    rows = [cat(at_re[:, 0], at_re[:, 0]), cat(-at_im[:, 0], at_im[:, 0]),
            cat(at_re[:, 1], at_re[:, 1]), cat(-at_im[:, 1], at_im[:, 1])]
    coef = jnp.stack(rows + [jnp.zeros_like(rows[0])] * (SUBLANES - len(rows)), axis=2)
    return m_a, lags, m_o, coef


def _block_transpose(v):
    nblk = len(v)
    blk = lax.broadcasted_iota(jnp.int32, v[0].shape, 1) // SSM_GROUP
    d = nblk // 2
    while d >= 1:
        keep_lo = (blk & d) == 0
        out = list(v)
        for r in range(nblk):
            if r & d:
                continue
            lo, hi = v[r], v[r + d]
            out[r] = jnp.where(keep_lo, lo, pltpu.roll(hi, d * SSM_GROUP, 1))
            out[r + d] = jnp.where(keep_lo, pltpu.roll(lo, LANES - d * SSM_GROUP, 1), hi)
        v = out
        d //= 2
    return v


def _s5_kernel(u_ref, ma_ref, lag_ref, mo_ref, coef_ref, y_ref, x_scr, z_scr, cin_scr, mi_scr):
    nb, seq = u_ref.shape[1], u_ref.shape[2]
    gpl, halves, rows = x_scr.shape[0:3]
    kx = halves * LANES
    sw = coef_ref.shape[-1]
    n_chunks = seq // CHUNK
    cpt = SUBLANES // nb
    n_tiles = rows // SUBLANES
    span = SUBLANES * CHUNK
    pos_per_half = LANES // SSM_GROUP

    def relayout(co, to_chunks):
        tok0 = pl.multiple_of(co * span, span)
        row0 = pl.multiple_of(co * SUBLANES * nb, SUBLANES * nb)
        for b in range(nb):
            tok_ref = (u_ref if to_chunks else y_ref).at[0, b, pl.ds(tok0, span), :]
            for h in range(halves):
                chunk_refs = [x_scr.at[gp, h, pl.ds(row0, SUBLANES * nb), :] for gp in range(gpl)]
                tok_rows = [pl.ds(h * pos_per_half + ip, SUBLANES, stride=CHUNK)
                            for ip in range(pos_per_half)]
                chunk_rows = pl.ds(b, SUBLANES, stride=nb)
                if to_chunks:
                    dst = _block_transpose([tok_ref[r, :] for r in tok_rows])
                    for gp in range(gpl):
                        chunk_refs[gp][chunk_rows, :] = dst[gp]
                else:
                    dst = _block_transpose([chunk_refs[gp][chunk_rows, :] for gp in range(gpl)])
                    for ip in range(pos_per_half):
                        tok_ref[tok_rows[ip], :] = dst[ip]

    def relayout_in(co, carry):
        relayout(co, True)
        return carry

    def relayout_out(co, carry):
        relayout(co, False)
        return carry

    lax.fori_loop(0, n_chunks // SUBLANES, relayout_in, 0)

    rc = min(rows, 512)
    row_id = lax.broadcasted_iota(jnp.int32, (SUBLANES, sw), 0)
    x_rows = lambda gp, r0: jnp.concatenate(
        [x_scr[gp, h, r0:r0 + rc, :] for h in range(halves)], axis=-1).astype(BF16)
    for gp in range(gpl):
        for r0 in range(0, rows, rc):
            xr = x_rows(gp, r0)
            for dr in range(ma_ref.shape[1]):
                z_scr[r0:r0 + rc, 2 * dr * sw:2 * (dr + 1) * sw] = jnp.dot(
                    xr, ma_ref[gp, dr], preferred_element_type=F32)
        coef = coef_ref[gp]
        bc = lambda r: jnp.broadcast_to(coef[r:r + 1, :], (SUBLANES, sw))
        af1, af2, ab1, ab2 = bc(0), bc(1), bc(2), bc(3)

        def tile_step(ti, carry):
            sf, sfp, sb, sbp = carry
            rf = pl.multiple_of(ti * SUBLANES, SUBLANES)
            rb = pl.multiple_of((n_tiles - 1 - ti) * SUBLANES, SUBLANES)
            zf = z_scr[pl.ds(rf, SUBLANES), 0:sw]
            zfp = z_scr[pl.ds(rf, SUBLANES), sw:2 * sw]
            zb = z_scr[pl.ds(rb, SUBLANES), 2 * sw:3 * sw]
            zbp = z_scr[pl.ds(rb, SUBLANES), 3 * sw:4 * sw]
            cin_f, cin_b = sf, sb
            for k in range(cpt):
                nf, nfp = af1 * sf + af2 * sfp + zf, af1 * sfp - af2 * sf + zfp
                sf, sfp = pltpu.roll(nf, nb, 0), pltpu.roll(nfp, nb, 0)
                kb = cpt - 1 - k
                nbk, nbkp = ab1 * sb + ab2 * sbp + zb, ab1 * sbp - ab2 * sb + zbp
                sb, sbp = pltpu.roll(nbk, SUBLANES - nb, 0), pltpu.roll(nbkp, SUBLANES - nb, 0)
                if k < cpt - 1:
                    cin_f = jnp.where(row_id >= (k + 1) * nb, sf, cin_f)
                    cin_b = jnp.where(row_id < kb * nb, sb, cin_b)
            cin_scr[pl.ds(rf, SUBLANES), 0:sw] = cin_f
            cin_scr[pl.ds(rb, SUBLANES), sw:2 * sw] = cin_b
            return sf, sfp, sb, sbp

        zero = jnp.zeros((SUBLANES, sw), F32)
        lax.fori_loop(0, n_tiles, tile_step, (zero, zero, zero, zero), unroll=2)

        lag = lag_ref[gp]
        for j in range(CHUNK):
            first = (CHUNK - 1 - j) * SSM_GROUP
            mi_scr[j * SSM_GROUP:(j + 1) * SSM_GROUP, :] = lag[:, first:first + kx].astype(BF16)

        for r0 in range(0, rows, rc):
            y = (jnp.dot(x_rows(gp, r0), mi_scr[...], preferred_element_type=F32)
                 + jnp.dot(cin_scr[r0:r0 + rc, :].astype(BF16), mo_ref[gp],
                           preferred_element_type=F32))
            for h in range(halves):
                x_scr[gp, h, r0:r0 + rc, :] = y[:, h * LANES:(h + 1) * LANES]

    lax.fori_loop(0, n_chunks // SUBLANES, relayout_out, 0)


def _s5_branch(u, m_a, lags, m_o, coef, layer):
    nq, b, seq, _ = u.shape
    g = m_a.shape[1]
    gpl = g // nq
    n_chunks = seq // CHUNK
    assert SUBLANES % b == 0 and n_chunks % SUBLANES == 0 and gpl * SSM_GROUP == LANES
    rows = n_chunks * b
    n_dir, kx, ks = m_a.shape[2], m_a.shape[3], m_o.shape[2]
    nz = n_dir * m_a.shape[4]
    return pl.pallas_call(
        _s5_kernel,
        grid=(nq,),
        in_specs=[
            pl.BlockSpec((1, b, seq, LANES), lambda i: (i, 0, 0, 0), pipeline_mode=pl.Buffered(1)),
            pl.BlockSpec((None, gpl, n_dir, kx, nz // n_dir), lambda i: (layer, i, 0, 0, 0)),
            pl.BlockSpec((None, gpl) + lags.shape[2:], lambda i: (layer, i, 0, 0)),
            pl.BlockSpec((None, gpl, ks, kx), lambda i: (layer, i, 0, 0)),
            pl.BlockSpec((None, gpl, SUBLANES, coef.shape[-1]), lambda i: (layer, i, 0, 0)),
        ],
        out_specs=pl.BlockSpec((1, b, seq, LANES), lambda i: (i, 0, 0, 0)),
        out_shape=jax.ShapeDtypeStruct((nq, b, seq, LANES), F32),
        scratch_shapes=[
            pltpu.VMEM((gpl, kx // LANES, rows, LANES), F32),
            pltpu.VMEM((rows, nz), F32),
            pltpu.VMEM((rows, ks), F32),
            pltpu.VMEM((kx, kx), BF16),
        ],
        compiler_params=_params("parallel"),
        name="s5_scan",
    )(u, m_a, lags, m_o, coef)


def _merge_kernel(x_ref, hc_ref, y_ref, gate_ref, wglu_ref, bglu_ref, ws_ref, wc_ref,
                  wout_ref, gn_ref, *rest, moe):
    if moe:
        rw_ref, rb_ref, xo_ref, h_ref, route_ref = rest
    else:
        xo_ref, h_ref = rest
    d = x_ref.shape[-1]
    yg = jax.nn.gelu(jnp.concatenate([y_ref[q, 0] for q in range(y_ref.shape[0])], axis=-1))
    glu = jnp.dot(yg.astype(BF16), wglu_ref[...], preferred_element_type=F32) + bglu_ref[...]
    y2 = yg * jax.nn.sigmoid(glu)
    br_s = jnp.dot(y2.astype(BF16), ws_ref[...], preferred_element_type=F32)
    br_c = jnp.dot(hc_ref[...], wc_ref[...], preferred_element_type=F32)
    gc = gate_ref[:, 0:d].astype(F32)
    gs = gate_ref[:, d:2 * d].astype(F32)
    merged = jax.nn.sigmoid(gc) * br_c + jax.nn.sigmoid(gs) * br_s
    xn = x_ref[...] + jnp.dot(merged.astype(BF16), wout_ref[...], preferred_element_type=F32)
    xo_ref[...] = xn
    h = _rms(xn, gn_ref[...])
    if not moe:
        h_ref[...] = h.astype(h_ref.dtype)
    if moe:
        _rows_to_tiles(h_ref, h)
        h_hi = h.astype(BF16)
        h_lo = (h - h_hi.astype(F32)).astype(BF16)
        logits = (jnp.dot(h_hi, rw_ref[0], preferred_element_type=F32)
                  + jnp.dot(h_hi, rw_ref[1], preferred_element_type=F32)
                  + jnp.dot(h_lo, rw_ref[0], preferred_element_type=F32)) + rb_ref[1:2, :]
        lane = lax.broadcasted_iota(jnp.int32, logits.shape, 1).astype(F32)
        valid = rb_ref[0:1, :] > 0.0
        neg = jnp.float32(-jnp.inf)
        logits = jnp.where(valid, logits, neg)
        m1 = jnp.max(logits, axis=-1, keepdims=True)
        i1 = jnp.min(jnp.where(logits == m1, lane, float(LANES)), axis=-1, keepdims=True)
        rest_l = jnp.where(lane == i1, neg, logits)
        m2 = jnp.max(rest_l, axis=-1, keepdims=True)
        i2 = jnp.min(jnp.where(rest_l == m2, lane, float(LANES)), axis=-1, keepdims=True)
        e2 = jnp.exp(m2 - m1)
        w1 = 1.0 / (1.0 + e2)
        w2 = e2 / (1.0 + e2)
        route_ref[...] = jnp.where(lane == 0.0, i1, jnp.where(lane == 1.0, i2,
                                   jnp.where(lane == 2.0, w1, jnp.where(lane == 3.0, w2, 0.0))))


def _merge(x2, hc, y, gates, layer, wglu, bglu, ws, wc, wout, gn, router=None):
    n, d = x2.shape
    ds_ = hc.shape[1]
    nq, _, seq, _ = y.shape
    tm = ROW_TILE
    per_seq = seq // tm
    moe = router is not None
    row = lambda w: pl.BlockSpec((tm, w), lambda i: (i, 0))
    full = lambda a: pl.BlockSpec(a.shape, lambda i: (0,) * a.ndim)
    y_spec = pl.BlockSpec((nq, 1, tm, LANES), lambda i: (0, i // per_seq, i % per_seq, 0))
    ins = [x2, hc, y, gates, wglu, bglu, ws, wc, wout, gn]
    in_specs = ([row(d), row(ds_), y_spec, row(gates.shape[1])]
                + [_stacked(a, layer) for a in ins[4:]])
    out_specs = [row(d), row(d)]
    out_shape = [jax.ShapeDtypeStruct((n, d), F32), jax.ShapeDtypeStruct((n, d), BF16)]
    if moe:
        tr = d // LANES
        out_specs[1] = pl.BlockSpec((tm * tr, LANES), lambda i: (i, 0))
        out_shape[1] = jax.ShapeDtypeStruct((n * tr, LANES), F32)
        rw, rb = router
        n_exp = rw.shape[1]
        rw_f = jnp.zeros((d, LANES), F32).at[:, :n_exp].set(rw.astype(F32))
        rw_hi = rw_f.astype(BF16)
        rw_pad = jnp.stack([rw_hi, (rw_f - rw_hi.astype(F32)).astype(BF16)])
        rb_pad = jnp.zeros((SUBLANES, LANES), F32).at[0, :n_exp].set(1.0).at[1, :n_exp].set(rb.astype(F32))
        ins += [rw_pad, rb_pad]
        in_specs += [full(rw_pad), full(rb_pad)]
        out_specs.append(row(LANES))
        out_shape.append(jax.ShapeDtypeStruct((n, LANES), F32))
    return pl.pallas_call(
        functools.partial(_merge_kernel, moe=moe),
        grid=(n // tm,),
        in_specs=in_specs,
        out_specs=out_specs,
        out_shape=out_shape,
        compiler_params=_params("parallel"),
        name="merge_moe" if moe else "merge",
    )(*ins)


def _swiglu_rows(h_bf, wg_ref, wu_ref, wd_ref, a_scr, wd_scr=None):
    d_ff = a_scr.shape[1]
    for c0 in range(0, d_ff, FF_CHUNK):
        cs = slice(c0, c0 + FF_CHUNK)
        gt = jnp.dot(h_bf, wg_ref[:, cs].astype(BF16), preferred_element_type=F32)
        up = jnp.dot(h_bf, wu_ref[:, cs].astype(BF16), preferred_element_type=F32)
        a_scr[:, cs] = (gt * jax.nn.sigmoid(gt) * up).astype(BF16)
        if wd_scr is not None:
            wd_scr[cs, :] = wd_ref[cs, :].astype(BF16)
    wd = wd_ref[...] if wd_scr is None else wd_scr[...]
    return jnp.dot(a_scr[...], wd, preferred_element_type=F32)


def _ffn_kernel(x_ref, h_ref, wg_ref, wu_ref, wd_ref, gf_ref, o_ref, a_scr, *, final_norm):
    out = x_ref[...] + _swiglu_rows(h_ref[...], wg_ref, wu_ref, wd_ref, a_scr)
    if final_norm:
        out = _rms(out, gf_ref[...])
    o_ref[...] = out


def _ffn_dense(x2, h, wg, wu, wd, idx, gf, final_norm):
    n, d = x2.shape
    d_ff = wg.shape[-1]
    tm = ROW_TILE
    once = pl.Buffered(1)
    return pl.pallas_call(
        functools.partial(_ffn_kernel, final_norm=final_norm),
        grid=(n // tm,),
        in_specs=[
            pl.BlockSpec((tm, d), lambda i: (i, 0)),
            pl.BlockSpec((tm, d), lambda i: (i, 0)),
            pl.BlockSpec((None, d, d_ff), lambda i: (idx, 0, 0), pipeline_mode=once),
            pl.BlockSpec((None, d, d_ff), lambda i: (idx, 0, 0), pipeline_mode=once),
            pl.BlockSpec((None, d_ff, d), lambda i: (idx, 0, 0), pipeline_mode=once),
            pl.BlockSpec((1, d), lambda i: (0, 0)),
        ],
        out_specs=pl.BlockSpec((tm, d), lambda i: (i, 0)),
        out_shape=jax.ShapeDtypeStruct((n, d), F32),
        scratch_shapes=[pltpu.VMEM((tm, d_ff), BF16)],
        compiler_params=_params("parallel"),
        name="ffn_dense",
    )(x2, h, wg, wu, wd, gf.reshape(1, d))


TOKEN_ROWS = SUBLANES
PERM_UNROLL = 4


def _rows_to_tiles(ref, val):
    m = val.shape[0]
    for j in range(TOKEN_ROWS):
        ref[pl.ds(j, m, stride=TOKEN_ROWS), :] = val[:, j * LANES:(j + 1) * LANES]


def _tiles_to_rows(ref):
    m = ref.shape[0] // TOKEN_ROWS
    return jnp.concatenate([ref[pl.ds(j, m, stride=TOKEN_ROWS), :] for j in range(TOKEN_ROWS)],
                           axis=-1)


def _token_copy(src, dst, sem, s, t):
    rows = lambda i: pl.ds(pl.multiple_of(i * TOKEN_ROWS, TOKEN_ROWS), TOKEN_ROWS)
    return pltpu.make_async_copy(src.at[rows(s), :], dst.at[rows(t), :], sem)


def _for_tokens(n_tok, fn):
    def body(i, carry):
        for j in range(PERM_UNROLL):
            fn(i * PERM_UNROLL + j, j % 2)
        return carry
    lax.fori_loop(0, n_tok // PERM_UNROLL, body, 0)


def _dispatch_kernel(zb_ref, nu_ref, p0_ref, p1_ref, h_ref, xs_ref, zero_scr, sem):
    n_tok = h_ref.shape[0] // TOKEN_ROWS
    n_exp = zb_ref.shape[0]

    @pl.when(pl.program_id(0) == 0)
    def _():
        zero_scr[...] = jnp.zeros(zero_scr.shape, F32)
        rows = zero_scr.shape[0]

        def fill(blk):
            start = pl.multiple_of(blk * rows, rows)
            return pltpu.make_async_copy(zero_scr, xs_ref.at[pl.ds(start, rows), :], sem)

        for e in range(n_exp):
            fill(zb_ref[e]).start()
        for e in range(n_exp):
            fill(zb_ref[e]).wait()

        def fill_unused(blk, carry):
            fill(blk).start()
            fill(blk).wait()
            return carry

        lax.fori_loop(nu_ref[0], xs_ref.shape[0] // rows, fill_unused, 0)

    def copies(r):
        return (_token_copy(h_ref, xs_ref, sem, r, p0_ref[0, 0, r]),
                _token_copy(h_ref, xs_ref, sem, r, p1_ref[0, 0, r]))

    _for_tokens(n_tok, lambda r, lane: [cp.start(priority=lane) for cp in copies(r)])
    _for_tokens(n_tok, lambda r, lane: [cp.wait() for cp in copies(r)])


def _dispatch(h, pos0, pos1, zero_blocks, n_used, n_sorted):
    n = h.shape[0] // TOKEN_ROWS
    tp = PERM_TILE
    smem_rows = lambda: pl.BlockSpec((1, 1, tp), lambda i, zb, nu: (i, 0, 0),
                                     memory_space=pltpu.SMEM)
    grid_spec = pltpu.PrefetchScalarGridSpec(
        num_scalar_prefetch=2,
        grid=(n // tp,),
        in_specs=[smem_rows(), smem_rows(),
                  pl.BlockSpec((tp * TOKEN_ROWS, LANES), lambda i, zb, nu: (i, 0))],
        out_specs=pl.BlockSpec(memory_space=pl.ANY),
        scratch_shapes=[pltpu.VMEM((MOE_TILE * TOKEN_ROWS, LANES), F32),
                        pltpu.SemaphoreType.DMA(())],
    )
    return pl.pallas_call(
        _dispatch_kernel,
        grid_spec=grid_spec,
        out_shape=jax.ShapeDtypeStruct((n_sorted * TOKEN_ROWS, LANES), F32),
        compiler_params=_params("arbitrary"),
        name="moe_dispatch",
    )(zero_blocks, n_used, pos0.reshape(n // tp, 1, tp), pos1.reshape(n // tp, 1, tp), h)


def _expert_kernel(be_ref, nu_ref, xs_ref, wg_ref, wu_ref, wd_ref, ys_ref, a_scr, wd_scr):
    @pl.when(pl.program_id(0) < nu_ref[0])
    def _():
        _rows_to_tiles(ys_ref, _swiglu_rows(_tiles_to_rows(xs_ref).astype(BF16), wg_ref, wu_ref,
                                            wd_ref, a_scr, wd_scr))

    @pl.when(pl.program_id(0) >= nu_ref[0])
    def _():
        ys_ref[...] = jnp.zeros(ys_ref.shape, F32)


def _experts(xs, wg, wu, wd, idx, block_expert, n_used):
    d, d_ff = wg.shape[-2:]
    tm = MOE_TILE
    tile = (tm * TOKEN_ROWS, LANES)
    once = pl.Buffered(1)
    grid_spec = pltpu.PrefetchScalarGridSpec(
        num_scalar_prefetch=2,
        grid=(xs.shape[0] // tile[0],),
        in_specs=[
            pl.BlockSpec(tile, lambda i, be, nu: (jnp.minimum(i, nu[0] - 1), 0)),
            pl.BlockSpec((None, None, d, d_ff), lambda i, be, nu: (idx, be[i], 0, 0),
                         pipeline_mode=once),
            pl.BlockSpec((None, None, d, d_ff), lambda i, be, nu: (idx, be[i], 0, 0),
                         pipeline_mode=once),
            pl.BlockSpec((None, None, d_ff, d), lambda i, be, nu: (idx, be[i], 0, 0),
                         pipeline_mode=once),
        ],
        out_specs=pl.BlockSpec(tile, lambda i, be, nu: (i, 0)),
        scratch_shapes=[pltpu.VMEM((tm, d_ff), BF16), pltpu.VMEM((d_ff, d), BF16)],
    )
    return pl.pallas_call(
        _expert_kernel,
        grid_spec=grid_spec,
        out_shape=jax.ShapeDtypeStruct(xs.shape, F32),
        compiler_params=_params("arbitrary"),
        name="moe_experts",
    )(block_expert, n_used, xs, wg, wu, wd)


def _combine_kernel(p0_ref, p1_ref, q0_ref, q1_ref, x_ref, route_ref, gf_ref, ys_ref, o_ref,
                    g_scr, sem, *, final_norm):
    n_tok = x_ref.shape[0]
    step = pl.program_id(0)
    slot = lax.rem(step, 2)

    def copies(pa_ref, pb_ref, sl):
        return lambda r: (
            _token_copy(ys_ref, g_scr.at[sl, 0], sem.at[sl], pa_ref[0, 0, r], r),
            _token_copy(ys_ref, g_scr.at[sl, 1], sem.at[sl], pb_ref[0, 0, r], r))

    def start(cps):
        _for_tokens(n_tok, lambda r, lane: [cp.start(priority=lane) for cp in cps(r)])

    @pl.when(step == 0)
    def _():
        start(copies(p0_ref, p1_ref, 0))

    @pl.when(step + 1 < pl.num_programs(0))
    def _():
        start(copies(q0_ref, q1_ref, 1 - slot))

    mine = copies(p0_ref, p1_ref, slot)
    _for_tokens(n_tok, lambda r, lane: [cp.wait() for cp in mine(r)])
    w1 = route_ref[:, 2:3]
    w2 = route_ref[:, 3:4]
    out = (x_ref[...] + w1 * _tiles_to_rows(g_scr.at[slot, 0])
           + w2 * _tiles_to_rows(g_scr.at[slot, 1]))
    if final_norm:
        out = _rms(out, gf_ref[...])
    o_ref[...] = out


def _combine(x2, route, ys, pos0, pos1, gf, final_norm):
    n, d = x2.shape
    tp = PERM_TILE
    n_steps = n // tp
    cur = lambda: pl.BlockSpec((1, 1, tp), lambda i: (i, 0, 0), memory_space=pltpu.SMEM)
    nxt = lambda: pl.BlockSpec((1, 1, tp), lambda i: (jnp.minimum(i + 1, n_steps - 1), 0, 0),
                               memory_space=pltpu.SMEM)
    pos0, pos1 = pos0.reshape(n_steps, 1, tp), pos1.reshape(n_steps, 1, tp)
    return pl.pallas_call(
        functools.partial(_combine_kernel, final_norm=final_norm),
        grid=(n_steps,),
        in_specs=[cur(), cur(), nxt(), nxt(),
                  pl.BlockSpec((tp, d), lambda i: (i, 0)),
                  pl.BlockSpec((tp, LANES), lambda i: (i, 0)),
                  pl.BlockSpec((1, d), lambda i: (0, 0)),
                  pl.BlockSpec(memory_space=pl.ANY)],
        out_specs=pl.BlockSpec((tp, d), lambda i: (i, 0)),
        out_shape=jax.ShapeDtypeStruct((n, d), F32),
        scratch_shapes=[pltpu.VMEM((2, 2, tp * TOKEN_ROWS, LANES), F32),
                        pltpu.SemaphoreType.DMA((2,))],
        compiler_params=_params("arbitrary"),
        name="moe_combine",
    )(pos0, pos1, pos0, pos1, x2, route, gf.reshape(1, d), ys)


def _routing_tables(route, n_exp):
    n = route.shape[0]
    tm = MOE_TILE
    i1 = route[:, 0].astype(jnp.int32)
    i2 = route[:, 1].astype(jnp.int32)
    oh1 = jax.nn.one_hot(i1, n_exp, dtype=jnp.int32)
    oh2 = jax.nn.one_hot(i2, n_exp, dtype=jnp.int32)
    tot = oh1 + oh2
    csum = jnp.cumsum(tot, axis=0)
    excl = csum - tot
    rank1 = jnp.sum(excl * oh1, axis=-1)
    rank2 = jnp.sum((excl + oh1) * oh2, axis=-1)
    counts = csum[-1]
    nblk = (counts + tm - 1) // tm
    blk_end = jnp.cumsum(nblk)
    blk_off = blk_end - nblk
    row_off = blk_off * tm
    pos0 = row_off[i1] + rank1
    pos1 = row_off[i2] + rank2
    n_blocks = TOP_K * n // tm + n_exp
    n_used = blk_end[-1]
    jb = jnp.minimum(jnp.arange(n_blocks, dtype=jnp.int32), n_used - 1)
    block_expert = jnp.sum((blk_end[None, :] <= jb[:, None]).astype(jnp.int32), axis=-1)
    block_expert = jnp.minimum(block_expert, n_exp - 1)
    zero_blocks = jnp.where(nblk > 0, blk_end - 1, n_blocks - 1).astype(jnp.int32)
    return (pos0.astype(jnp.int32), pos1.astype(jnp.int32), block_expert,
            n_used.astype(jnp.int32).reshape(1), zero_blocks, n_blocks * tm)


def _moe(x2, h, route, wg, wu, wd, idx, gf, final_norm):
    n_exp = wg.shape[1]
    pos0, pos1, block_expert, n_used, zero_blocks, n_sorted = _routing_tables(route, n_exp)
    xs = _dispatch(h, pos0, pos1, zero_blocks, n_used, n_sorted)
    ys = _experts(xs, wg, wu, wd, idx, block_expert, n_used)
    return _combine(x2, route, ys, pos0, pos1, gf, final_norm)


def kernel(x, norm_mix_g, w_in, conv_dw_w, conv_dw_b, conv_ln_g, conv_ln_b, w_conv_proj,
           ssm_a_re, ssm_a_im, ssm_log_dt, ssm_b_re, ssm_b_im, ssm_c_re, ssm_c_im, ssm_d,
           ssm_w_glu, ssm_b_glu, w_ssm_proj, w_out, norm_ffn_g,
           ffn_w_gate, ffn_w_up, ffn_w_down, router_w, router_b,
           moe_w_gate, moe_w_up, moe_w_down, final_norm_g):
    bsz, seq, d = x.shape
    depth = w_in.shape[0]
    d_conv = conv_dw_w.shape[-1]
    d_ssm = ssm_d.shape[-1]
    n = bsz * seq
    bf = lambda a: a.astype(BF16)
    x2 = x.reshape(n, d).astype(F32)
    s5_mats = _s5_matrices(ssm_a_re, ssm_a_im, ssm_log_dt, ssm_b_re, ssm_b_im,
                           ssm_c_re, ssm_c_im, ssm_d)
    mix_g, ffn_g = _rows3(norm_mix_g), _rows3(norm_ffn_g)
    dw_b, ln_g, ln_b = _rows3(conv_dw_b), _rows3(conv_ln_g), _rows3(conv_ln_b)
    b_glu = _rows3(ssm_b_glu)
    w_in_bf, w_glu_bf, w_sp_bf = bf(w_in), bf(ssm_w_glu), bf(w_ssm_proj)
    w_cp_bf, w_out_bf = bf(w_conv_proj), bf(w_out)
    ffn_bf = (bf(ffn_w_gate), bf(ffn_w_up), bf(ffn_w_down))
    moe_w = (moe_w_gate, moe_w_up, moe_w_down)
    for layer in range(depth):
        last = layer == depth - 1
        ch, u, gates = _inproj(x2, mix_g, w_in_bf, layer, bsz, d_conv, d_ssm)
        hc = _conv_branch(ch.reshape(bsz, seq, d_conv), conv_dw_w, dw_b, ln_g, ln_b,
                          layer).reshape(n, d_conv)
        y = _s5_branch(u, *s5_mats, layer)
        i = layer // 2
        moe = layer % 2 == 1
        router = (router_w[i], router_b[i]) if moe else None
        outs = _merge(x2, hc, y, gates, layer, w_glu_bf, b_glu, w_sp_bf, w_cp_bf, w_out_bf,
                      ffn_g, router)
        if moe:
            x2, h, route = outs
            x2 = _moe(x2, h, route, *moe_w, i, final_norm_g, last)
        else:
            x2, h = outs
            x2 = _ffn_dense(x2, h, *ffn_bf, i, final_norm_g, last)
    return x2.reshape(bsz, seq, d).astype(x.dtype)
```

```python
import functools
import math

import jax
import jax.numpy as jnp
from jax import lax
from jax.experimental import pallas as pl
from jax.experimental.pallas import tpu as pltpu

F32 = jnp.float32
BF16 = jnp.bfloat16

RMS_EPS = 1e-6
LN_EPS = 1e-5
CONV_WIDTH = 31
CONV_HALF = CONV_WIDTH // 2
SSM_GROUP = 16
SSM_STATE = 64
CHUNK = 16
TOP_K = 2
LANES = 128
SUBLANES = 8
VMEM_LIMIT = 56 * 1024 * 1024

ROW_TILE = 512
MOE_TILE = 256
PERM_TILE = 512
CONV_TILE = 128
SCAN_GROUPS = 2
FF_CHUNK = 256


def _params(*sem):
    return pltpu.CompilerParams(dimension_semantics=sem, vmem_limit_bytes=VMEM_LIMIT)


def _rms(xf, g):
    ms = jnp.mean(xf * xf, axis=-1, keepdims=True)
    return xf * lax.rsqrt(ms + RMS_EPS) * g


def _stacked(arr, *lead):
    rest = arr.shape[len(lead):]
    return pl.BlockSpec((None,) * len(lead) + rest, lambda *_: tuple(lead) + (0,) * len(rest))


def _rows3(a):
    return a.reshape(a.shape[0], 1, a.shape[1])


def _inproj_kernel(x_ref, g_ref, w_ref, ch_ref, u_ref, gate_ref):
    dc = ch_ref.shape[-1]
    ds_ = u_ref.shape[0] * LANES
    h = _rms(x_ref[...], g_ref[...]).astype(BF16)
    v = jnp.dot(h, w_ref[:, 0:dc], preferred_element_type=F32)
    gt = jnp.dot(h, w_ref[:, dc:2 * dc], preferred_element_type=F32)
    ch_ref[...] = v * jax.nn.sigmoid(gt)
    u = jnp.dot(h, w_ref[:, 2 * dc:2 * dc + ds_], preferred_element_type=F32)
    for q in range(u_ref.shape[0]):
        u_ref[q, 0] = u[:, q * LANES:(q + 1) * LANES]
    gate_ref[...] = jnp.dot(h, w_ref[:, 2 * dc + ds_:], preferred_element_type=F32).astype(BF16)


def _inproj(x2, g, w_bf, layer, bsz, d_conv, d_ssm):
    n, d = x2.shape
    seq = n // bsz
    d_in = w_bf.shape[-1]
    d_gate = d_in - 2 * d_conv - d_ssm
    tm = ROW_TILE
    per_seq = seq // tm
    nq = d_ssm // LANES
    return pl.pallas_call(
        _inproj_kernel,
        grid=(n // tm,),
        in_specs=[
            pl.BlockSpec((tm, d), lambda i: (i, 0)),
            _stacked(g, layer),
            _stacked(w_bf, layer),
        ],
        out_specs=[
            pl.BlockSpec((tm, d_conv), lambda i: (i, 0)),
            pl.BlockSpec((nq, 1, tm, LANES), lambda i: (0, i // per_seq, i % per_seq, 0)),
            pl.BlockSpec((tm, d_gate), lambda i: (i, 0)),
        ],
        out_shape=[
            jax.ShapeDtypeStruct((n, d_conv), F32),
            jax.ShapeDtypeStruct((nq, bsz, seq, LANES), F32),
            jax.ShapeDtypeStruct((n, d_gate), BF16),
        ],
        compiler_params=_params("parallel"),
        name="inproj",
    )(x2, g, w_bf)


def _conv_kernel(h_ref, w_ref, b_ref, lg_ref, lb_ref, o_ref, pad_scr, tile_scr):
    seq, c = h_ref.shape[1], h_ref.shape[2]
    front = 2 * SUBLANES
    zeros = jnp.zeros((front, c), F32)
    pad_scr[0:front, :] = zeros
    pad_scr[front + seq:front + seq + front, :] = zeros
    pad_scr[front:front + seq, :] = h_ref[0]

    phases = [[] for _ in range(SUBLANES)]
    for k in range(CONV_WIDTH):
        off = k - CONV_HALF + front
        phases[off % SUBLANES].append((k, off - off % SUBLANES))

    def tile_body(ti, carry):
        base = pl.multiple_of(ti * CONV_TILE, CONV_TILE)
        win_ref = pad_scr.at[pl.ds(base, CONV_TILE + 2 * front), :]
        for lb in range(c // LANES):
            ls = slice(lb * LANES, (lb + 1) * LANES)
            acc = b_ref[:, ls]
            for s, taps in enumerate(phases):
                part = None
                for k, row0 in taps:
                    term = w_ref[k:k + 1, ls] * win_ref[pl.ds(row0, CONV_TILE + SUBLANES), ls]
                    part = term if part is None else part + term
                if part is not None:
                    acc = acc + part[s:s + CONV_TILE, :]
            tile_scr[:, ls] = acc
        hh = tile_scr[...]
        mu = jnp.mean(hh, axis=-1, keepdims=True)
        cen = hh - mu
        var = jnp.mean(cen * cen, axis=-1, keepdims=True)
        y = cen * lax.rsqrt(var + LN_EPS) * lg_ref[...] + lb_ref[...]
        o_ref[0, pl.ds(base, CONV_TILE), :] = (y * jax.nn.sigmoid(y)).astype(BF16)
        return carry

    lax.fori_loop(0, seq // CONV_TILE, tile_body, 0)


def _conv_branch(ch, dw_w, dw_b, ln_g, ln_b, layer):
    b, seq, c = ch.shape
    front = 2 * SUBLANES
    return pl.pallas_call(
        _conv_kernel,
        grid=(b,),
        in_specs=[
            pl.BlockSpec((1, seq, c), lambda i: (i, 0, 0)),
            _stacked(dw_w, layer), _stacked(dw_b, layer), _stacked(ln_g, layer),
            _stacked(ln_b, layer),
        ],
        out_specs=pl.BlockSpec((1, seq, c), lambda i: (i, 0, 0)),
        out_shape=jax.ShapeDtypeStruct((b, seq, c), BF16),
        scratch_shapes=[
            pltpu.VMEM((seq + 2 * front, c), F32),
            pltpu.VMEM((CONV_TILE, c), F32),
        ],
        compiler_params=_params("parallel"),
        name="conv_branch",
    )(ch, dw_w, dw_b, ln_g, ln_b)


def _s5_matrices(a_re, a_im, log_dt, b_re, b_im, c_re, c_im, d_skip):
    t = CHUNK
    n_layers, n_dir, g, p = a_re.shape
    c = b_re.shape[-1]
    f = lambda a: a.astype(F32)
    cat = lambda *xs: jnp.concatenate(xs, axis=-1)
    a_re, a_im, c_re, c_im = f(a_re), f(a_im), f(c_re), f(c_im)
    dt = jnp.exp(f(log_dt))[..., None]
    lam_re, lam_im = a_re * dt, a_im * dt
    up, down = jnp.arange(t, dtype=F32), jnp.arange(t - 1, -1, -1, dtype=F32)
    expo = jnp.stack([cat(down, up + 1.0, up, jnp.full((1,), t, F32)),
                      cat(up, down + 1.0, down, jnp.full((1,), t, F32))])
    expo = expo[None, :, None, :, None]
    mag = jnp.exp(expo * lam_re[:, :, :, None, :])
    pw_re = mag * jnp.cos(expo * lam_im[:, :, :, None, :])
    pw_im = mag * jnp.sin(expo * lam_im[:, :, :, None, :])
    ab_re = jnp.exp(lam_re) * jnp.cos(lam_im)
    ab_im = jnp.exp(lam_re) * jnp.sin(lam_im)
    n_re, n_im = ab_re - 1.0, ab_im
    den = a_re * a_re + a_im * a_im
    q_re = ((n_re * a_re + n_im * a_im) / den)[:, :, :, None, :]
    q_im = ((n_im * a_re - n_re * a_im) / den)[:, :, :, None, :]
    bt_re = jnp.swapaxes(f(b_re), -1, -2)[:, None]
    bt_im = jnp.swapaxes(f(b_im), -1, -2)[:, None]
    bb_re = q_re * bt_re - q_im * bt_im
    bb_im = q_re * bt_im + q_im * bt_re

    def powers(lo, hi_, reps):
        sl = lambda w: jnp.tile(w[:, :, :, lo:hi_], (1, 1, 1, 1, reps))[:, :, :, :, None, :]
        return sl(pw_re), sl(pw_im)

    kr, ki = powers(0, t, 4)
    z = (kr * cat(bb_re, bb_im, bb_im, bb_re)[:, :, :, None]
         + ki * cat(-bb_im, bb_re, bb_re, -bb_im)[:, :, :, None])
    m_a = jnp.swapaxes(z.reshape(n_layers, n_dir, g, t * c, 4 * p), 1, 2).astype(BF16)
    kr, ki = powers(t, 2 * t, 2)
    v = (kr * cat(c_re, -c_im)[:, :, :, None] + ki * cat(-c_im, -c_re)[:, :, :, None])
    m_o = jnp.swapaxes(v.reshape(n_layers, n_dir, g, t * c, 2 * p), -1, -2)
    pr = jnp.swapaxes(pw_re[:, :, :, 2 * t:3 * t], -1, -2)[..., None]
    pi = jnp.swapaxes(pw_im[:, :, :, 2 * t:3 * t], -1, -2)[..., None]
    cp_re = jnp.swapaxes(c_re, -1, -2)[:, :, :, :, None, :]
    cp_im = jnp.swapaxes(c_im, -1, -2)[:, :, :, :, None, :]
    vp_re = (pr * cp_re - pi * cp_im)[:, :, :, :, None]
    vp_im = (pr * cp_im + pi * cp_re)[:, :, :, :, None]
    bp_re = jnp.swapaxes(bb_re, -1, -2)[..., None, None]
    bp_im = jnp.swapaxes(bb_im, -1, -2)[..., None, None]
    kk = jnp.sum(bp_re * vp_re - bp_im * vp_im, axis=3)
    kf, kb = kk[:, 0], kk[:, 1]
    diag = kf[:, :, :, 0] + kb[:, :, :, t - 1] + (f(d_skip).reshape(n_layers, g, 1, c)
                                                  * jnp.eye(c, dtype=F32))
    lags = jnp.concatenate([kb[:, :, :, :t - 1], diag[:, :, :, None], kf[:, :, :, 1:]], axis=3)
    lags = lags.reshape(n_layers, g, c, (2 * t - 1) * c)
    m_o = jnp.concatenate([m_o[:, 0], m_o[:, 1]], axis=2).astype(BF16)
    at_re, at_im = pw_re[:, :, :, 3 * t], pw_im[:, :, :, 3 * t]
    rows = [cat(at_re[:, 0], at_re[:, 0]), cat(-at_im[:, 0], at_im[:, 0]),
            cat(at_re[:, 1], at_re[:, 1]), cat(-at_im[:, 1], at_im[:, 1])]
    coef = jnp.stack(rows + [jnp.zeros_like(rows[0])] * (SUBLANES - len(rows)), axis=2)
    return m_a, lags, m_o, coef


def _block_transpose(v):
    nblk = len(v)
    blk = lax.broadcasted_iota(jnp.int32, v[0].shape, 1) // SSM_GROUP
    d = nblk // 2
    while d >= 1:
        keep_lo = (blk & d) == 0
        out = list(v)
        for r in range(nblk):
            if r & d:
                continue
            lo, hi = v[r], v[r + d]
            out[r] = jnp.where(keep_lo, lo, pltpu.roll(hi, d * SSM_GROUP, 1))
            out[r + d] = jnp.where(keep_lo, pltpu.roll(lo, LANES - d * SSM_GROUP, 1), hi)
        v = out
        d //= 2
    return v


def _s5_kernel(u_ref, ma_ref, lag_ref, mo_ref, coef_ref, y_ref, x_scr, z_scr, cin_scr, mi_scr):
    nb, seq = u_ref.shape[1], u_ref.shape[2]
    gpl, halves, rows = x_scr.shape[0:3]
    kx = halves * LANES
    sw = coef_ref.shape[-1]
    n_chunks = seq // CHUNK
    cpt = SUBLANES // nb
    n_tiles = rows // SUBLANES
    span = SUBLANES * CHUNK
    pos_per_half = LANES // SSM_GROUP

    def relayout(co, to_chunks):
        tok0 = pl.multiple_of(co * span, span)
        row0 = pl.multiple_of(co * SUBLANES * nb, SUBLANES * nb)
        for b in range(nb):
            tok_ref = (u_ref if to_chunks else y_ref).at[0, b, pl.ds(tok0, span), :]
            for h in range(halves):
                chunk_refs = [x_scr.at[gp, h, pl.ds(row0, SUBLANES * nb), :] for gp in range(gpl)]
                tok_rows = [pl.ds(h * pos_per_half + ip, SUBLANES, stride=CHUNK)
                            for ip in range(pos_per_half)]
                chunk_rows = pl.ds(b, SUBLANES, stride=nb)
                if to_chunks:
                    dst = _block_transpose([tok_ref[r, :] for r in tok_rows])
                    for gp in range(gpl):
                        chunk_refs[gp][chunk_rows, :] = dst[gp]
                else:
                    dst = _block_transpose([chunk_refs[gp][chunk_rows, :] for gp in range(gpl)])
                    for ip in range(pos_per_half):
                        tok_ref[tok_rows[ip], :] = dst[ip]

    def relayout_in(co, carry):
        relayout(co, True)
        return carry

    def relayout_out(co, carry):
        relayout(co, False)
        return carry

    lax.fori_loop(0, n_chunks // SUBLANES, relayout_in, 0)

    rc = min(rows, 512)
    row_id = lax.broadcasted_iota(jnp.int32, (SUBLANES, sw), 0)
    x_rows = lambda gp, r0: jnp.concatenate(
        [x_scr[gp, h, r0:r0 + rc, :] for h in range(halves)], axis=-1).astype(BF16)
    n_par = z_scr.shape[0]
    for g0 in range(0, gpl, n_par):
        for q in range(n_par):
            for r0 in range(0, rows, rc):
                xr = x_rows(g0 + q, r0)
                for dr in range(ma_ref.shape[1]):
                    z_scr[q, r0:r0 + rc, 2 * dr * sw:2 * (dr + 1) * sw] = jnp.dot(
                        xr, ma_ref[g0 + q, dr], preferred_element_type=F32)
        bc = lambda q, r: jnp.broadcast_to(coef_ref[g0 + q, r:r + 1, :], (SUBLANES, sw))
        coefs = [[bc(q, r) for r in range(4)] for q in range(n_par)]

        def tile_step(ti, carry):
            rf = pl.multiple_of(ti * SUBLANES, SUBLANES)
            rb = pl.multiple_of((n_tiles - 1 - ti) * SUBLANES, SUBLANES)
            out = []
            for q in range(n_par):
                sf, sfp, sb, sbp = carry[4 * q:4 * q + 4]
                af1, af2, ab1, ab2 = coefs[q]
                zf = z_scr[q, pl.ds(rf, SUBLANES), 0:sw]
                zfp = z_scr[q, pl.ds(rf, SUBLANES), sw:2 * sw]
                zb = z_scr[q, pl.ds(rb, SUBLANES), 2 * sw:3 * sw]
                zbp = z_scr[q, pl.ds(rb, SUBLANES), 3 * sw:4 * sw]
                cin_f, cin_b = sf, sb
                for k in range(cpt):
                    nf, nfp = af1 * sf + af2 * sfp + zf, af1 * sfp - af2 * sf + zfp
                    sf, sfp = pltpu.roll(nf, nb, 0), pltpu.roll(nfp, nb, 0)
                    kb = cpt - 1 - k
                    nbk, nbkp = ab1 * sb + ab2 * sbp + zb, ab1 * sbp - ab2 * sb + zbp
                    sb = pltpu.roll(nbk, SUBLANES - nb, 0)
                    sbp = pltpu.roll(nbkp, SUBLANES - nb, 0)
                    if k < cpt - 1:
                        cin_f = jnp.where(row_id >= (k + 1) * nb, sf, cin_f)
                        cin_b = jnp.where(row_id < kb * nb, sb, cin_b)
                cin_scr[q, pl.ds(rf, SUBLANES), 0:sw] = cin_f
                cin_scr[q, pl.ds(rb, SUBLANES), sw:2 * sw] = cin_b
                out += [sf, sfp, sb, sbp]
            return tuple(out)

        zero = jnp.zeros((SUBLANES, sw), F32)
        lax.fori_loop(0, n_tiles, tile_step, (zero,) * (4 * n_par), unroll=2)

        for q in range(n_par):
            gp = g0 + q
            lag = lag_ref[gp]
            for j in range(CHUNK):
                first = (CHUNK - 1 - j) * SSM_GROUP
                mi_scr[q, j * SSM_GROUP:(j + 1) * SSM_GROUP, :] = (
                    lag[:, first:first + kx].astype(BF16))
            for r0 in range(0, rows, rc):
                y = (jnp.dot(x_rows(gp, r0), mi_scr[q], preferred_element_type=F32)
                     + jnp.dot(cin_scr[q, r0:r0 + rc, :].astype(BF16), mo_ref[gp],
                               preferred_element_type=F32))
                for h in range(halves):
                    x_scr[gp, h, r0:r0 + rc, :] = y[:, h * LANES:(h + 1) * LANES]

    lax.fori_loop(0, n_chunks // SUBLANES, relayout_out, 0)


def _s5_branch(u, m_a, lags, m_o, coef, layer):
    nq, b, seq, _ = u.shape
    g = m_a.shape[1]
    gpl = g // nq
    n_chunks = seq // CHUNK
    assert SUBLANES % b == 0 and n_chunks % SUBLANES == 0 and gpl * SSM_GROUP == LANES
    rows = n_chunks * b
    n_dir, kx, ks = m_a.shape[2], m_a.shape[3], m_o.shape[2]
    nz = n_dir * m_a.shape[4]
    return pl.pallas_call(
        _s5_kernel,
        grid=(nq,),
        in_specs=[
            pl.BlockSpec((1, b, seq, LANES), lambda i: (i, 0, 0, 0), pipeline_mode=pl.Buffered(1)),
            pl.BlockSpec((None, gpl, n_dir, kx, nz // n_dir), lambda i: (layer, i, 0, 0, 0)),
            pl.BlockSpec((None, gpl) + lags.shape[2:], lambda i: (layer, i, 0, 0)),
            pl.BlockSpec((None, gpl, ks, kx), lambda i: (layer, i, 0, 0)),
            pl.BlockSpec((None, gpl, SUBLANES, coef.shape[-1]), lambda i: (layer, i, 0, 0)),
        ],
        out_specs=pl.BlockSpec((1, b, seq, LANES), lambda i: (i, 0, 0, 0)),
        out_shape=jax.ShapeDtypeStruct((nq, b, seq, LANES), F32),
        scratch_shapes=[
            pltpu.VMEM((gpl, kx // LANES, rows, LANES), F32),
            pltpu.VMEM((SCAN_GROUPS, rows, nz), F32),
            pltpu.VMEM((SCAN_GROUPS, rows, ks), F32),
            pltpu.VMEM((SCAN_GROUPS, kx, kx), BF16),
        ],
        compiler_params=_params("parallel"),
        name="s5_scan",
    )(u, m_a, lags, m_o, coef)


def _merge_kernel(x_ref, hc_ref, y_ref, gate_ref, wglu_ref, bglu_ref, ws_ref, wc_ref,
                  wout_ref, gn_ref, *rest, moe):
    if moe:
        rw_ref, rb_ref, xo_ref, h_ref, route_ref = rest
    else:
        xo_ref, h_ref = rest
    d = x_ref.shape[-1]
    yg = jax.nn.gelu(jnp.concatenate([y_ref[q, 0] for q in range(y_ref.shape[0])], axis=-1))
    glu = jnp.dot(yg.astype(BF16), wglu_ref[...], preferred_element_type=F32) + bglu_ref[...]
    y2 = yg * jax.nn.sigmoid(glu)
    br_s = jnp.dot(y2.astype(BF16), ws_ref[...], preferred_element_type=F32)
    br_c = jnp.dot(hc_ref[...], wc_ref[...], preferred_element_type=F32)
    gc = gate_ref[:, 0:d].astype(F32)
    gs = gate_ref[:, d:2 * d].astype(F32)
    merged = jax.nn.sigmoid(gc) * br_c + jax.nn.sigmoid(gs) * br_s
    xn = x_ref[...] + jnp.dot(merged.astype(BF16), wout_ref[...], preferred_element_type=F32)
    xo_ref[...] = xn
    h = _rms(xn, gn_ref[...])
    if not moe:
        h_ref[...] = h.astype(h_ref.dtype)
    if moe:
        _rows_to_tiles(h_ref, h)
        h_hi = h.astype(BF16)
        h_lo = (h - h_hi.astype(F32)).astype(BF16)
        logits = (jnp.dot(h_hi, rw_ref[0], preferred_element_type=F32)
                  + jnp.dot(h_hi, rw_ref[1], preferred_element_type=F32)
                  + jnp.dot(h_lo, rw_ref[0], preferred_element_type=F32)) + rb_ref[1:2, :]
        lane = lax.broadcasted_iota(jnp.int32, logits.shape, 1).astype(F32)
        valid = rb_ref[0:1, :] > 0.0
        neg = jnp.float32(-jnp.inf)
        logits = jnp.where(valid, logits, neg)
        m1 = jnp.max(logits, axis=-1, keepdims=True)
        i1 = jnp.min(jnp.where(logits == m1, lane, float(LANES)), axis=-1, keepdims=True)
        rest_l = jnp.where(lane == i1, neg, logits)
        m2 = jnp.max(rest_l, axis=-1, keepdims=True)
        i2 = jnp.min(jnp.where(rest_l == m2, lane, float(LANES)), axis=-1, keepdims=True)
        e2 = jnp.exp(m2 - m1)
        w1 = 1.0 / (1.0 + e2)
        w2 = e2 / (1.0 + e2)
        route_ref[...] = jnp.where(lane == 0.0, i1, jnp.where(lane == 1.0, i2,
                                   jnp.where(lane == 2.0, w1, jnp.where(lane == 3.0, w2, 0.0))))


def _merge(x2, hc, y, gates, layer, wglu, bglu, ws, wc, wout, gn, router=None):
    n, d = x2.shape
    ds_ = hc.shape[1]
    nq, _, seq, _ = y.shape
    tm = ROW_TILE
    per_seq = seq // tm
    moe = router is not None
    row = lambda w: pl.BlockSpec((tm, w), lambda i: (i, 0))
    full = lambda a: pl.BlockSpec(a.shape, lambda i: (0,) * a.ndim)
    y_spec = pl.BlockSpec((nq, 1, tm, LANES), lambda i: (0, i // per_seq, i % per_seq, 0))
    ins = [x2, hc, y, gates, wglu, bglu, ws, wc, wout, gn]
    in_specs = ([row(d), row(ds_), y_spec, row(gates.shape[1])]
                + [_stacked(a, layer) for a in ins[4:]])
    out_specs = [row(d), row(d)]
    out_shape = [jax.ShapeDtypeStruct((n, d), F32), jax.ShapeDtypeStruct((n, d), BF16)]
    if moe:
        tr = d // LANES
        out_specs[1] = pl.BlockSpec((tm * tr, LANES), lambda i: (i, 0))
        out_shape[1] = jax.ShapeDtypeStruct((n * tr, LANES), F32)
        rw, rb = router
        n_exp = rw.shape[1]
        rw_f = jnp.zeros((d, LANES), F32).at[:, :n_exp].set(rw.astype(F32))
        rw_hi = rw_f.astype(BF16)
        rw_pad = jnp.stack([rw_hi, (rw_f - rw_hi.astype(F32)).astype(BF16)])
        rb_pad = jnp.zeros((SUBLANES, LANES), F32).at[0, :n_exp].set(1.0).at[1, :n_exp].set(rb.astype(F32))
        ins += [rw_pad, rb_pad]
        in_specs += [full(rw_pad), full(rb_pad)]
        out_specs.append(row(LANES))
        out_shape.append(jax.ShapeDtypeStruct((n, LANES), F32))
    return pl.pallas_call(
        functools.partial(_merge_kernel, moe=moe),
        grid=(n // tm,),
        in_specs=in_specs,
        out_specs=out_specs,
        out_shape=out_shape,
        compiler_params=_params("parallel"),
        name="merge_moe" if moe else "merge",
    )(*ins)


def _swiglu_rows(h_bf, wg_ref, wu_ref, wd_ref, a_scr, wd_scr=None):
    d_ff = a_scr.shape[1]
    for c0 in range(0, d_ff, FF_CHUNK):
        cs = slice(c0, c0 + FF_CHUNK)
        gt = jnp.dot(h_bf, wg_ref[:, cs].astype(BF16), preferred_element_type=F32)
        up = jnp.dot(h_bf, wu_ref[:, cs].astype(BF16), preferred_element_type=F32)
        a_scr[:, cs] = (gt * jax.nn.sigmoid(gt) * up).astype(BF16)
        if wd_scr is not None:
            wd_scr[cs, :] = wd_ref[cs, :].astype(BF16)
    wd = wd_ref[...] if wd_scr is None else wd_scr[...]
    return jnp.dot(a_scr[...], wd, preferred_element_type=F32)


def _ffn_kernel(x_ref, h_ref, wg_ref, wu_ref, wd_ref, gf_ref, o_ref, a_scr, *, final_norm):
    out = x_ref[...] + _swiglu_rows(h_ref[...], wg_ref, wu_ref, wd_ref, a_scr)
    if final_norm:
        out = _rms(out, gf_ref[...])
    o_ref[...] = out


def _ffn_dense(x2, h, wg, wu, wd, idx, gf, final_norm):
    n, d = x2.shape
    d_ff = wg.shape[-1]
    tm = ROW_TILE
    once = pl.Buffered(1)
    return pl.pallas_call(
        functools.partial(_ffn_kernel, final_norm=final_norm),
        grid=(n // tm,),
        in_specs=[
            pl.BlockSpec((tm, d), lambda i: (i, 0)),
            pl.BlockSpec((tm, d), lambda i: (i, 0)),
            pl.BlockSpec((None, d, d_ff), lambda i: (idx, 0, 0), pipeline_mode=once),
            pl.BlockSpec((None, d, d_ff), lambda i: (idx, 0, 0), pipeline_mode=once),
            pl.BlockSpec((None, d_ff, d), lambda i: (idx, 0, 0), pipeline_mode=once),
            pl.BlockSpec((1, d), lambda i: (0, 0)),
        ],
        out_specs=pl.BlockSpec((tm, d), lambda i: (i, 0)),
        out_shape=jax.ShapeDtypeStruct((n, d), F32),
        scratch_shapes=[pltpu.VMEM((tm, d_ff), BF16)],
        compiler_params=_params("parallel"),
        name="ffn_dense",
    )(x2, h, wg, wu, wd, gf.reshape(1, d))


TOKEN_ROWS = SUBLANES
PERM_UNROLL = 4


def _rows_to_tiles(ref, val):
    m = val.shape[0]
    for j in range(TOKEN_ROWS):
        ref[pl.ds(j, m, stride=TOKEN_ROWS), :] = val[:, j * LANES:(j + 1) * LANES]


def _tiles_to_rows(ref):
    m = ref.shape[0] // TOKEN_ROWS
    return jnp.concatenate([ref[pl.ds(j, m, stride=TOKEN_ROWS), :] for j in range(TOKEN_ROWS)],
                           axis=-1)


def _token_copy(src, dst, sem, s, t):
    rows = lambda i: pl.ds(pl.multiple_of(i * TOKEN_ROWS, TOKEN_ROWS), TOKEN_ROWS)
    return pltpu.make_async_copy(src.at[rows(s), :], dst.at[rows(t), :], sem)


def _for_tokens(n_tok, fn):
    def body(i, carry):
        for j in range(PERM_UNROLL):
            fn(i * PERM_UNROLL + j, j % 2)
        return carry
    lax.fori_loop(0, n_tok // PERM_UNROLL, body, 0)


def _dispatch_kernel(zb_ref, nu_ref, p0_ref, p1_ref, h_ref, xs_ref, zero_scr, sem):
    n_tok = h_ref.shape[0] // TOKEN_ROWS
    n_exp = zb_ref.shape[0]

    @pl.when(pl.program_id(0) == 0)
    def _():
        zero_scr[...] = jnp.zeros(zero_scr.shape, F32)
        rows = zero_scr.shape[0]

        def fill(blk):
            start = pl.multiple_of(blk * rows, rows)
            return pltpu.make_async_copy(zero_scr, xs_ref.at[pl.ds(start, rows), :], sem)

        for e in range(n_exp):
            fill(zb_ref[e]).start()
        for e in range(n_exp):
            fill(zb_ref[e]).wait()

        def fill_unused(blk, carry):
            fill(blk).start()
            fill(blk).wait()
            return carry

        lax.fori_loop(nu_ref[0], xs_ref.shape[0] // rows, fill_unused, 0)

    def copies(r):
        return (_token_copy(h_ref, xs_ref, sem, r, p0_ref[0, 0, r]),
                _token_copy(h_ref, xs_ref, sem, r, p1_ref[0, 0, r]))

    _for_tokens(n_tok, lambda r, lane: [cp.start(priority=lane) for cp in copies(r)])
    _for_tokens(n_tok, lambda r, lane: [cp.wait() for cp in copies(r)])


def _dispatch(h, pos0, pos1, zero_blocks, n_used, n_sorted):
    n = h.shape[0] // TOKEN_ROWS
    tp = PERM_TILE
    smem_rows = lambda: pl.BlockSpec((1, 1, tp), lambda i, zb, nu: (i, 0, 0),
                                     memory_space=pltpu.SMEM)
    grid_spec = pltpu.PrefetchScalarGridSpec(
        num_scalar_prefetch=2,
        grid=(n // tp,),
        in_specs=[smem_rows(), smem_rows(),
                  pl.BlockSpec((tp * TOKEN_ROWS, LANES), lambda i, zb, nu: (i, 0))],
        out_specs=pl.BlockSpec(memory_space=pl.ANY),
        scratch_shapes=[pltpu.VMEM((MOE_TILE * TOKEN_ROWS, LANES), F32),
                        pltpu.SemaphoreType.DMA(())],
    )
    return pl.pallas_call(
        _dispatch_kernel,
        grid_spec=grid_spec,
        out_shape=jax.ShapeDtypeStruct((n_sorted * TOKEN_ROWS, LANES), F32),
        compiler_params=_params("arbitrary"),
        name="moe_dispatch",
    )(zero_blocks, n_used, pos0.reshape(n // tp, 1, tp), pos1.reshape(n // tp, 1, tp), h)


def _expert_kernel(be_ref, nu_ref, xs_ref, wg_ref, wu_ref, wd_ref, ys_ref, a_scr, wd_scr):
    @pl.when(pl.program_id(0) < nu_ref[0])
    def _():
        _rows_to_tiles(ys_ref, _swiglu_rows(_tiles_to_rows(xs_ref).astype(BF16), wg_ref, wu_ref,
                                            wd_ref, a_scr, wd_scr))

    @pl.when(pl.program_id(0) >= nu_ref[0])
    def _():
        ys_ref[...] = jnp.zeros(ys_ref.shape, F32)


def _experts(xs, wg, wu, wd, idx, block_expert, n_used):
    d, d_ff = wg.shape[-2:]
    tm = MOE_TILE
    tile = (tm * TOKEN_ROWS, LANES)
    once = pl.Buffered(1)
    grid_spec = pltpu.PrefetchScalarGridSpec(
        num_scalar_prefetch=2,
        grid=(xs.shape[0] // tile[0],),
        in_specs=[
            pl.BlockSpec(tile, lambda i, be, nu: (jnp.minimum(i, nu[0] - 1), 0)),
            pl.BlockSpec((None, None, d, d_ff), lambda i, be, nu: (idx, be[i], 0, 0),
                         pipeline_mode=once),
            pl.BlockSpec((None, None, d, d_ff), lambda i, be, nu: (idx, be[i], 0, 0),
                         pipeline_mode=once),
            pl.BlockSpec((None, None, d_ff, d), lambda i, be, nu: (idx, be[i], 0, 0),
                         pipeline_mode=once),
        ],
        out_specs=pl.BlockSpec(tile, lambda i, be, nu: (i, 0)),
        scratch_shapes=[pltpu.VMEM((tm, d_ff), BF16), pltpu.VMEM((d_ff, d), BF16)],
    )
    return pl.pallas_call(
        _expert_kernel,
        grid_spec=grid_spec,
        out_shape=jax.ShapeDtypeStruct(xs.shape, F32),
        compiler_params=_params("arbitrary"),
        name="moe_experts",
    )(block_expert, n_used, xs, wg, wu, wd)


def _combine_kernel(p0_ref, p1_ref, q0_ref, q1_ref, x_ref, route_ref, gf_ref, ys_ref, o_ref,
                    g_scr, sem, *, final_norm):
    n_tok = x_ref.shape[0]
    step = pl.program_id(0)
    slot = lax.rem(step, 2)

    def copies(pa_ref, pb_ref, sl):
        return lambda r: (
            _token_copy(ys_ref, g_scr.at[sl, 0], sem.at[sl], pa_ref[0, 0, r], r),
            _token_copy(ys_ref, g_scr.at[sl, 1], sem.at[sl], pb_ref[0, 0, r], r))

    def start(cps):
        _for_tokens(n_tok, lambda r, lane: [cp.start(priority=lane) for cp in cps(r)])

    @pl.when(step == 0)
    def _():
        start(copies(p0_ref, p1_ref, 0))

    @pl.when(step + 1 < pl.num_programs(0))
    def _():
        start(copies(q0_ref, q1_ref, 1 - slot))

    mine = copies(p0_ref, p1_ref, slot)
    _for_tokens(n_tok, lambda r, lane: [cp.wait() for cp in mine(r)])
    w1 = route_ref[:, 2:3]
    w2 = route_ref[:, 3:4]
    out = (x_ref[...] + w1 * _tiles_to_rows(g_scr.at[slot, 0])
           + w2 * _tiles_to_rows(g_scr.at[slot, 1]))
    if final_norm:
        out = _rms(out, gf_ref[...])
    o_ref[...] = out


def _combine(x2, route, ys, pos0, pos1, gf, final_norm):
    n, d = x2.shape
    tp = PERM_TILE
    n_steps = n // tp
    cur = lambda: pl.BlockSpec((1, 1, tp), lambda i: (i, 0, 0), memory_space=pltpu.SMEM)
    nxt = lambda: pl.BlockSpec((1, 1, tp), lambda i: (jnp.minimum(i + 1, n_steps - 1), 0, 0),
                               memory_space=pltpu.SMEM)
    pos0, pos1 = pos0.reshape(n_steps, 1, tp), pos1.reshape(n_steps, 1, tp)
    return pl.pallas_call(
        functools.partial(_combine_kernel, final_norm=final_norm),
        grid=(n_steps,),
        in_specs=[cur(), cur(), nxt(), nxt(),
                  pl.BlockSpec((tp, d), lambda i: (i, 0)),
                  pl.BlockSpec((tp, LANES), lambda i: (i, 0)),
                  pl.BlockSpec((1, d), lambda i: (0, 0)),
                  pl.BlockSpec(memory_space=pl.ANY)],
        out_specs=pl.BlockSpec((tp, d), lambda i: (i, 0)),
        out_shape=jax.ShapeDtypeStruct((n, d), F32),
        scratch_shapes=[pltpu.VMEM((2, 2, tp * TOKEN_ROWS, LANES), F32),
                        pltpu.SemaphoreType.DMA((2,))],
        compiler_params=_params("arbitrary"),
        name="moe_combine",
    )(pos0, pos1, pos0, pos1, x2, route, gf.reshape(1, d), ys)


def _routing_tables(route, n_exp):
    n = route.shape[0]
    tm = MOE_TILE
    i1 = route[:, 0].astype(jnp.int32)
    i2 = route[:, 1].astype(jnp.int32)
    oh1 = jax.nn.one_hot(i1, n_exp, dtype=jnp.int32)
    oh2 = jax.nn.one_hot(i2, n_exp, dtype=jnp.int32)
    tot = oh1 + oh2
    csum = jnp.cumsum(tot, axis=0)
    excl = csum - tot
    rank1 = jnp.sum(excl * oh1, axis=-1)
    rank2 = jnp.sum((excl + oh1) * oh2, axis=-1)
    counts = csum[-1]
    nblk = (counts + tm - 1) // tm
    blk_end = jnp.cumsum(nblk)
    blk_off = blk_end - nblk
    row_off = blk_off * tm
    pos0 = row_off[i1] + rank1
    pos1 = row_off[i2] + rank2
    n_blocks = TOP_K * n // tm + n_exp
    n_used = blk_end[-1]
    jb = jnp.minimum(jnp.arange(n_blocks, dtype=jnp.int32), n_used - 1)
    block_expert = jnp.sum((blk_end[None, :] <= jb[:, None]).astype(jnp.int32), axis=-1)
    block_expert = jnp.minimum(block_expert, n_exp - 1)
    zero_blocks = jnp.where(nblk > 0, blk_end - 1, n_blocks - 1).astype(jnp.int32)
    return (pos0.astype(jnp.int32), pos1.astype(jnp.int32), block_expert,
            n_used.astype(jnp.int32).reshape(1), zero_blocks, n_blocks * tm)


def _moe(x2, h, route, wg, wu, wd, idx, gf, final_norm):
    n_exp = wg.shape[1]
    pos0, pos1, block_expert, n_used, zero_blocks, n_sorted = _routing_tables(route, n_exp)
    xs = _dispatch(h, pos0, pos1, zero_blocks, n_used, n_sorted)
    ys = _experts(xs, wg, wu, wd, idx, block_expert, n_used)
    return _combine(x2, route, ys, pos0, pos1, gf, final_norm)


def kernel(x, norm_mix_g, w_in, conv_dw_w, conv_dw_b, conv_ln_g, conv_ln_b, w_conv_proj,
           ssm_a_re, ssm_a_im, ssm_log_dt, ssm_b_re, ssm_b_im, ssm_c_re, ssm_c_im, ssm_d,
           ssm_w_glu, ssm_b_glu, w_ssm_proj, w_out, norm_ffn_g,
           ffn_w_gate, ffn_w_up, ffn_w_down, router_w, router_b,
           moe_w_gate, moe_w_up, moe_w_down, final_norm_g):
    bsz, seq, d = x.shape
    depth = w_in.shape[0]
    d_conv = conv_dw_w.shape[-1]
    d_ssm = ssm_d.shape[-1]
    n = bsz * seq
    bf = lambda a: a.astype(BF16)
    x2 = x.reshape(n, d).astype(F32)
    s5_mats = _s5_matrices(ssm_a_re, ssm_a_im, ssm_log_dt, ssm_b_re, ssm_b_im,
                           ssm_c_re, ssm_c_im, ssm_d)
    mix_g, ffn_g = _rows3(norm_mix_g), _rows3(norm_ffn_g)
    dw_b, ln_g, ln_b = _rows3(conv_dw_b), _rows3(conv_ln_g), _rows3(conv_ln_b)
    b_glu = _rows3(ssm_b_glu)
    w_in_bf, w_glu_bf, w_sp_bf = bf(w_in), bf(ssm_w_glu), bf(w_ssm_proj)
    w_cp_bf, w_out_bf = bf(w_conv_proj), bf(w_out)
    ffn_bf = (bf(ffn_w_gate), bf(ffn_w_up), bf(ffn_w_down))
    moe_w = (moe_w_gate, moe_w_up, moe_w_down)
    for layer in range(depth):
        last = layer == depth - 1
        ch, u, gates = _inproj(x2, mix_g, w_in_bf, layer, bsz, d_conv, d_ssm)
        hc = _conv_branch(ch.reshape(bsz, seq, d_conv), conv_dw_w, dw_b, ln_g, ln_b,
                          layer).reshape(n, d_conv)
        y = _s5_branch(u, *s5_mats, layer)
        i = layer // 2
        moe = layer % 2 == 1
        router = (router_w[i], router_b[i]) if moe else None
        outs = _merge(x2, hc, y, gates, layer, w_glu_bf, b_glu, w_sp_bf, w_cp_bf, w_out_bf,
                      ffn_g, router)
        if moe:
            x2, h, route = outs
            x2 = _moe(x2, h, route, *moe_w, i, final_norm_g, last)
        else:
            x2, h = outs
            x2 = _ffn_dense(x2, h, *ffn_bf, i, final_norm_g, last)
    return x2.reshape(bsz, seq, d).astype(x.dtype)
```

```python
import functools
import math

import jax
import jax.numpy as jnp
from jax import lax
from jax.experimental import pallas as pl
from jax.experimental.pallas import tpu as pltpu

F32 = jnp.float32
BF16 = jnp.bfloat16

RMS_EPS = 1e-6
LN_EPS = 1e-5
CONV_WIDTH = 31
CONV_HALF = CONV_WIDTH // 2
SSM_GROUP = 16
SSM_STATE = 64
CHUNK = 16
TOP_K = 2
LANES = 128
SUBLANES = 8
VMEM_LIMIT = 56 * 1024 * 1024

ROW_TILE = 512
MOE_TILE = 256
PERM_TILE = 512
CONV_TILE = 128
SCAN_GROUPS = 2
FF_CHUNK = 256


def _params(*sem):
    return pltpu.CompilerParams(dimension_semantics=sem, vmem_limit_bytes=VMEM_LIMIT)


def _rms(xf, g):
    ms = jnp.mean(xf * xf, axis=-1, keepdims=True)
    return xf * lax.rsqrt(ms + RMS_EPS) * g


def _stacked(arr, *lead):
    rest = arr.shape[len(lead):]
    return pl.BlockSpec((None,) * len(lead) + rest, lambda *_: tuple(lead) + (0,) * len(rest))


def _rows3(a):
    return a.reshape(a.shape[0], 1, a.shape[1])


def _inproj_kernel(x_ref, g_ref, w_ref, ch_ref, u_ref, gate_ref):
    dc = ch_ref.shape[-1]
    ds_ = u_ref.shape[0] * LANES
    h = _rms(x_ref[...], g_ref[...]).astype(BF16)
    v = jnp.dot(h, w_ref[:, 0:dc], preferred_element_type=F32)
    gt = jnp.dot(h, w_ref[:, dc:2 * dc], preferred_element_type=F32)
    ch_ref[...] = v * jax.nn.sigmoid(gt)
    u = jnp.dot(h, w_ref[:, 2 * dc:2 * dc + ds_], preferred_element_type=F32)
    for q in range(u_ref.shape[0]):
        u_ref[q, 0] = u[:, q * LANES:(q + 1) * LANES]
    gate_ref[...] = jnp.dot(h, w_ref[:, 2 * dc + ds_:], preferred_element_type=F32).astype(BF16)


def _inproj(x2, g, w_bf, layer, bsz, d_conv, d_ssm):
    n, d = x2.shape
    seq = n // bsz
    d_in = w_bf.shape[-1]
    d_gate = d_in - 2 * d_conv - d_ssm
    tm = ROW_TILE
    per_seq = seq // tm
    nq = d_ssm // LANES
    return pl.pallas_call(
        _inproj_kernel,
        grid=(n // tm,),
        in_specs=[
            pl.BlockSpec((tm, d), lambda i: (i, 0)),
            _stacked(g, layer),
            _stacked(w_bf, layer),
        ],
        out_specs=[
            pl.BlockSpec((tm, d_conv), lambda i: (i, 0)),
            pl.BlockSpec((nq, 1, tm, LANES), lambda i: (0, i // per_seq, i % per_seq, 0)),
            pl.BlockSpec((tm, d_gate), lambda i: (i, 0)),
        ],
        out_shape=[
            jax.ShapeDtypeStruct((n, d_conv), F32),
            jax.ShapeDtypeStruct((nq, bsz, seq, LANES), F32),
            jax.ShapeDtypeStruct((n, d_gate), BF16),
        ],
        compiler_params=_params("parallel"),
        name="inproj",
    )(x2, g, w_bf)


def _conv_kernel(h_ref, w_ref, b_ref, lg_ref, lb_ref, o_ref, pad_scr, tile_scr):
    seq, c = h_ref.shape[1], h_ref.shape[2]
    front = 2 * SUBLANES
    zeros = jnp.zeros((front, c), F32)
    pad_scr[0:front, :] = zeros
    pad_scr[front + seq:front + seq + front, :] = zeros
    pad_scr[front:front + seq, :] = h_ref[0]

    phases = [[] for _ in range(SUBLANES)]
    for k in range(CONV_WIDTH):
        off = k - CONV_HALF + front
        phases[off % SUBLANES].append((k, off - off % SUBLANES))

    def tile_body(ti, carry):
        base = pl.multiple_of(ti * CONV_TILE, CONV_TILE)
        win_ref = pad_scr.at[pl.ds(base, CONV_TILE + 2 * front), :]
        for lb in range(c // LANES):
            ls = slice(lb * LANES, (lb + 1) * LANES)
            acc = b_ref[:, ls]
            for s, taps in enumerate(phases):
                part = None
                for k, row0 in taps:
                    term = w_ref[k:k + 1, ls] * win_ref[pl.ds(row0, CONV_TILE + SUBLANES), ls]
                    part = term if part is None else part + term
                if part is not None:
                    acc = acc + part[s:s + CONV_TILE, :]
            tile_scr[:, ls] = acc
        hh = tile_scr[...]
        mu = jnp.mean(hh, axis=-1, keepdims=True)
        cen = hh - mu
        var = jnp.mean(cen * cen, axis=-1, keepdims=True)
        y = cen * lax.rsqrt(var + LN_EPS) * lg_ref[...] + lb_ref[...]
        o_ref[0, pl.ds(base, CONV_TILE), :] = (y * jax.nn.sigmoid(y)).astype(BF16)
        return carry

    lax.fori_loop(0, seq // CONV_TILE, tile_body, 0)


def _conv_branch(ch, dw_w, dw_b, ln_g, ln_b, layer):
    b, seq, c = ch.shape
    front = 2 * SUBLANES
    return pl.pallas_call(
        _conv_kernel,
        grid=(b,),
        in_specs=[
            pl.BlockSpec((1, seq, c), lambda i: (i, 0, 0)),
            _stacked(dw_w, layer), _stacked(dw_b, layer), _stacked(ln_g, layer),
            _stacked(ln_b, layer),
        ],
        out_specs=pl.BlockSpec((1, seq, c), lambda i: (i, 0, 0)),
        out_shape=jax.ShapeDtypeStruct((b, seq, c), BF16),
        scratch_shapes=[
            pltpu.VMEM((seq + 2 * front, c), F32),
            pltpu.VMEM((CONV_TILE, c), F32),
        ],
        compiler_params=_params("parallel"),
        name="conv_branch",
    )(ch, dw_w, dw_b, ln_g, ln_b)


def _s5_matrices(a_re, a_im, log_dt, b_re, b_im, c_re, c_im, d_skip):
    t = CHUNK
    n_layers, n_dir, g, p = a_re.shape
    c = b_re.shape[-1]
    f = lambda a: a.astype(F32)
    cat = lambda *xs: jnp.concatenate(xs, axis=-1)
    a_re, a_im, c_re, c_im = f(a_re), f(a_im), f(c_re), f(c_im)
    dt = jnp.exp(f(log_dt))[..., None]
    lam_re, lam_im = a_re * dt, a_im * dt
    up, down = jnp.arange(t, dtype=F32), jnp.arange(t - 1, -1, -1, dtype=F32)
    expo = jnp.stack([cat(down, up + 1.0, up, jnp.full((1,), t, F32)),
                      cat(up, down + 1.0, down, jnp.full((1,), t, F32))])
    expo = expo[None, :, None, :, None]
    mag = jnp.exp(expo * lam_re[:, :, :, None, :])
    pw_re = mag * jnp.cos(expo * lam_im[:, :, :, None, :])
    pw_im = mag * jnp.sin(expo * lam_im[:, :, :, None, :])
    ab_re = jnp.exp(lam_re) * jnp.cos(lam_im)
    ab_im = jnp.exp(lam_re) * jnp.sin(lam_im)
    n_re, n_im = ab_re - 1.0, ab_im
    den = a_re * a_re + a_im * a_im
    q_re = ((n_re * a_re + n_im * a_im) / den)[:, :, :, None, :]
    q_im = ((n_im * a_re - n_re * a_im) / den)[:, :, :, None, :]
    bt_re = jnp.swapaxes(f(b_re), -1, -2)[:, None]
    bt_im = jnp.swapaxes(f(b_im), -1, -2)[:, None]
    bb_re = q_re * bt_re - q_im * bt_im
    bb_im = q_re * bt_im + q_im * bt_re

    def powers(lo, hi_, reps):
        sl = lambda w: jnp.tile(w[:, :, :, lo:hi_], (1, 1, 1, 1, reps))[:, :, :, :, None, :]
        return sl(pw_re), sl(pw_im)

    kr, ki = powers(0, t, 4)
    z = (kr * cat(bb_re, bb_im, bb_im, bb_re)[:, :, :, None]
         + ki * cat(-bb_im, bb_re, bb_re, -bb_im)[:, :, :, None])
    m_a = jnp.swapaxes(z.reshape(n_layers, n_dir, g, t * c, 4 * p), 1, 2).astype(BF16)
    kr, ki = powers(t, 2 * t, 2)
    v = (kr * cat(c_re, -c_im)[:, :, :, None] + ki * cat(-c_im, -c_re)[:, :, :, None])
    m_o = jnp.swapaxes(v.reshape(n_layers, n_dir, g, t * c, 2 * p), -1, -2)
    pr = jnp.swapaxes(pw_re[:, :, :, 2 * t:3 * t], -1, -2)[..., None]
    pi = jnp.swapaxes(pw_im[:, :, :, 2 * t:3 * t], -1, -2)[..., None]
    cp_re = jnp.swapaxes(c_re, -1, -2)[:, :, :, :, None, :]
    cp_im = jnp.swapaxes(c_im, -1, -2)[:, :, :, :, None, :]
    vp_re = (pr * cp_re - pi * cp_im)[:, :, :, :, None]
    vp_im = (pr * cp_im + pi * cp_re)[:, :, :, :, None]
    bp_re = jnp.swapaxes(bb_re, -1, -2)[..., None, None]
    bp_im = jnp.swapaxes(bb_im, -1, -2)[..., None, None]
    kk = jnp.sum(bp_re * vp_re - bp_im * vp_im, axis=3)
    kf, kb = kk[:, 0], kk[:, 1]
    diag = kf[:, :, :, 0] + kb[:, :, :, t - 1] + (f(d_skip).reshape(n_layers, g, 1, c)
                                                  * jnp.eye(c, dtype=F32))
    lags = jnp.concatenate([kb[:, :, :, :t - 1], diag[:, :, :, None], kf[:, :, :, 1:]], axis=3)
    lags = lags.reshape(n_layers, g, c, (2 * t - 1) * c)
    m_o = jnp.concatenate([m_o[:, 0], m_o[:, 1]], axis=2).astype(BF16)
    at_re, at_im = pw_re[:, :, :, 3 * t], pw_im[:, :, :, 3 * t]
    rows = [cat(at_re[:, 0], at_re[:, 0]), cat(-at_im[:, 0], at_im[:, 0]),
            cat(at_re[:, 1], at_re[:, 1]), cat(-at_im[:, 1], at_im[:, 1])]
    coef = jnp.stack(rows + [jnp.zeros_like(rows[0])] * (SUBLANES - len(rows)), axis=2)
    return m_a, lags, m_o, coef


def _block_transpose(v):
    nblk = len(v)
    blk = lax.broadcasted_iota(jnp.int32, v[0].shape, 1) // SSM_GROUP
    d = nblk // 2
    while d >= 1:
        keep_lo = (blk & d) == 0
        out = list(v)
        for r in range(nblk):
            if r & d:
                continue
            lo, hi = v[r], v[r + d]
            out[r] = jnp.where(keep_lo, lo, pltpu.roll(hi, d * SSM_GROUP, 1))
            out[r + d] = jnp.where(keep_lo, pltpu.roll(lo, LANES - d * SSM_GROUP, 1), hi)
        v = out
        d //= 2
    return v


def _s5_kernel(u_ref, ma_ref, lag_ref, mo_ref, coef_ref, y_ref, x_scr, z_scr, cin_scr, mi_scr):
    nb, seq = u_ref.shape[1], u_ref.shape[2]
    gpl, halves, rows = x_scr.shape[0:3]
    kx = halves * LANES
    sw = coef_ref.shape[-1]
    n_chunks = seq // CHUNK
    cpt = SUBLANES // nb
    n_tiles = rows // SUBLANES
    span = SUBLANES * CHUNK
    pos_per_half = LANES // SSM_GROUP

    def relayout(co, to_chunks):
        tok0 = pl.multiple_of(co * span, span)
        row0 = pl.multiple_of(co * SUBLANES * nb, SUBLANES * nb)
        for b in range(nb):
            tok_ref = (u_ref if to_chunks else y_ref).at[0, b, pl.ds(tok0, span), :]
            for h in range(halves):
                chunk_refs = [x_scr.at[gp, h, pl.ds(row0, SUBLANES * nb), :] for gp in range(gpl)]
                tok_rows = [pl.ds(h * pos_per_half + ip, SUBLANES, stride=CHUNK)
                            for ip in range(pos_per_half)]
                chunk_rows = pl.ds(b, SUBLANES, stride=nb)
                if to_chunks:
                    dst = _block_transpose([tok_ref[r, :] for r in tok_rows])
                    for gp in range(gpl):
                        chunk_refs[gp][chunk_rows, :] = dst[gp]
                else:
                    dst = _block_transpose([chunk_refs[gp][chunk_rows, :] for gp in range(gpl)])
                    for ip in range(pos_per_half):
                        tok_ref[tok_rows[ip], :] = dst[ip]

    def relayout_in(co, carry):
        relayout(co, True)
        return carry

    def relayout_out(co, carry):
        relayout(co, False)
        return carry

    lax.fori_loop(0, n_chunks // SUBLANES, relayout_in, 0)

    rc = min(rows, 512)
    row_id = lax.broadcasted_iota(jnp.int32, (SUBLANES, sw), 0)
    x_rows = lambda gp, r0: jnp.concatenate(
        [x_scr[gp, h, r0:r0 + rc, :] for h in range(halves)], axis=-1).astype(BF16)
    n_par = z_scr.shape[0]
    for g0 in range(0, gpl, n_par):
        for q in range(n_par):
            for r0 in range(0, rows, rc):
                xr = x_rows(g0 + q, r0)
                for dr in range(ma_ref.shape[1]):
                    z_scr[q, r0:r0 + rc, 2 * dr * sw:2 * (dr + 1) * sw] = jnp.dot(
                        xr, ma_ref[g0 + q, dr], preferred_element_type=F32)
        bc = lambda q, r: jnp.broadcast_to(coef_ref[g0 + q, r:r + 1, :], (SUBLANES, sw))
        coefs = [[bc(q, r) for r in range(4)] for q in range(n_par)]

        def tile_step(ti, carry):
            rf = pl.multiple_of(ti * SUBLANES, SUBLANES)
            rb = pl.multiple_of((n_tiles - 1 - ti) * SUBLANES, SUBLANES)
            out = []
            for q in range(n_par):
                sf, sfp, sb, sbp = carry[4 * q:4 * q + 4]
                af1, af2, ab1, ab2 = coefs[q]
                zf = z_scr[q, pl.ds(rf, SUBLANES), 0:sw]
                zfp = z_scr[q, pl.ds(rf, SUBLANES), sw:2 * sw]
                zb = z_scr[q, pl.ds(rb, SUBLANES), 2 * sw:3 * sw]
                zbp = z_scr[q, pl.ds(rb, SUBLANES), 3 * sw:4 * sw]
                cin_f, cin_b = sf, sb
                for k in range(cpt):
                    nf, nfp = af1 * sf + af2 * sfp + zf, af1 * sfp - af2 * sf + zfp
                    sf, sfp = pltpu.roll(nf, nb, 0), pltpu.roll(nfp, nb, 0)
                    kb = cpt - 1 - k
                    nbk, nbkp = ab1 * sb + ab2 * sbp + zb, ab1 * sbp - ab2 * sb + zbp
                    sb = pltpu.roll(nbk, SUBLANES - nb, 0)
                    sbp = pltpu.roll(nbkp, SUBLANES - nb, 0)
                    if k < cpt - 1:
                        cin_f = jnp.where(row_id >= (k + 1) * nb, sf, cin_f)
                        cin_b = jnp.where(row_id < kb * nb, sb, cin_b)
                cin_scr[q, pl.ds(rf, SUBLANES), 0:sw] = cin_f
                cin_scr[q, pl.ds(rb, SUBLANES), sw:2 * sw] = cin_b
                out += [sf, sfp, sb, sbp]
            return tuple(out)

        zero = jnp.zeros((SUBLANES, sw), F32)
        lax.fori_loop(0, n_tiles, tile_step, (zero,) * (4 * n_par), unroll=2)

        for q in range(n_par):
            gp = g0 + q
            lag = lag_ref[gp]
            for j in range(CHUNK):
                first = (CHUNK - 1 - j) * SSM_GROUP
                mi_scr[q, j * SSM_GROUP:(j + 1) * SSM_GROUP, :] = (
                    lag[:, first:first + kx].astype(BF16))
            for r0 in range(0, rows, rc):
                y = (jnp.dot(x_rows(gp, r0), mi_scr[q], preferred_element_type=F32)
                     + jnp.dot(cin_scr[q, r0:r0 + rc, :].astype(BF16), mo_ref[gp],
                               preferred_element_type=F32))
                for h in range(halves):
                    x_scr[gp, h, r0:r0 + rc, :] = y[:, h * LANES:(h + 1) * LANES]

    lax.fori_loop(0, n_chunks // SUBLANES, relayout_out, 0)


def _s5_branch(u, m_a, lags, m_o, coef, layer):
    nq, b, seq, _ = u.shape
    g = m_a.shape[1]
    gpl = g // nq
    n_chunks = seq // CHUNK
    assert SUBLANES % b == 0 and n_chunks % SUBLANES == 0 and gpl * SSM_GROUP == LANES
    rows = n_chunks * b
    n_dir, kx, ks = m_a.shape[2], m_a.shape[3], m_o.shape[2]
    nz = n_dir * m_a.shape[4]
    return pl.pallas_call(
        _s5_kernel,
        grid=(nq,),
        in_specs=[
            pl.BlockSpec((1, b, seq, LANES), lambda i: (i, 0, 0, 0), pipeline_mode=pl.Buffered(1)),
            pl.BlockSpec((None, gpl, n_dir, kx, nz // n_dir), lambda i: (layer, i, 0, 0, 0)),
            pl.BlockSpec((None, gpl) + lags.shape[2:], lambda i: (layer, i, 0, 0)),
            pl.BlockSpec((None, gpl, ks, kx), lambda i: (layer, i, 0, 0)),
            pl.BlockSpec((None, gpl, SUBLANES, coef.shape[-1]), lambda i: (layer, i, 0, 0)),
        ],
        out_specs=pl.BlockSpec((1, b, seq, LANES), lambda i: (i, 0, 0, 0)),
        out_shape=jax.ShapeDtypeStruct((nq, b, seq, LANES), F32),
        scratch_shapes=[
            pltpu.VMEM((gpl, kx // LANES, rows, LANES), F32),
            pltpu.VMEM((SCAN_GROUPS, rows, nz), F32),
            pltpu.VMEM((SCAN_GROUPS, rows, ks), F32),
            pltpu.VMEM((SCAN_GROUPS, kx, kx), BF16),
        ],
        compiler_params=_params("parallel"),
        name="s5_scan",
    )(u, m_a, lags, m_o, coef)


def _merge_kernel(x_ref, hc_ref, y_ref, gate_ref, wglu_ref, bglu_ref, ws_ref, wc_ref,
                  wout_ref, gn_ref, *rest, moe):
    if moe:
        rw_ref, rb_ref, xo_ref, h_ref, route_ref = rest
    else:
        xo_ref, h_ref = rest
    d = x_ref.shape[-1]
    yg = jax.nn.gelu(jnp.concatenate([y_ref[q, 0] for q in range(y_ref.shape[0])], axis=-1))
    glu = jnp.dot(yg.astype(BF16), wglu_ref[...], preferred_element_type=F32) + bglu_ref[...]
    y2 = yg * jax.nn.sigmoid(glu)
    br_s = jnp.dot(y2.astype(BF16), ws_ref[...], preferred_element_type=F32)
    br_c = jnp.dot(hc_ref[...], wc_ref[...], preferred_element_type=F32)
    gc = gate_ref[:, 0:d].astype(F32)
    gs = gate_ref[:, d:2 * d].astype(F32)
    merged = jax.nn.sigmoid(gc) * br_c + jax.nn.sigmoid(gs) * br_s
    xn = x_ref[...] + jnp.dot(merged.astype(BF16), wout_ref[...], preferred_element_type=F32)
    xo_ref[...] = xn
    h = _rms(xn, gn_ref[...])
    if not moe:
        h_ref[...] = h.astype(h_ref.dtype)
    if moe:
        _rows_to_tiles(h_ref, h)
        h_hi = h.astype(BF16)
        h_lo = (h - h_hi.astype(F32)).astype(BF16)
        logits = (jnp.dot(h_hi, rw_ref[0], preferred_element_type=F32)
                  + jnp.dot(h_hi, rw_ref[1], preferred_element_type=F32)
                  + jnp.dot(h_lo, rw_ref[0], preferred_element_type=F32)) + rb_ref[1:2, :]
        lane = lax.broadcasted_iota(jnp.int32, logits.shape, 1).astype(F32)
        valid = rb_ref[0:1, :] > 0.0
        neg = jnp.float32(-jnp.inf)
        logits = jnp.where(valid, logits, neg)
        m1 = jnp.max(logits, axis=-1, keepdims=True)
        i1 = jnp.min(jnp.where(logits == m1, lane, float(LANES)), axis=-1, keepdims=True)
        rest_l = jnp.where(lane == i1, neg, logits)
        m2 = jnp.max(rest_l, axis=-1, keepdims=True)
        i2 = jnp.min(jnp.where(rest_l == m2, lane, float(LANES)), axis=-1, keepdims=True)
        e2 = jnp.exp(m2 - m1)
        w1 = 1.0 / (1.0 + e2)
        w2 = e2 / (1.0 + e2)
        route_ref[...] = jnp.where(lane == 0.0, i1, jnp.where(lane == 1.0, i2,
                                   jnp.where(lane == 2.0, w1, jnp.where(lane == 3.0, w2, 0.0))))


def _merge(x2, hc, y, gates, layer, wglu, bglu, ws, wc, wout, gn, router=None):
    n, d = x2.shape
    ds_ = hc.shape[1]
    nq, _, seq, _ = y.shape
    tm = ROW_TILE
    per_seq = seq // tm
    moe = router is not None
    row = lambda w: pl.BlockSpec((tm, w), lambda i: (i, 0))
    full = lambda a: pl.BlockSpec(a.shape, lambda i: (0,) * a.ndim)
    y_spec = pl.BlockSpec((nq, 1, tm, LANES), lambda i: (0, i // per_seq, i % per_seq, 0))
    ins = [x2, hc, y, gates, wglu, bglu, ws, wc, wout, gn]
    in_specs = ([row(d), row(ds_), y_spec, row(gates.shape[1])]
                + [_stacked(a, layer) for a in ins[4:]])
    out_specs = [row(d), row(d)]
    out_shape = [jax.ShapeDtypeStruct((n, d), F32), jax.ShapeDtypeStruct((n, d), BF16)]
    if moe:
        tr = d // LANES
        out_specs[1] = pl.BlockSpec((tm * tr, LANES), lambda i: (i, 0))
        out_shape[1] = jax.ShapeDtypeStruct((n * tr, LANES), F32)
        rw, rb = router
        n_exp = rw.shape[1]
        rw_f = jnp.zeros((d, LANES), F32).at[:, :n_exp].set(rw.astype(F32))
        rw_hi = rw_f.astype(BF16)
        rw_pad = jnp.stack([rw_hi, (rw_f - rw_hi.astype(F32)).astype(BF16)])
        rb_pad = jnp.zeros((SUBLANES, LANES), F32).at[0, :n_exp].set(1.0).at[1, :n_exp].set(rb.astype(F32))
        ins += [rw_pad, rb_pad]
        in_specs += [full(rw_pad), full(rb_pad)]
        out_specs.append(row(LANES))
        out_shape.append(jax.ShapeDtypeStruct((n, LANES), F32))
    return pl.pallas_call(
        functools.partial(_merge_kernel, moe=moe),
        grid=(n // tm,),
        in_specs=in_specs,
        out_specs=out_specs,
        out_shape=out_shape,
        compiler_params=_params("parallel"),
        name="merge_moe" if moe else "merge",
    )(*ins)


def _swiglu_rows(h_bf, wg_ref, wu_ref, wd_ref, a_scr, wd_scr=None):
    d_ff = a_scr.shape[1]
    for c0 in range(0, d_ff, FF_CHUNK):
        cs = slice(c0, c0 + FF_CHUNK)
        gt = jnp.dot(h_bf, wg_ref[:, cs].astype(BF16), preferred_element_type=F32)
        up = jnp.dot(h_bf, wu_ref[:, cs].astype(BF16), preferred_element_type=F32)
        a_scr[:, cs] = (gt * jax.nn.sigmoid(gt) * up).astype(BF16)
        if wd_scr is not None:
            wd_scr[cs, :] = wd_ref[cs, :].astype(BF16)
    wd = wd_ref[...] if wd_scr is None else wd_scr[...]
    return jnp.dot(a_scr[...], wd, preferred_element_type=F32)


def _ffn_kernel(x_ref, h_ref, wg_ref, wu_ref, wd_ref, gf_ref, o_ref, a_scr, *, final_norm):
    out = x_ref[...] + _swiglu_rows(h_ref[...], wg_ref, wu_ref, wd_ref, a_scr)
    if final_norm:
        out = _rms(out, gf_ref[...])
    o_ref[...] = out


def _ffn_dense(x2, h, wg, wu, wd, idx, gf, final_norm):
    n, d = x2.shape
    d_ff = wg.shape[-1]
    tm = ROW_TILE
    once = pl.Buffered(1)
    return pl.pallas_call(
        functools.partial(_ffn_kernel, final_norm=final_norm),
        grid=(n // tm,),
        in_specs=[
            pl.BlockSpec((tm, d), lambda i: (i, 0)),
            pl.BlockSpec((tm, d), lambda i: (i, 0)),
            pl.BlockSpec((None, d, d_ff), lambda i: (idx, 0, 0), pipeline_mode=once),
            pl.BlockSpec((None, d, d_ff), lambda i: (idx, 0, 0), pipeline_mode=once),
            pl.BlockSpec((None, d_ff, d), lambda i: (idx, 0, 0), pipeline_mode=once),
            pl.BlockSpec((1, d), lambda i: (0, 0)),
        ],
        out_specs=pl.BlockSpec((tm, d), lambda i: (i, 0)),
        out_shape=jax.ShapeDtypeStruct((n, d), F32),
        scratch_shapes=[pltpu.VMEM((tm, d_ff), BF16)],
        compiler_params=_params("parallel"),
        name="ffn_dense",
    )(x2, h, wg, wu, wd, gf.reshape(1, d))


TOKEN_ROWS = SUBLANES
PERM_UNROLL = 4


def _rows_to_tiles(ref, val):
    m = val.shape[0]
    for j in range(TOKEN_ROWS):
        ref[pl.ds(j, m, stride=TOKEN_ROWS), :] = val[:, j * LANES:(j + 1) * LANES]


def _tiles_to_rows(ref):
    m = ref.shape[0] // TOKEN_ROWS
    return jnp.concatenate([ref[pl.ds(j, m, stride=TOKEN_ROWS), :] for j in range(TOKEN_ROWS)],
                           axis=-1)


def _token_copy(src, dst, sem, s, t):
    rows = lambda i: pl.ds(pl.multiple_of(i * TOKEN_ROWS, TOKEN_ROWS), TOKEN_ROWS)
    return pltpu.make_async_copy(src.at[rows(s), :], dst.at[rows(t), :], sem)


def _for_tokens(n_tok, fn):
    def body(i, carry):
        for j in range(PERM_UNROLL):
            fn(i * PERM_UNROLL + j, j % 2)
        return carry
    lax.fori_loop(0, n_tok // PERM_UNROLL, body, 0)


def _dispatch_kernel(zb_ref, nu_ref, p0_ref, p1_ref, o0_ref, o1_ref, h_ref, xs_ref, zero_scr,
                     sem, ring):
    n_tok = p0_ref.shape[-1]
    n_exp = zb_ref.shape[0]
    step = pl.program_id(0)
    slot = lax.rem(step, 2)

    @pl.when(pl.program_id(0) == 0)
    def _():
        zero_scr[...] = jnp.zeros(zero_scr.shape, F32)
        rows = zero_scr.shape[0]

        def fill(blk):
            start = pl.multiple_of(blk * rows, rows)
            return pltpu.make_async_copy(zero_scr, xs_ref.at[pl.ds(start, rows), :], sem)

        for e in range(n_exp):
            fill(zb_ref[e]).start()
        for e in range(n_exp):
            fill(zb_ref[e]).wait()

        def fill_unused(blk, carry):
            fill(blk).start()
            fill(blk).wait()
            return carry

        lax.fori_loop(nu_ref[0], xs_ref.shape[0] // rows, fill_unused, 0)

    def copies(pa_ref, pb_ref, at_step, sl):
        first = at_step * n_tok
        return lambda r: (
            _token_copy(h_ref, xs_ref, ring.at[sl], first + r, pa_ref[0, 0, r]),
            _token_copy(h_ref, xs_ref, ring.at[sl], first + r, pb_ref[0, 0, r]))

    def drain(cps):
        _for_tokens(n_tok, lambda r, lane: [cp.wait() for cp in cps(r)])

    mine = copies(p0_ref, p1_ref, step, slot)
    _for_tokens(n_tok, lambda r, lane: [cp.start(priority=lane) for cp in mine(r)])

    @pl.when(step > 0)
    def _():
        drain(copies(o0_ref, o1_ref, step - 1, 1 - slot))

    @pl.when(step == pl.num_programs(0) - 1)
    def _():
        drain(mine)


def _dispatch(h, pos0, pos1, zero_blocks, n_used, n_sorted):
    n = h.shape[0] // TOKEN_ROWS
    tp = PERM_TILE
    cur = lambda: pl.BlockSpec((1, 1, tp), lambda i, zb, nu: (i, 0, 0), memory_space=pltpu.SMEM)
    old = lambda: pl.BlockSpec((1, 1, tp), lambda i, zb, nu: (jnp.maximum(i - 1, 0), 0, 0),
                               memory_space=pltpu.SMEM)
    pos0, pos1 = pos0.reshape(n // tp, 1, tp), pos1.reshape(n // tp, 1, tp)
    grid_spec = pltpu.PrefetchScalarGridSpec(
        num_scalar_prefetch=2,
        grid=(n // tp,),
        in_specs=[cur(), cur(), old(), old(), pl.BlockSpec(memory_space=pl.ANY)],
        out_specs=pl.BlockSpec(memory_space=pl.ANY),
        scratch_shapes=[pltpu.VMEM((MOE_TILE * TOKEN_ROWS, LANES), F32),
                        pltpu.SemaphoreType.DMA(()), pltpu.SemaphoreType.DMA((2,))],
    )
    return pl.pallas_call(
        _dispatch_kernel,
        grid_spec=grid_spec,
        out_shape=jax.ShapeDtypeStruct((n_sorted * TOKEN_ROWS, LANES), F32),
        compiler_params=_params("arbitrary"),
        name="moe_dispatch",
    )(zero_blocks, n_used, pos0, pos1, pos0, pos1, h)


def _expert_kernel(be_ref, nu_ref, xs_ref, wg_ref, wu_ref, wd_ref, ys_ref, a_scr, wd_scr):
    @pl.when(pl.program_id(0) < nu_ref[0])
    def _():
        _rows_to_tiles(ys_ref, _swiglu_rows(_tiles_to_rows(xs_ref).astype(BF16), wg_ref, wu_ref,
                                            wd_ref, a_scr, wd_scr))

    @pl.when(pl.program_id(0) >= nu_ref[0])
    def _():
        ys_ref[...] = jnp.zeros(ys_ref.shape, F32)


def _experts(xs, wg, wu, wd, idx, block_expert, n_used):
    d, d_ff = wg.shape[-2:]
    tm = MOE_TILE
    tile = (tm * TOKEN_ROWS, LANES)
    once = pl.Buffered(1)
    grid_spec = pltpu.PrefetchScalarGridSpec(
        num_scalar_prefetch=2,
        grid=(xs.shape[0] // tile[0],),
        in_specs=[
            pl.BlockSpec(tile, lambda i, be, nu: (jnp.minimum(i, nu[0] - 1), 0)),
            pl.BlockSpec((None, None, d, d_ff), lambda i, be, nu: (idx, be[i], 0, 0),
                         pipeline_mode=once),
            pl.BlockSpec((None, None, d, d_ff), lambda i, be, nu: (idx, be[i], 0, 0),
                         pipeline_mode=once),
            pl.BlockSpec((None, None, d_ff, d), lambda i, be, nu: (idx, be[i], 0, 0),
                         pipeline_mode=once),
        ],
        out_specs=pl.BlockSpec(tile, lambda i, be, nu: (i, 0)),
        scratch_shapes=[pltpu.VMEM((tm, d_ff), BF16), pltpu.VMEM((d_ff, d), BF16)],
    )
    return pl.pallas_call(
        _expert_kernel,
        grid_spec=grid_spec,
        out_shape=jax.ShapeDtypeStruct(xs.shape, F32),
        compiler_params=_params("arbitrary"),
        name="moe_experts",
    )(block_expert, n_used, xs, wg, wu, wd)


def _combine_kernel(p0_ref, p1_ref, q0_ref, q1_ref, x_ref, route_ref, gf_ref, ys_ref, o_ref,
                    g_scr, sem, *, final_norm):
    n_tok = x_ref.shape[0]
    step = pl.program_id(0)
    slot = lax.rem(step, 2)

    def copies(pa_ref, pb_ref, sl):
        return lambda r: (
            _token_copy(ys_ref, g_scr.at[sl, 0], sem.at[sl], pa_ref[0, 0, r], r),
            _token_copy(ys_ref, g_scr.at[sl, 1], sem.at[sl], pb_ref[0, 0, r], r))

    def start(cps):
        _for_tokens(n_tok, lambda r, lane: [cp.start(priority=lane) for cp in cps(r)])

    @pl.when(step == 0)
    def _():
        start(copies(p0_ref, p1_ref, 0))

    @pl.when(step + 1 < pl.num_programs(0))
    def _():
        start(copies(q0_ref, q1_ref, 1 - slot))

    mine = copies(p0_ref, p1_ref, slot)
    _for_tokens(n_tok, lambda r, lane: [cp.wait() for cp in mine(r)])
    w1 = route_ref[:, 2:3]
    w2 = route_ref[:, 3:4]
    out = (x_ref[...] + w1 * _tiles_to_rows(g_scr.at[slot, 0])
           + w2 * _tiles_to_rows(g_scr.at[slot, 1]))
    if final_norm:
        out = _rms(out, gf_ref[...])
    o_ref[...] = out


def _combine(x2, route, ys, pos0, pos1, gf, final_norm):
    n, d = x2.shape
    tp = PERM_TILE
    n_steps = n // tp
    cur = lambda: pl.BlockSpec((1, 1, tp), lambda i: (i, 0, 0), memory_space=pltpu.SMEM)
    nxt = lambda: pl.BlockSpec((1, 1, tp), lambda i: (jnp.minimum(i + 1, n_steps - 1), 0, 0),
                               memory_space=pltpu.SMEM)
    pos0, pos1 = pos0.reshape(n_steps, 1, tp), pos1.reshape(n_steps, 1, tp)
    return pl.pallas_call(
        functools.partial(_combine_kernel, final_norm=final_norm),
        grid=(n_steps,),
        in_specs=[cur(), cur(), nxt(), nxt(),
                  pl.BlockSpec((tp, d), lambda i: (i, 0)),
                  pl.BlockSpec((tp, LANES), lambda i: (i, 0)),
                  pl.BlockSpec((1, d), lambda i: (0, 0)),
                  pl.BlockSpec(memory_space=pl.ANY)],
        out_specs=pl.BlockSpec((tp, d), lambda i: (i, 0)),
        out_shape=jax.ShapeDtypeStruct((n, d), F32),
        scratch_shapes=[pltpu.VMEM((2, 2, tp * TOKEN_ROWS, LANES), F32),
                        pltpu.SemaphoreType.DMA((2,))],
        compiler_params=_params("arbitrary"),
        name="moe_combine",
    )(pos0, pos1, pos0, pos1, x2, route, gf.reshape(1, d), ys)


def _routing_tables(route, n_exp):
    n = route.shape[0]
    tm = MOE_TILE
    i1 = route[:, 0].astype(jnp.int32)
    i2 = route[:, 1].astype(jnp.int32)
    oh1 = jax.nn.one_hot(i1, n_exp, dtype=jnp.int32)
    oh2 = jax.nn.one_hot(i2, n_exp, dtype=jnp.int32)
    tot = oh1 + oh2
    csum = jnp.cumsum(tot, axis=0)
    excl = csum - tot
    rank1 = jnp.sum(excl * oh1, axis=-1)
    rank2 = jnp.sum((excl + oh1) * oh2, axis=-1)
    counts = csum[-1]
    nblk = (counts + tm - 1) // tm
    blk_end = jnp.cumsum(nblk)
    blk_off = blk_end - nblk
    row_off = blk_off * tm
    pos0 = row_off[i1] + rank1
    pos1 = row_off[i2] + rank2
    n_blocks = TOP_K * n // tm + n_exp
    n_used = blk_end[-1]
    jb = jnp.minimum(jnp.arange(n_blocks, dtype=jnp.int32), n_used - 1)
    block_expert = jnp.sum((blk_end[None, :] <= jb[:, None]).astype(jnp.int32), axis=-1)
    block_expert = jnp.minimum(block_expert, n_exp - 1)
    zero_blocks = jnp.where(nblk > 0, blk_end - 1, n_blocks - 1).astype(jnp.int32)
    return (pos0.astype(jnp.int32), pos1.astype(jnp.int32), block_expert,
            n_used.astype(jnp.int32).reshape(1), zero_blocks, n_blocks * tm)


def _moe(x2, h, route, wg, wu, wd, idx, gf, final_norm):
    n_exp = wg.shape[1]
    pos0, pos1, block_expert, n_used, zero_blocks, n_sorted = _routing_tables(route, n_exp)
    xs = _dispatch(h, pos0, pos1, zero_blocks, n_used, n_sorted)
    ys = _experts(xs, wg, wu, wd, idx, block_expert, n_used)
    return _combine(x2, route, ys, pos0, pos1, gf, final_norm)


def kernel(x, norm_mix_g, w_in, conv_dw_w, conv_dw_b, conv_ln_g, conv_ln_b, w_conv_proj,
           ssm_a_re, ssm_a_im, ssm_log_dt, ssm_b_re, ssm_b_im, ssm_c_re, ssm_c_im, ssm_d,
           ssm_w_glu, ssm_b_glu, w_ssm_proj, w_out, norm_ffn_g,
           ffn_w_gate, ffn_w_up, ffn_w_down, router_w, router_b,
           moe_w_gate, moe_w_up, moe_w_down, final_norm_g):
    bsz, seq, d = x.shape
    depth = w_in.shape[0]
    d_conv = conv_dw_w.shape[-1]
    d_ssm = ssm_d.shape[-1]
    n = bsz * seq
    bf = lambda a: a.astype(BF16)
    x2 = x.reshape(n, d).astype(F32)
    s5_mats = _s5_matrices(ssm_a_re, ssm_a_im, ssm_log_dt, ssm_b_re, ssm_b_im,
                           ssm_c_re, ssm_c_im, ssm_d)
    mix_g, ffn_g = _rows3(norm_mix_g), _rows3(norm_ffn_g)
    dw_b, ln_g, ln_b = _rows3(conv_dw_b), _rows3(conv_ln_g), _rows3(conv_ln_b)
    b_glu = _rows3(ssm_b_glu)
    w_in_bf, w_glu_bf, w_sp_bf = bf(w_in), bf(ssm_w_glu), bf(w_ssm_proj)
    w_cp_bf, w_out_bf = bf(w_conv_proj), bf(w_out)
    ffn_bf = (bf(ffn_w_gate), bf(ffn_w_up), bf(ffn_w_down))
    moe_w = (moe_w_gate, moe_w_up, moe_w_down)
    for layer in range(depth):
        last = layer == depth - 1
        ch, u, gates = _inproj(x2, mix_g, w_in_bf, layer, bsz, d_conv, d_ssm)
        hc = _conv_branch(ch.reshape(bsz, seq, d_conv), conv_dw_w, dw_b, ln_g, ln_b,
                          layer).reshape(n, d_conv)
        y = _s5_branch(u, *s5_mats, layer)
        i = layer // 2
        moe = layer % 2 == 1
        router = (router_w[i], router_b[i]) if moe else None
        outs = _merge(x2, hc, y, gates, layer, w_glu_bf, b_glu, w_sp_bf, w_cp_bf, w_out_bf,
                      ffn_g, router)
        if moe:
            x2, h, route = outs
            x2 = _moe(x2, h, route, *moe_w, i, final_norm_g, last)
        else:
            x2, h = outs
            x2 = _ffn_dense(x2, h, *ffn_bf, i, final_norm_g, last)
    return x2.reshape(bsz, seq, d).astype(x.dtype)
```

```python
import functools
import math

import jax
import jax.numpy as jnp
from jax import lax
from jax.experimental import pallas as pl
from jax.experimental.pallas import tpu as pltpu

F32 = jnp.float32
BF16 = jnp.bfloat16

RMS_EPS = 1e-6
LN_EPS = 1e-5
CONV_WIDTH = 31
CONV_HALF = CONV_WIDTH // 2
SSM_GROUP = 16
SSM_STATE = 64
CHUNK = 16
TOP_K = 2
LANES = 128
SUBLANES = 8
VMEM_LIMIT = 56 * 1024 * 1024

ROW_TILE = 512
MOE_TILE = 256
PERM_TILE = 512
CONV_TILE = 128
SCAN_GROUPS = 2
FF_CHUNK = 256


def _params(*sem):
    return pltpu.CompilerParams(dimension_semantics=sem, vmem_limit_bytes=VMEM_LIMIT)


def _rms(xf, g):
    ms = jnp.mean(xf * xf, axis=-1, keepdims=True)
    return xf * lax.rsqrt(ms + RMS_EPS) * g


def _stacked(arr, *lead):
    rest = arr.shape[len(lead):]
    return pl.BlockSpec((None,) * len(lead) + rest, lambda *_: tuple(lead) + (0,) * len(rest))


def _rows3(a):
    return a.reshape(a.shape[0], 1, a.shape[1])


def _inproj_kernel(x_ref, g_ref, w_ref, ch_ref, u_ref, gate_ref):
    dc = ch_ref.shape[-1]
    ds_ = u_ref.shape[0] * LANES
    h = _rms(x_ref[...], g_ref[...]).astype(BF16)
    v = jnp.dot(h, w_ref[:, 0:dc], preferred_element_type=F32)
    gt = jnp.dot(h, w_ref[:, dc:2 * dc], preferred_element_type=F32)
    ch_ref[...] = v * jax.nn.sigmoid(gt)
    u = jnp.dot(h, w_ref[:, 2 * dc:2 * dc + ds_], preferred_element_type=F32)
    for q in range(u_ref.shape[0]):
        u_ref[q, 0] = u[:, q * LANES:(q + 1) * LANES]
    gate_ref[...] = jnp.dot(h, w_ref[:, 2 * dc + ds_:], preferred_element_type=F32).astype(BF16)


def _inproj(x2, g, w_bf, layer, bsz, d_conv, d_ssm):
    n, d = x2.shape
    seq = n // bsz
    d_in = w_bf.shape[-1]
    d_gate = d_in - 2 * d_conv - d_ssm
    tm = ROW_TILE
    per_seq = seq // tm
    nq = d_ssm // LANES
    return pl.pallas_call(
        _inproj_kernel,
        grid=(n // tm,),
        in_specs=[
            pl.BlockSpec((tm, d), lambda i: (i, 0)),
            _stacked(g, layer),
            _stacked(w_bf, layer),
        ],
        out_specs=[
            pl.BlockSpec((tm, d_conv), lambda i: (i, 0)),
            pl.BlockSpec((nq, 1, tm, LANES), lambda i: (0, i // per_seq, i % per_seq, 0)),
            pl.BlockSpec((tm, d_gate), lambda i: (i, 0)),
        ],
        out_shape=[
            jax.ShapeDtypeStruct((n, d_conv), F32),
            jax.ShapeDtypeStruct((nq, bsz, seq, LANES), F32),
            jax.ShapeDtypeStruct((n, d_gate), BF16),
        ],
        compiler_params=_params("parallel"),
        name="inproj",
    )(x2, g, w_bf)


def _conv_kernel(h_ref, w_ref, b_ref, lg_ref, lb_ref, o_ref, pad_scr, tile_scr):
    seq, c = h_ref.shape[1], h_ref.shape[2]
    front = 2 * SUBLANES
    zeros = jnp.zeros((front, c), F32)
    pad_scr[0:front, :] = zeros
    pad_scr[front + seq:front + seq + front, :] = zeros
    pad_scr[front:front + seq, :] = h_ref[0]

    phases = [[] for _ in range(SUBLANES)]
    for k in range(CONV_WIDTH):
        off = k - CONV_HALF + front
        phases[off % SUBLANES].append((k, off - off % SUBLANES))

    def tile_body(ti, carry):
        base = pl.multiple_of(ti * CONV_TILE, CONV_TILE)
        win_ref = pad_scr.at[pl.ds(base, CONV_TILE + 2 * front), :]
        for lb in range(c // LANES):
            ls = slice(lb * LANES, (lb + 1) * LANES)
            acc = b_ref[:, ls]
            for s, taps in enumerate(phases):
                part = None
                for k, row0 in taps:
                    term = w_ref[k:k + 1, ls] * win_ref[pl.ds(row0, CONV_TILE + SUBLANES), ls]
                    part = term if part is None else part + term
                if part is not None:
                    acc = acc + part[s:s + CONV_TILE, :]
            tile_scr[:, ls] = acc
        hh = tile_scr[...]
        mu = jnp.mean(hh, axis=-1, keepdims=True)
        cen = hh - mu
        var = jnp.mean(cen * cen, axis=-1, keepdims=True)
        y = cen * lax.rsqrt(var + LN_EPS) * lg_ref[...] + lb_ref[...]
        o_ref[0, pl.ds(base, CONV_TILE), :] = (y * jax.nn.sigmoid(y)).astype(BF16)
        return carry

    lax.fori_loop(0, seq // CONV_TILE, tile_body, 0)


def _conv_branch(ch, dw_w, dw_b, ln_g, ln_b, layer):
    b, seq, c = ch.shape
    front = 2 * SUBLANES
    return pl.pallas_call(
        _conv_kernel,
        grid=(b,),
        in_specs=[
            pl.BlockSpec((1, seq, c), lambda i: (i, 0, 0)),
            _stacked(dw_w, layer), _stacked(dw_b, layer), _stacked(ln_g, layer),
            _stacked(ln_b, layer),
        ],
        out_specs=pl.BlockSpec((1, seq, c), lambda i: (i, 0, 0)),
        out_shape=jax.ShapeDtypeStruct((b, seq, c), BF16),
        scratch_shapes=[
            pltpu.VMEM((seq + 2 * front, c), F32),
            pltpu.VMEM((CONV_TILE, c), F32),
        ],
        compiler_params=_params("parallel"),
        name="conv_branch",
    )(ch, dw_w, dw_b, ln_g, ln_b)


def _s5_matrices(a_re, a_im, log_dt, b_re, b_im, c_re, c_im, d_skip):
    t = CHUNK
    n_layers, n_dir, g, p = a_re.shape
    c = b_re.shape[-1]
    f = lambda a: a.astype(F32)
    cat = lambda *xs: jnp.concatenate(xs, axis=-1)
    a_re, a_im, c_re, c_im = f(a_re), f(a_im), f(c_re), f(c_im)
    dt = jnp.exp(f(log_dt))[..., None]
    lam_re, lam_im = a_re * dt, a_im * dt
    up, down = jnp.arange(t, dtype=F32), jnp.arange(t - 1, -1, -1, dtype=F32)
    expo = jnp.stack([cat(down, up + 1.0, up, jnp.full((1,), t, F32)),
                      cat(up, down + 1.0, down, jnp.full((1,), t, F32))])
    expo = expo[None, :, None, :, None]
    mag = jnp.exp(expo * lam_re[:, :, :, None, :])
    pw_re = mag * jnp.cos(expo * lam_im[:, :, :, None, :])
    pw_im = mag * jnp.sin(expo * lam_im[:, :, :, None, :])
    ab_re = jnp.exp(lam_re) * jnp.cos(lam_im)
    ab_im = jnp.exp(lam_re) * jnp.sin(lam_im)
    n_re, n_im = ab_re - 1.0, ab_im
    den = a_re * a_re + a_im * a_im
    q_re = ((n_re * a_re + n_im * a_im) / den)[:, :, :, None, :]
    q_im = ((n_im * a_re - n_re * a_im) / den)[:, :, :, None, :]
    bt_re = jnp.swapaxes(f(b_re), -1, -2)[:, None]
    bt_im = jnp.swapaxes(f(b_im), -1, -2)[:, None]
    bb_re = q_re * bt_re - q_im * bt_im
    bb_im = q_re * bt_im + q_im * bt_re

    def powers(lo, hi_, reps):
        sl = lambda w: jnp.tile(w[:, :, :, lo:hi_], (1, 1, 1, 1, reps))[:, :, :, :, None, :]
        return sl(pw_re), sl(pw_im)

    kr, ki = powers(0, t, 4)
    z = (kr * cat(bb_re, bb_im, bb_im, bb_re)[:, :, :, None]
         + ki * cat(-bb_im, bb_re, bb_re, -bb_im)[:, :, :, None])
    m_a = jnp.swapaxes(z.reshape(n_layers, n_dir, g, t * c, 4 * p), 1, 2).astype(BF16)
    kr, ki = powers(t, 2 * t, 2)
    v = (kr * cat(c_re, -c_im)[:, :, :, None] + ki * cat(-c_im, -c_re)[:, :, :, None])
    m_o = jnp.swapaxes(v.reshape(n_layers, n_dir, g, t * c, 2 * p), -1, -2)
    pr = jnp.swapaxes(pw_re[:, :, :, 2 * t:3 * t], -1, -2)[..., None]
    pi = jnp.swapaxes(pw_im[:, :, :, 2 * t:3 * t], -1, -2)[..., None]
    cp_re = jnp.swapaxes(c_re, -1, -2)[:, :, :, :, None, :]
    cp_im = jnp.swapaxes(c_im, -1, -2)[:, :, :, :, None, :]
    flat_p = lambda w: w.reshape(n_layers, n_dir, g, p, t * c)
    lag_rhs = jnp.swapaxes(jnp.concatenate([flat_p(pr * cp_re - pi * cp_im),
                                            flat_p(pr * cp_im + pi * cp_re)], axis=3), 1, 2)
    lag_lhs = jnp.swapaxes(cat(bb_re, -bb_im), 1, 2)
    skip = f(d_skip).reshape(n_layers, g, 1, c) * jnp.eye(c, dtype=F32)
    skip = jnp.pad(skip, ((0, 0), (0, 0), (0, 0), ((t - 1) * c, 0)))
    m_o = jnp.concatenate([m_o[:, 0], m_o[:, 1]], axis=2).astype(BF16)
    at_re, at_im = pw_re[:, :, :, 3 * t], pw_im[:, :, :, 3 * t]
    rows = [cat(at_re[:, 0], at_re[:, 0]), cat(-at_im[:, 0], at_im[:, 0]),
            cat(at_re[:, 1], at_re[:, 1]), cat(-at_im[:, 1], at_im[:, 1])]
    coef = jnp.stack(rows + [jnp.zeros_like(rows[0])] * (SUBLANES - len(rows)), axis=2)
    return m_a, lag_lhs, lag_rhs, skip, m_o, coef


def _block_transpose(v):
    nblk = len(v)
    blk = lax.broadcasted_iota(jnp.int32, v[0].shape, 1) // SSM_GROUP
    d = nblk // 2
    while d >= 1:
        keep_lo = (blk & d) == 0
        out = list(v)
        for r in range(nblk):
            if r & d:
                continue
            lo, hi = v[r], v[r + d]
            out[r] = jnp.where(keep_lo, lo, pltpu.roll(hi, d * SSM_GROUP, 1))
            out[r + d] = jnp.where(keep_lo, pltpu.roll(lo, LANES - d * SSM_GROUP, 1), hi)
        v = out
        d //= 2
    return v


def _s5_kernel(u_ref, ma_ref, ll_ref, lr_ref, skip_ref, mo_ref, coef_ref, y_ref, x_scr, z_scr,
               cin_scr, mi_scr):
    nb, seq = u_ref.shape[1], u_ref.shape[2]
    gpl, halves, rows = x_scr.shape[0:3]
    kx = halves * LANES
    sw = coef_ref.shape[-1]
    n_chunks = seq // CHUNK
    cpt = SUBLANES // nb
    n_tiles = rows // SUBLANES
    span = SUBLANES * CHUNK
    pos_per_half = LANES // SSM_GROUP

    def relayout(co, to_chunks):
        tok0 = pl.multiple_of(co * span, span)
        row0 = pl.multiple_of(co * SUBLANES * nb, SUBLANES * nb)
        for b in range(nb):
            tok_ref = (u_ref if to_chunks else y_ref).at[0, b, pl.ds(tok0, span), :]
            for h in range(halves):
                chunk_refs = [x_scr.at[gp, h, pl.ds(row0, SUBLANES * nb), :] for gp in range(gpl)]
                tok_rows = [pl.ds(h * pos_per_half + ip, SUBLANES, stride=CHUNK)
                            for ip in range(pos_per_half)]
                chunk_rows = pl.ds(b, SUBLANES, stride=nb)
                if to_chunks:
                    dst = _block_transpose([tok_ref[r, :] for r in tok_rows])
                    for gp in range(gpl):
                        chunk_refs[gp][chunk_rows, :] = dst[gp]
                else:
                    dst = _block_transpose([chunk_refs[gp][chunk_rows, :] for gp in range(gpl)])
                    for ip in range(pos_per_half):
                        tok_ref[tok_rows[ip], :] = dst[ip]

    def relayout_in(co, carry):
        relayout(co, True)
        return carry

    def relayout_out(co, carry):
        relayout(co, False)
        return carry

    lax.fori_loop(0, n_chunks // SUBLANES, relayout_in, 0)

    rc = min(rows, 512)
    row_id = lax.broadcasted_iota(jnp.int32, (SUBLANES, sw), 0)
    x_rows = lambda gp, r0: jnp.concatenate(
        [x_scr[gp, h, r0:r0 + rc, :] for h in range(halves)], axis=-1).astype(BF16)
    n_par = z_scr.shape[0]
    for g0 in range(0, gpl, n_par):
        for q in range(n_par):
            for r0 in range(0, rows, rc):
                xr = x_rows(g0 + q, r0)
                for dr in range(ma_ref.shape[1]):
                    z_scr[q, r0:r0 + rc, 2 * dr * sw:2 * (dr + 1) * sw] = jnp.dot(
                        xr, ma_ref[g0 + q, dr], preferred_element_type=F32)
        bc = lambda q, r: jnp.broadcast_to(coef_ref[g0 + q, r:r + 1, :], (SUBLANES, sw))
        coefs = [[bc(q, r) for r in range(4)] for q in range(n_par)]

        def tile_step(ti, carry):
            rf = pl.multiple_of(ti * SUBLANES, SUBLANES)
            rb = pl.multiple_of((n_tiles - 1 - ti) * SUBLANES, SUBLANES)
            out = []
            for q in range(n_par):
                sf, sfp, sb, sbp = carry[4 * q:4 * q + 4]
                af1, af2, ab1, ab2 = coefs[q]
                zf = z_scr[q, pl.ds(rf, SUBLANES), 0:sw]
                zfp = z_scr[q, pl.ds(rf, SUBLANES), sw:2 * sw]
                zb = z_scr[q, pl.ds(rb, SUBLANES), 2 * sw:3 * sw]
                zbp = z_scr[q, pl.ds(rb, SUBLANES), 3 * sw:4 * sw]
                cin_f, cin_b = sf, sb
                for k in range(cpt):
                    nf, nfp = af1 * sf + af2 * sfp + zf, af1 * sfp - af2 * sf + zfp
                    sf, sfp = pltpu.roll(nf, nb, 0), pltpu.roll(nfp, nb, 0)
                    kb = cpt - 1 - k
                    nbk, nbkp = ab1 * sb + ab2 * sbp + zb, ab1 * sbp - ab2 * sb + zbp
                    sb = pltpu.roll(nbk, SUBLANES - nb, 0)
                    sbp = pltpu.roll(nbkp, SUBLANES - nb, 0)
                    if k < cpt - 1:
                        cin_f = jnp.where(row_id >= (k + 1) * nb, sf, cin_f)
                        cin_b = jnp.where(row_id < kb * nb, sb, cin_b)
                cin_scr[q, pl.ds(rf, SUBLANES), 0:sw] = cin_f
                cin_scr[q, pl.ds(rb, SUBLANES), sw:2 * sw] = cin_b
                out += [sf, sfp, sb, sbp]
            return tuple(out)

        zero = jnp.zeros((SUBLANES, sw), F32)
        lax.fori_loop(0, n_tiles, tile_step, (zero,) * (4 * n_par), unroll=2)

        for q in range(n_par):
            gp = g0 + q
            hi = lax.Precision.HIGHEST
            kf = jnp.dot(ll_ref[gp, 0], lr_ref[gp, 0], precision=hi, preferred_element_type=F32)
            kb = jnp.dot(ll_ref[gp, 1], lr_ref[gp, 1], precision=hi, preferred_element_type=F32)
            blk = lax.broadcasted_iota(jnp.int32, kf.shape, 1) // SSM_GROUP
            past = kb + skip_ref[gp] + jnp.where(blk == CHUNK - 1,
                                                 pltpu.roll(kf, kx - SSM_GROUP, 1), 0.0)
            for j in range(CHUNK):
                lo = past if j == CHUNK - 1 else pltpu.roll(past, (j + 1) * SSM_GROUP, 1)
                up_ = kf if j == 0 else pltpu.roll(kf, j * SSM_GROUP, 1)
                mi_scr[q, j * SSM_GROUP:(j + 1) * SSM_GROUP, :] = (
                    jnp.where(blk <= j, lo, up_).astype(BF16))
            for r0 in range(0, rows, rc):
                y = (jnp.dot(x_rows(gp, r0), mi_scr[q], preferred_element_type=F32)
                     + jnp.dot(cin_scr[q, r0:r0 + rc, :].astype(BF16), mo_ref[gp],
                               preferred_element_type=F32))
                for h in range(halves):
                    x_scr[gp, h, r0:r0 + rc, :] = y[:, h * LANES:(h + 1) * LANES]

    lax.fori_loop(0, n_chunks // SUBLANES, relayout_out, 0)


def _s5_branch(u, m_a, lag_lhs, lag_rhs, skip, m_o, coef, layer):
    nq, b, seq, _ = u.shape
    g = m_a.shape[1]
    gpl = g // nq
    n_chunks = seq // CHUNK
    assert SUBLANES % b == 0 and n_chunks % SUBLANES == 0 and gpl * SSM_GROUP == LANES
    rows = n_chunks * b
    n_dir, kx, ks = m_a.shape[2], m_a.shape[3], m_o.shape[2]
    nz = n_dir * m_a.shape[4]
    return pl.pallas_call(
        _s5_kernel,
        grid=(nq,),
        in_specs=[
            pl.BlockSpec((1, b, seq, LANES), lambda i: (i, 0, 0, 0), pipeline_mode=pl.Buffered(1)),
            pl.BlockSpec((None, gpl, n_dir, kx, nz // n_dir), lambda i: (layer, i, 0, 0, 0)),
            pl.BlockSpec((None, gpl) + lag_lhs.shape[2:], lambda i: (layer, i, 0, 0, 0)),
            pl.BlockSpec((None, gpl) + lag_rhs.shape[2:], lambda i: (layer, i, 0, 0, 0)),
            pl.BlockSpec((None, gpl) + skip.shape[2:], lambda i: (layer, i, 0, 0)),
            pl.BlockSpec((None, gpl, ks, kx), lambda i: (layer, i, 0, 0)),
            pl.BlockSpec((None, gpl, SUBLANES, coef.shape[-1]), lambda i: (layer, i, 0, 0)),
        ],
        out_specs=pl.BlockSpec((1, b, seq, LANES), lambda i: (i, 0, 0, 0)),
        out_shape=jax.ShapeDtypeStruct((nq, b, seq, LANES), F32),
        scratch_shapes=[
            pltpu.VMEM((gpl, kx // LANES, rows, LANES), F32),
            pltpu.VMEM((SCAN_GROUPS, rows, nz), F32),
            pltpu.VMEM((SCAN_GROUPS, rows, ks), F32),
            pltpu.VMEM((SCAN_GROUPS, kx, kx), BF16),
        ],
        compiler_params=_params("parallel"),
        name="s5_scan",
    )(u, m_a, lag_lhs, lag_rhs, skip, m_o, coef)


def _merge_kernel(x_ref, hc_ref, y_ref, gate_ref, wglu_ref, bglu_ref, ws_ref, wc_ref,
                  wout_ref, gn_ref, *rest, moe):
    if moe:
        rw_ref, rb_ref, xo_ref, h_ref, route_ref = rest
    else:
        xo_ref, h_ref = rest
    d = x_ref.shape[-1]
    yg = jax.nn.gelu(jnp.concatenate([y_ref[q, 0] for q in range(y_ref.shape[0])], axis=-1))
    glu = jnp.dot(yg.astype(BF16), wglu_ref[...], preferred_element_type=F32) + bglu_ref[...]
    y2 = yg * jax.nn.sigmoid(glu)
    br_s = jnp.dot(y2.astype(BF16), ws_ref[...], preferred_element_type=F32)
    br_c = jnp.dot(hc_ref[...], wc_ref[...], preferred_element_type=F32)
    gc = gate_ref[:, 0:d].astype(F32)
    gs = gate_ref[:, d:2 * d].astype(F32)
    merged = jax.nn.sigmoid(gc) * br_c + jax.nn.sigmoid(gs) * br_s
    xn = x_ref[...] + jnp.dot(merged.astype(BF16), wout_ref[...], preferred_element_type=F32)
    xo_ref[...] = xn
    h = _rms(xn, gn_ref[...])
    if not moe:
        h_ref[...] = h.astype(h_ref.dtype)
    if moe:
        _rows_to_tiles(h_ref, h)
        h_hi = h.astype(BF16)
        h_lo = (h - h_hi.astype(F32)).astype(BF16)
        logits = (jnp.dot(h_hi, rw_ref[0], preferred_element_type=F32)
                  + jnp.dot(h_hi, rw_ref[1], preferred_element_type=F32)
                  + jnp.dot(h_lo, rw_ref[0], preferred_element_type=F32)) + rb_ref[1:2, :]
        lane = lax.broadcasted_iota(jnp.int32, logits.shape, 1).astype(F32)
        valid = rb_ref[0:1, :] > 0.0
        neg = jnp.float32(-jnp.inf)
        logits = jnp.where(valid, logits, neg)
        m1 = jnp.max(logits, axis=-1, keepdims=True)
        i1 = jnp.min(jnp.where(logits == m1, lane, float(LANES)), axis=-1, keepdims=True)
        rest_l = jnp.where(lane == i1, neg, logits)
        m2 = jnp.max(rest_l, axis=-1, keepdims=True)
        i2 = jnp.min(jnp.where(rest_l == m2, lane, float(LANES)), axis=-1, keepdims=True)
        e2 = jnp.exp(m2 - m1)
        w1 = 1.0 / (1.0 + e2)
        w2 = e2 / (1.0 + e2)
        route_ref[...] = jnp.where(lane == 0.0, i1, jnp.where(lane == 1.0, i2,
                                   jnp.where(lane == 2.0, w1, jnp.where(lane == 3.0, w2, 0.0))))


def _merge(x2, hc, y, gates, layer, wglu, bglu, ws, wc, wout, gn, router=None):
    n, d = x2.shape
    ds_ = hc.shape[1]
    nq, _, seq, _ = y.shape
    tm = ROW_TILE
    per_seq = seq // tm
    moe = router is not None
    row = lambda w: pl.BlockSpec((tm, w), lambda i: (i, 0))
    full = lambda a: pl.BlockSpec(a.shape, lambda i: (0,) * a.ndim)
    y_spec = pl.BlockSpec((nq, 1, tm, LANES), lambda i: (0, i // per_seq, i % per_seq, 0))
    ins = [x2, hc, y, gates, wglu, bglu, ws, wc, wout, gn]
    in_specs = ([row(d), row(ds_), y_spec, row(gates.shape[1])]
                + [_stacked(a, layer) for a in ins[4:]])
    out_specs = [row(d), row(d)]
    out_shape = [jax.ShapeDtypeStruct((n, d), F32), jax.ShapeDtypeStruct((n, d), BF16)]
    if moe:
        tr = d // LANES
        out_specs[1] = pl.BlockSpec((tm * tr, LANES), lambda i: (i, 0))
        out_shape[1] = jax.ShapeDtypeStruct((n * tr, LANES), F32)
        rw, rb = router
        n_exp = rw.shape[1]
        rw_f = jnp.zeros((d, LANES), F32).at[:, :n_exp].set(rw.astype(F32))
        rw_hi = rw_f.astype(BF16)
        rw_pad = jnp.stack([rw_hi, (rw_f - rw_hi.astype(F32)).astype(BF16)])
        rb_pad = jnp.zeros((SUBLANES, LANES), F32).at[0, :n_exp].set(1.0).at[1, :n_exp].set(rb.astype(F32))
        ins += [rw_pad, rb_pad]
        in_specs += [full(rw_pad), full(rb_pad)]
        out_specs.append(row(LANES))
        out_shape.append(jax.ShapeDtypeStruct((n, LANES), F32))
    return pl.pallas_call(
        functools.partial(_merge_kernel, moe=moe),
        grid=(n // tm,),
        in_specs=in_specs,
        out_specs=out_specs,
        out_shape=out_shape,
        compiler_params=_params("parallel"),
        name="merge_moe" if moe else "merge",
    )(*ins)


def _swiglu_rows(h_bf, wg_ref, wu_ref, wd_ref, a_scr, wd_scr=None):
    d_ff = a_scr.shape[1]
    for c0 in range(0, d_ff, FF_CHUNK):
        cs = slice(c0, c0 + FF_CHUNK)
        gt = jnp.dot(h_bf, wg_ref[:, cs].astype(BF16), preferred_element_type=F32)
        up = jnp.dot(h_bf, wu_ref[:, cs].astype(BF16), preferred_element_type=F32)
        a_scr[:, cs] = (gt * jax.nn.sigmoid(gt) * up).astype(BF16)
        if wd_scr is not None:
            wd_scr[cs, :] = wd_ref[cs, :].astype(BF16)
    wd = wd_ref[...] if wd_scr is None else wd_scr[...]
    return jnp.dot(a_scr[...], wd, preferred_element_type=F32)


def _ffn_kernel(x_ref, h_ref, wg_ref, wu_ref, wd_ref, gf_ref, o_ref, a_scr, *, final_norm):
    out = x_ref[...] + _swiglu_rows(h_ref[...], wg_ref, wu_ref, wd_ref, a_scr)
    if final_norm:
        out = _rms(out, gf_ref[...])
    o_ref[...] = out


def _ffn_dense(x2, h, wg, wu, wd, idx, gf, final_norm):
    n, d = x2.shape
    d_ff = wg.shape[-1]
    tm = ROW_TILE
    once = pl.Buffered(1)
    return pl.pallas_call(
        functools.partial(_ffn_kernel, final_norm=final_norm),
        grid=(n // tm,),
        in_specs=[
            pl.BlockSpec((tm, d), lambda i: (i, 0)),
            pl.BlockSpec((tm, d), lambda i: (i, 0)),
            pl.BlockSpec((None, d, d_ff), lambda i: (idx, 0, 0), pipeline_mode=once),
            pl.BlockSpec((None, d, d_ff), lambda i: (idx, 0, 0), pipeline_mode=once),
            pl.BlockSpec((None, d_ff, d), lambda i: (idx, 0, 0), pipeline_mode=once),
            pl.BlockSpec((1, d), lambda i: (0, 0)),
        ],
        out_specs=pl.BlockSpec((tm, d), lambda i: (i, 0)),
        out_shape=jax.ShapeDtypeStruct((n, d), F32),
        scratch_shapes=[pltpu.VMEM((tm, d_ff), BF16)],
        compiler_params=_params("parallel"),
        name="ffn_dense",
    )(x2, h, wg, wu, wd, gf.reshape(1, d))


TOKEN_ROWS = SUBLANES
PERM_UNROLL = 4


def _rows_to_tiles(ref, val):
    m = val.shape[0]
    for j in range(TOKEN_ROWS):
        ref[pl.ds(j, m, stride=TOKEN_ROWS), :] = val[:, j * LANES:(j + 1) * LANES]


def _tiles_to_rows(ref):
    m = ref.shape[0] // TOKEN_ROWS
    return jnp.concatenate([ref[pl.ds(j, m, stride=TOKEN_ROWS), :] for j in range(TOKEN_ROWS)],
                           axis=-1)


def _token_copy(src, dst, sem, s, t):
    rows = lambda i: pl.ds(pl.multiple_of(i * TOKEN_ROWS, TOKEN_ROWS), TOKEN_ROWS)
    return pltpu.make_async_copy(src.at[rows(s), :], dst.at[rows(t), :], sem)


def _for_tokens(n_tok, fn):
    def body(i, carry):
        for j in range(PERM_UNROLL):
            fn(i * PERM_UNROLL + j, j % 2)
        return carry
    lax.fori_loop(0, n_tok // PERM_UNROLL, body, 0)


def _dispatch_kernel(zb_ref, nu_ref, p0_ref, p1_ref, h_ref, xs_ref, zero_scr, sem):
    n_tok = h_ref.shape[0] // TOKEN_ROWS
    n_exp = zb_ref.shape[0]

    @pl.when(pl.program_id(0) == 0)
    def _():
        zero_scr[...] = jnp.zeros(zero_scr.shape, F32)
        rows = zero_scr.shape[0]

        def fill(blk):
            start = pl.multiple_of(blk * rows, rows)
            return pltpu.make_async_copy(zero_scr, xs_ref.at[pl.ds(start, rows), :], sem)

        for e in range(n_exp):
            fill(zb_ref[e]).start()
        for e in range(n_exp):
            fill(zb_ref[e]).wait()

        def fill_unused(blk, carry):
            fill(blk).start()
            fill(blk).wait()
            return carry

        lax.fori_loop(nu_ref[0], xs_ref.shape[0] // rows, fill_unused, 0)

    def copies(r):
        return (_token_copy(h_ref, xs_ref, sem, r, p0_ref[0, 0, r]),
                _token_copy(h_ref, xs_ref, sem, r, p1_ref[0, 0, r]))

    _for_tokens(n_tok, lambda r, lane: [cp.start(priority=lane) for cp in copies(r)])
    _for_tokens(n_tok, lambda r, lane: [cp.wait() for cp in copies(r)])


def _dispatch(h, pos0, pos1, zero_blocks, n_used, n_sorted):
    n = h.shape[0] // TOKEN_ROWS
    tp = PERM_TILE
    smem_rows = lambda: pl.BlockSpec((1, 1, tp), lambda i, zb, nu: (i, 0, 0),
                                     memory_space=pltpu.SMEM)
    grid_spec = pltpu.PrefetchScalarGridSpec(
        num_scalar_prefetch=2,
        grid=(n // tp,),
        in_specs=[smem_rows(), smem_rows(),
                  pl.BlockSpec((tp * TOKEN_ROWS, LANES), lambda i, zb, nu: (i, 0))],
        out_specs=pl.BlockSpec(memory_space=pl.ANY),
        scratch_shapes=[pltpu.VMEM((MOE_TILE * TOKEN_ROWS, LANES), F32),
                        pltpu.SemaphoreType.DMA(())],
    )
    return pl.pallas_call(
        _dispatch_kernel,
        grid_spec=grid_spec,
        out_shape=jax.ShapeDtypeStruct((n_sorted * TOKEN_ROWS, LANES), F32),
        compiler_params=_params("arbitrary"),
        name="moe_dispatch",
    )(zero_blocks, n_used, pos0.reshape(n // tp, 1, tp), pos1.reshape(n // tp, 1, tp), h)


def _expert_kernel(be_ref, nu_ref, xs_ref, wg_ref, wu_ref, wd_ref, ys_ref, a_scr, wd_scr):
    @pl.when(pl.program_id(0) < nu_ref[0])
    def _():
        _rows_to_tiles(ys_ref, _swiglu_rows(_tiles_to_rows(xs_ref).astype(BF16), wg_ref, wu_ref,
                                            wd_ref, a_scr, wd_scr))

    @pl.when(pl.program_id(0) >= nu_ref[0])
    def _():
        ys_ref[...] = jnp.zeros(ys_ref.shape, F32)


def _experts(xs, wg, wu, wd, idx, block_expert, n_used):
    d, d_ff = wg.shape[-2:]
    tm = MOE_TILE
    tile = (tm * TOKEN_ROWS, LANES)
    once = pl.Buffered(1)
    grid_spec = pltpu.PrefetchScalarGridSpec(
        num_scalar_prefetch=2,
        grid=(xs.shape[0] // tile[0],),
        in_specs=[
            pl.BlockSpec(tile, lambda i, be, nu: (jnp.minimum(i, nu[0] - 1), 0)),
            pl.BlockSpec((None, None, d, d_ff), lambda i, be, nu: (idx, be[i], 0, 0),
                         pipeline_mode=once),
            pl.BlockSpec((None, None, d, d_ff), lambda i, be, nu: (idx, be[i], 0, 0),
                         pipeline_mode=once),
            pl.BlockSpec((None, None, d_ff, d), lambda i, be, nu: (idx, be[i], 0, 0),
                         pipeline_mode=once),
        ],
        out_specs=pl.BlockSpec(tile, lambda i, be, nu: (i, 0)),
        scratch_shapes=[pltpu.VMEM((tm, d_ff), BF16), pltpu.VMEM((d_ff, d), BF16)],
    )
    return pl.pallas_call(
        _expert_kernel,
        grid_spec=grid_spec,
        out_shape=jax.ShapeDtypeStruct(xs.shape, F32),
        compiler_params=_params("arbitrary"),
        name="moe_experts",
    )(block_expert, n_used, xs, wg, wu, wd)


def _combine_kernel(p0_ref, p1_ref, q0_ref, q1_ref, x_ref, route_ref, gf_ref, ys_ref, o_ref,
                    g_scr, sem, *, final_norm):
    n_tok = x_ref.shape[0]
    step = pl.program_id(0)
    slot = lax.rem(step, 2)

    def copies(pa_ref, pb_ref, sl):
        return lambda r: (
            _token_copy(ys_ref, g_scr.at[sl, 0], sem.at[sl], pa_ref[0, 0, r], r),
            _token_copy(ys_ref, g_scr.at[sl, 1], sem.at[sl], pb_ref[0, 0, r], r))

    def start(cps):
        _for_tokens(n_tok, lambda r, lane: [cp.start(priority=lane) for cp in cps(r)])

    @pl.when(step == 0)
    def _():
        start(copies(p0_ref, p1_ref, 0))

    @pl.when(step + 1 < pl.num_programs(0))
    def _():
        start(copies(q0_ref, q1_ref, 1 - slot))

    mine = copies(p0_ref, p1_ref, slot)
    _for_tokens(n_tok, lambda r, lane: [cp.wait() for cp in mine(r)])
    w1 = route_ref[:, 2:3]
    w2 = route_ref[:, 3:4]
    out = (x_ref[...] + w1 * _tiles_to_rows(g_scr.at[slot, 0])
           + w2 * _tiles_to_rows(g_scr.at[slot, 1]))
    if final_norm:
        out = _rms(out, gf_ref[...])
    o_ref[...] = out


def _combine(x2, route, ys, pos0, pos1, gf, final_norm):
    n, d = x2.shape
    tp = PERM_TILE
    n_steps = n // tp
    cur = lambda: pl.BlockSpec((1, 1, tp), lambda i: (i, 0, 0), memory_space=pltpu.SMEM)
    nxt = lambda: pl.BlockSpec((1, 1, tp), lambda i: (jnp.minimum(i + 1, n_steps - 1), 0, 0),
                               memory_space=pltpu.SMEM)
    pos0, pos1 = pos0.reshape(n_steps, 1, tp), pos1.reshape(n_steps, 1, tp)
    return pl.pallas_call(
        functools.partial(_combine_kernel, final_norm=final_norm),
        grid=(n_steps,),
        in_specs=[cur(), cur(), nxt(), nxt(),
                  pl.BlockSpec((tp, d), lambda i: (i, 0)),
                  pl.BlockSpec((tp, LANES), lambda i: (i, 0)),
                  pl.BlockSpec((1, d), lambda i: (0, 0)),
                  pl.BlockSpec(memory_space=pl.ANY)],
        out_specs=pl.BlockSpec((tp, d), lambda i: (i, 0)),
        out_shape=jax.ShapeDtypeStruct((n, d), F32),
        scratch_shapes=[pltpu.VMEM((2, 2, tp * TOKEN_ROWS, LANES), F32),
                        pltpu.SemaphoreType.DMA((2,))],
        compiler_params=_params("arbitrary"),
        name="moe_combine",
    )(pos0, pos1, pos0, pos1, x2, route, gf.reshape(1, d), ys)


def _routing_tables(route, n_exp):
    n = route.shape[0]
    tm = MOE_TILE
    i1 = route[:, 0].astype(jnp.int32)
    i2 = route[:, 1].astype(jnp.int32)
    oh1 = jax.nn.one_hot(i1, n_exp, dtype=jnp.int32)
    oh2 = jax.nn.one_hot(i2, n_exp, dtype=jnp.int32)
    tot = oh1 + oh2
    csum = jnp.cumsum(tot, axis=0)
    excl = csum - tot
    rank1 = jnp.sum(excl * oh1, axis=-1)
    rank2 = jnp.sum((excl + oh1) * oh2, axis=-1)
    counts = csum[-1]
    nblk = (counts + tm - 1) // tm
    blk_end = jnp.cumsum(nblk)
    blk_off = blk_end - nblk
    row_off = blk_off * tm
    pos0 = row_off[i1] + rank1
    pos1 = row_off[i2] + rank2
    n_blocks = TOP_K * n // tm + n_exp
    n_used = blk_end[-1]
    jb = jnp.minimum(jnp.arange(n_blocks, dtype=jnp.int32), n_used - 1)
    block_expert = jnp.sum((blk_end[None, :] <= jb[:, None]).astype(jnp.int32), axis=-1)
    block_expert = jnp.minimum(block_expert, n_exp - 1)
    zero_blocks = jnp.where(nblk > 0, blk_end - 1, n_blocks - 1).astype(jnp.int32)
    return (pos0.astype(jnp.int32), pos1.astype(jnp.int32), block_expert,
            n_used.astype(jnp.int32).reshape(1), zero_blocks, n_blocks * tm)


def _moe(x2, h, route, wg, wu, wd, idx, gf, final_norm):
    n_exp = wg.shape[1]
    pos0, pos1, block_expert, n_used, zero_blocks, n_sorted = _routing_tables(route, n_exp)
    xs = _dispatch(h, pos0, pos1, zero_blocks, n_used, n_sorted)
    ys = _experts(xs, wg, wu, wd, idx, block_expert, n_used)
    return _combine(x2, route, ys, pos0, pos1, gf, final_norm)


def kernel(x, norm_mix_g, w_in, conv_dw_w, conv_dw_b, conv_ln_g, conv_ln_b, w_conv_proj,
           ssm_a_re, ssm_a_im, ssm_log_dt, ssm_b_re, ssm_b_im, ssm_c_re, ssm_c_im, ssm_d,
           ssm_w_glu, ssm_b_glu, w_ssm_proj, w_out, norm_ffn_g,
           ffn_w_gate, ffn_w_up, ffn_w_down, router_w, router_b,
           moe_w_gate, moe_w_up, moe_w_down, final_norm_g):
    bsz, seq, d = x.shape
    depth = w_in.shape[0]
    d_conv = conv_dw_w.shape[-1]
    d_ssm = ssm_d.shape[-1]
    n = bsz * seq
    bf = lambda a: a.astype(BF16)
    x2 = x.reshape(n, d).astype(F32)
    s5_mats = _s5_matrices(ssm_a_re, ssm_a_im, ssm_log_dt, ssm_b_re, ssm_b_im,
                           ssm_c_re, ssm_c_im, ssm_d)
    mix_g, ffn_g = _rows3(norm_mix_g), _rows3(norm_ffn_g)
    dw_b, ln_g, ln_b = _rows3(conv_dw_b), _rows3(conv_ln_g), _rows3(conv_ln_b)
    b_glu = _rows3(ssm_b_glu)
    w_in_bf, w_glu_bf, w_sp_bf = bf(w_in), bf(ssm_w_glu), bf(w_ssm_proj)
    w_cp_bf, w_out_bf = bf(w_conv_proj), bf(w_out)
    ffn_bf = (bf(ffn_w_gate), bf(ffn_w_up), bf(ffn_w_down))
    moe_w = (moe_w_gate, moe_w_up, moe_w_down)
    for layer in range(depth):
        last = layer == depth - 1
        ch, u, gates = _inproj(x2, mix_g, w_in_bf, layer, bsz, d_conv, d_ssm)
        hc = _conv_branch(ch.reshape(bsz, seq, d_conv), conv_dw_w, dw_b, ln_g, ln_b,
                          layer).reshape(n, d_conv)
        y = _s5_branch(u, *s5_mats, layer)
        i = layer // 2
        moe = layer % 2 == 1
        router = (router_w[i], router_b[i]) if moe else None
        outs = _merge(x2, hc, y, gates, layer, w_glu_bf, b_glu, w_sp_bf, w_cp_bf, w_out_bf,
                      ffn_g, router)
        if moe:
            x2, h, route = outs
            x2 = _moe(x2, h, route, *moe_w, i, final_norm_g, last)
        else:
            x2, h = outs
            x2 = _ffn_dense(x2, h, *ffn_bf, i, final_norm_g, last)
    return x2.reshape(bsz, seq, d).astype(x.dtype)
```
